```python
import math
import jax
import jax.numpy as jnp
from jax import lax
import numpy as np

D_MODEL = 1024
BATCH = 2
SEQ = 16384
DEPTH = 4

N_MIXERS = 4
NORM_EPS = 1e-6
Q_BLOCK = 128
ROPE_THETA = 10000.0
FFN_HIDDEN = -(-(8 * D_MODEL // 3) // 256) * 256

HEAD_DIM = 64
NSA_HEADS = D_MODEL // HEAD_DIM
NSA_GROUPS = 4
NSA_CMP_STRIDE = 16
NSA_CMP_LEN = 2 * NSA_CMP_STRIDE
NSA_CMP_HIDDEN = 4 * HEAD_DIM
NSA_SEL_LEN = 64
NSA_N_SEL = 16
NSA_N_LOCAL = 2
NSA_WINDOW = 512
NSA_IN = NSA_HEADS * HEAD_DIM + 6 * NSA_GROUPS * HEAD_DIM + 3 * NSA_HEADS

MLA_HEADS = D_MODEL // 64
MLA_Q_LORA = 3 * D_MODEL // 8
MLA_KV_LORA = D_MODEL // 4
MLA_NOPE = 64
MLA_ROPE = 32
MLA_V = 64
MLA_IN = MLA_Q_LORA + MLA_KV_LORA + MLA_ROPE

RET_HEADS = D_MODEL // 256
RET_QK = 256
RET_V = 2 * RET_QK
RET_CHUNK = 128
RET_IN = 2 * RET_HEADS * RET_QK + 2 * RET_HEADS * RET_V

SSD_INNER = 2 * D_MODEL
SSD_HEADDIM = 64
SSD_HEADS = SSD_INNER // SSD_HEADDIM
SSD_GROUPS = 4
SSD_STATE = 128
SSD_CONV = 4
SSD_CHUNK = 128
SSD_CONV_CH = SSD_INNER + 2 * SSD_GROUPS * SSD_STATE
SSD_IN = SSD_INNER + SSD_CONV_CH + SSD_HEADS

kernel_name = 'hybrid_nsa_mla_retnet_ssd_trunk'


def _n_kind(m):
    return (DEPTH - m + N_MIXERS - 1) // N_MIXERS


def _rms(x, g, eps=NORM_EPS):
    xf = x.astype(jnp.float32)
    y = xf * lax.rsqrt(jnp.mean(jnp.square(xf), axis=-1, keepdims=True) + eps)
    return (y * g.astype(jnp.float32)).astype(x.dtype)


def _masked_softmax(s, mask):
    s = jnp.where(mask, s.astype(jnp.float32), -jnp.inf)
    m = jnp.max(s, axis=-1, keepdims=True)
    m = jnp.where(jnp.isfinite(m), m, 0.0)
    p = jnp.exp(s - m)
    return p / jnp.maximum(jnp.sum(p, axis=-1, keepdims=True), 1e-30)


def _rope(x, pos):
    d = x.shape[-1]
    inv = ROPE_THETA ** (-jnp.arange(0, d, 2, dtype=jnp.float32) / d)
    ang = pos.astype(jnp.float32)[..., None] * inv
    cos = jnp.cos(ang)[:, :, None, :]
    sin = jnp.sin(ang)[:, :, None, :]
    xf = x.astype(jnp.float32)
    x1, x2 = xf[..., : d // 2], xf[..., d // 2:]
    return jnp.concatenate([x1 * cos - x2 * sin, x1 * sin + x2 * cos], axis=-1).astype(x.dtype)


def _nsa_mixer(h, w_in, q_norm, k_norm, cmp_pe, cmp_w1, cmp_w2, w_out):
    B, S, _ = h.shape
    H, G, Dh = NSA_HEADS, NSA_GROUPS, HEAD_DIM
    R = H // G
    kvw = G * Dh
    parts = jnp.split(h @ w_in, np.cumsum([H * Dh] + [kvw] * 6).tolist(), axis=-1)
    q, k_c, v_c, k_s, v_s, k_w, v_w, gate = parts
    q = _rms(q.reshape(B, S, G, R, Dh), q_norm)
    k_c, v_c, k_s, v_s, k_w, v_w = (t.reshape(B, S, G, Dh) for t in (k_c, v_c, k_s, v_s, k_w, v_w))
    gate = jax.nn.sigmoid(gate.astype(jnp.float32)).reshape(B, S, G, R, 3)

    n_cmp = S // NSA_CMP_STRIDE - 1

    def compress(t, pe, w1, w2):
        ch = t.reshape(B, S // NSA_CMP_STRIDE, NSA_CMP_STRIDE, G, Dh)
        blk = jnp.concatenate([ch[:, :-1], ch[:, 1:]], axis=2) + pe[:, None, :]
        flat = jnp.swapaxes(blk, 2, 3).reshape(B, n_cmp, G, NSA_CMP_LEN * Dh)
        return jax.nn.silu(flat @ w1) @ w2

    k_cmp = _rms(compress(k_c, cmp_pe[0], cmp_w1[0], cmp_w2[0]), k_norm[0])
    v_cmp = compress(v_c, cmp_pe[1], cmp_w1[1], cmp_w2[1])
    cmp_end = jnp.arange(n_cmp) * NSA_CMP_STRIDE + NSA_CMP_LEN - 1

    n_blk = S // NSA_SEL_LEN
    top = min(NSA_N_SEL, n_blk)
    ratio = NSA_SEL_LEN // NSA_CMP_STRIDE
    k_sel = _rms(k_s, k_norm[1]).reshape(B, n_blk, NSA_SEL_LEN, G, Dh).transpose(0, 3, 1, 2, 4)
    v_sel = v_s.reshape(B, n_blk, NSA_SEL_LEN, G, Dh).transpose(0, 3, 1, 2, 4)
    blk_ids = jnp.arange(n_blk)
    bi = jnp.arange(B)[:, None, None, None]
    gi = jnp.arange(G)[None, :, None, None]

    pad = ((0, 0), (NSA_WINDOW, 0), (0, 0), (0, 0))
    k_win = jnp.pad(_rms(k_w, k_norm[2]), pad)
    v_win = jnp.pad(v_w, pad)
    scale = Dh ** -0.5

    def block(qb):
        q0 = qb * Q_BLOCK
        tpos = q0 + jnp.arange(Q_BLOCK)
        qi = lax.dynamic_slice_in_dim(q, q0, Q_BLOCK, axis=1)
        gb = lax.dynamic_slice_in_dim(gate, q0, Q_BLOCK, axis=1)
        s = jnp.einsum('bqgrd,bcgd->bgrqc', qi, k_cmp) * scale
        p_cmp = _masked_softmax(s, cmp_end[None, :] <= tpos[:, None])
        o_cmp = jnp.einsum('bgrqc,bcgd->bqgrd', p_cmp.astype(v_cmp.dtype), v_cmp)
        imp = jnp.pad(p_cmp.sum(axis=2), ((0, 0), (0, 0), (0, 0), (1, 1)))
        imp = imp[..., : ratio * n_blk].reshape(B, G, Q_BLOCK, n_blk, ratio).sum(-1) + imp[..., ratio::ratio]
        cur = tpos // NSA_SEL_LEN
        causal_blk = blk_ids[None, :] <= cur[:, None]
        forced = causal_blk & ((blk_ids[None, :] == 0) | (blk_ids[None, :] > cur[:, None] - NSA_N_LOCAL))
        score = jnp.where(forced, jnp.inf, jnp.where(causal_blk, imp, -jnp.inf))
        _, idx = lax.top_k(score, top)
        ks_g = k_sel[bi, gi, idx]
        vs_g = v_sel[bi, gi, idx]
        tok = idx[..., None] * NSA_SEL_LEN + jnp.arange(NSA_SEL_LEN)
        smask = (tok <= tpos[None, None, :, None, None]).reshape(B, G, 1, Q_BLOCK, top * NSA_SEL_LEN)
        s = jnp.einsum('bqgrd,bgqnld->bgrqnl', qi, ks_g).reshape(B, G, R, Q_BLOCK, top * NSA_SEL_LEN) * scale
        p = _masked_softmax(s, smask).reshape(B, G, R, Q_BLOCK, top, NSA_SEL_LEN)
        o_sel = jnp.einsum('bgrqnl,bgqnld->bqgrd', p.astype(vs_g.dtype), vs_g)
        kwi = lax.dynamic_slice_in_dim(k_win, q0, Q_BLOCK + NSA_WINDOW, axis=1)
        vwi = lax.dynamic_slice_in_dim(v_win, q0, Q_BLOCK + NSA_WINDOW, axis=1)
        kpos = q0 - NSA_WINDOW + jnp.arange(Q_BLOCK + NSA_WINDOW)
        wmask = (kpos[None, :] <= tpos[:, None]) & (kpos[None, :] > tpos[:, None] - NSA_WINDOW) & (kpos[None, :] >= 0)
        s = jnp.einsum('bqgrd,bkgd->bgrqk', qi, kwi) * scale
        p = _masked_softmax(s, wmask)
        o_win = jnp.einsum('bgrqk,bkgd->bqgrd', p.astype(vwi.dtype), vwi)
        o = gb[..., 0:1] * o_cmp + gb[..., 1:2] * o_sel + gb[..., 2:3] * o_win
        return o.reshape(B, Q_BLOCK, H * Dh).astype(h.dtype)

    o = lax.map(block, jnp.arange(S // Q_BLOCK))
    o = jnp.moveaxis(o, 0, 1).reshape(B, S, H * Dh)
    return o @ w_out


def _mla_mixer(h, pos, w_in, q_a_norm, kv_a_norm, w_q_b, w_kv_b, q_norm, k_norm, w_out):
    B, S, _ = h.shape
    H, dq = MLA_HEADS, MLA_NOPE + MLA_ROPE
    cq, ckv, k_rot = jnp.split(h @ w_in, [MLA_Q_LORA, MLA_Q_LORA + MLA_KV_LORA], axis=-1)
    q = _rms((_rms(cq, q_a_norm) @ w_q_b).reshape(B, S, H, dq), q_norm)
    kv = (_rms(ckv, kv_a_norm) @ w_kv_b).reshape(B, S, H, MLA_NOPE + MLA_V)
    k_nope = _rms(kv[..., :MLA_NOPE], k_norm[:MLA_NOPE])
    v = kv[..., MLA_NOPE:]
    k_rot = _rope(_rms(k_rot, k_norm[MLA_NOPE:])[:, :, None, :], pos)
    q = jnp.concatenate([q[..., :MLA_NOPE], _rope(q[..., MLA_NOPE:], pos)], axis=-1)
    k = jnp.concatenate([k_nope, jnp.broadcast_to(k_rot, (B, S, H, MLA_ROPE))], axis=-1)
    scale = dq ** -0.5
    key_pos = jnp.arange(S)

    def block(qb):
        q0 = qb * Q_BLOCK
        qi = lax.dynamic_slice_in_dim(q, q0, Q_BLOCK, axis=1)
        s = jnp.einsum('bqhd,bkhd->bhqk', qi, k) * scale
        p = _masked_softmax(s, key_pos[None, :] <= (q0 + jnp.arange(Q_BLOCK))[:, None])
        return jnp.einsum('bhqk,bkhd->bqhd', p.astype(v.dtype), v)

    o = lax.map(block, jnp.arange(S // Q_BLOCK))
    o = jnp.moveaxis(o, 0, 1).reshape(B, S, H * MLA_V)
    return o @ w_out


def _retention_mixer(h, pos, w_in, gn_w, w_out):
    B, S, _ = h.shape
    H, dk, dv, C = RET_HEADS, RET_QK, RET_V, RET_CHUNK
    q, k, v, g = jnp.split(h @ w_in, np.cumsum([H * dk, H * dk, H * dv]).tolist(), axis=-1)
    q = _rope(q.reshape(B, S, H, dk), pos)
    k = _rope(k.reshape(B, S, H, dk), pos) * (dk ** -0.5)
    v = v.reshape(B, S, H, dv)
    n_ch = S // C

    def chunks(t):
        return t.astype(jnp.float32).reshape(B, n_ch, C, H, t.shape[-1]).transpose(1, 0, 3, 2, 4)

    log_gamma = jnp.log1p(-(2.0 ** (-5.0 - jnp.arange(H, dtype=jnp.float32))))
    n = jnp.arange(C, dtype=jnp.float32)
    rel = n[:, None] - n[None, :]
    decay_in = jnp.where(rel >= 0, jnp.exp(jnp.maximum(rel, 0.0)[None] * log_gamma[:, None, None]), 0.0)
    xi = jnp.exp((n + 1.0)[None] * log_gamma[:, None])
    zeta = jnp.exp((C - 1.0 - n)[None] * log_gamma[:, None])
    gamma_c = jnp.exp(C * log_gamma)

    def step(state, inp):
        qc, kc, vc = inp
        inner = jnp.einsum('bhqk,bhke->bhqe', jnp.einsum('bhqd,bhkd->bhqk', qc, kc) * decay_in, vc)
        cross = jnp.einsum('bhqd,bhde->bhqe', qc, state) * xi[None, :, :, None]
        state = gamma_c[None, :, None, None] * state + jnp.einsum('bhkd,bhke->bhde', kc * zeta[None, :, :, None], vc)
        return state, inner + cross

    state0 = jnp.zeros((B, H, dk, dv), jnp.float32)
    _, y = lax.scan(step, state0, (chunks(q), chunks(k), chunks(v)))
    y = y.transpose(1, 0, 3, 2, 4).reshape(B, S, H, dv)
    mu = jnp.mean(y, axis=-1, keepdims=True)
    var = jnp.mean(jnp.square(y - mu), axis=-1, keepdims=True)
    y = ((y - mu) * lax.rsqrt(var + NORM_EPS)).reshape(B, S, H * dv) * gn_w.astype(jnp.float32)
    return (jax.nn.silu(g.astype(jnp.float32)) * y).astype(h.dtype) @ w_out


def _ssd_mixer(h, w_in, conv_w, conv_b, dt_bias, a_log, d_skip, norm_w, w_out):
    B, S, _ = h.shape
    H, P, G, N, C = SSD_HEADS, SSD_HEADDIM, SSD_GROUPS, SSD_STATE, SSD_CHUNK
    R = H // G
    z, xbc, dt = jnp.split(h @ w_in, [SSD_INNER, SSD_INNER + SSD_CONV_CH], axis=-1)
    xbc = lax.conv_general_dilated(xbc, conv_w[:, None, :], window_strides=(1,), padding=[(SSD_CONV - 1, 0)],
                                   dimension_numbers=('NWC', 'WIO', 'NWC'), feature_group_count=SSD_CONV_CH)
    xbc = jax.nn.silu(xbc + conv_b)
    xs, bm, cm = jnp.split(xbc, [SSD_INNER, SSD_INNER + G * N], axis=-1)
    xs = xs.reshape(B, S, H, P)
    dt = jax.nn.softplus(dt.astype(jnp.float32) + dt_bias.astype(jnp.float32))
    a = -jnp.exp(a_log.astype(jnp.float32))
    n_ch = S // C

    def chunks(t):
        return t.astype(jnp.float32).reshape((B, n_ch, C) + t.shape[2:]).swapaxes(0, 1)

    tril = jnp.tril(jnp.ones((C, C), dtype=bool))

    def step(state, inp):
        xc, dtc, bc, cc = inp
        cum = jnp.cumsum(dtc * a, axis=1)
        seg = jnp.where(tril[None, :, :, None], cum[:, :, None, :] - cum[:, None, :, :], -jnp.inf)
        w = jnp.exp(seg).reshape(B, C, C, G, R) * jnp.einsum('btgn,bsgn->btsg', cc, bc)[..., None] \
            * dtc.reshape(B, 1, C, G, R)
        xg = xc.reshape(B, C, G, R, P)
        y = jnp.einsum('btsgr,bsgrp->btgrp', w, xg)
        y = y + jnp.einsum('btgn,bgrpn->btgrp', cc, state) * jnp.exp(cum).reshape(B, C, G, R)[..., None]
        w_end = (jnp.exp(cum[:, -1:, :] - cum) * dtc).reshape(B, C, G, R)
        state = jnp.exp(cum[:, -1, :]).reshape(B, G, R)[..., None, None] * state \
            + jnp.einsum('bsgn,bsgr,bsgrp->bgrpn', bc, w_end, xg)
        return state, y

    state0 = jnp.zeros((B, G, R, P, N), jnp.float32)
    _, y = lax.scan(step, state0, (chunks(xs), chunks(dt), chunks(bm.reshape(B, S, G, N)), chunks(cm.reshape(B, S, G, N))))
    y = y.swapaxes(0, 1).reshape(B, S, H, P) + d_skip.astype(jnp.float32)[:, None] * xs.astype(jnp.float32)
    yg = (y.reshape(B, S, SSD_INNER) * jax.nn.silu(z.astype(jnp.float32))).reshape(B, S, G, SSD_INNER // G)
    y = _rms(yg, norm_w.reshape(G, SSD_INNER // G)).reshape(B, S, SSD_INNER)
    return y.astype(h.dtype) @ w_out


def _swiglu(h, w_in, w_out):
    a, b = jnp.split(h @ w_in, 2, axis=-1)
    return (jax.nn.silu(a) * b) @ w_out


def setup_inputs(seed: int = 0) -> dict:
    key = jax.random.key(seed)
    ks = iter(jax.random.split(key, 64))
    f32 = jnp.float32

    def nrm(shape, scale):
        return jax.random.normal(next(ks), shape, f32) * scale

    def gain(shape):
        return 1.0 + nrm(shape, 0.05)

    la, lb, lc, ld = (_n_kind(m) for m in range(N_MIXERS))
    D = D_MODEL
    positions = jax.random.randint(next(ks), (BATCH, 1), 0, 4096, jnp.int32) + jnp.arange(SEQ, dtype=jnp.int32)[None, :]
    dt0 = jnp.exp(jax.random.uniform(next(ks), (ld, SSD_HEADS), f32, math.log(1e-3), math.log(1e-1)))
    return {
        'x': nrm((BATCH, SEQ, D), 1.0),
        'c': nrm((BATCH, D), 1.0),
        'positions': positions,
        'ada_w': nrm((DEPTH, D, 6 * D), 0.5 * D ** -0.5),
        'ada_b': nrm((DEPTH, 6 * D), 0.02),
        'norm_mix': gain((DEPTH, D)),
        'norm_ffn': gain((DEPTH, D)),
        'ffn_w_in': nrm((DEPTH, D, 2 * FFN_HIDDEN), D ** -0.5),
        'ffn_w_out': nrm((DEPTH, FFN_HIDDEN, D), FFN_HIDDEN ** -0.5),
        'nsa_w_in': nrm((la, D, NSA_IN), D ** -0.5),
        'nsa_q_norm': gain((la, HEAD_DIM)),
        'nsa_k_norm': gain((la, 3, HEAD_DIM)),
        'nsa_cmp_pe': nrm((la, 2, NSA_CMP_LEN, HEAD_DIM), 0.1),
        'nsa_cmp_w1': nrm((la, 2, NSA_CMP_LEN * HEAD_DIM, NSA_CMP_HIDDEN), (NSA_CMP_LEN * HEAD_DIM) ** -0.5),
        'nsa_cmp_w2': nrm((la, 2, NSA_CMP_HIDDEN, HEAD_DIM), NSA_CMP_HIDDEN ** -0.5),
        'nsa_w_out': nrm((la, NSA_HEADS * HEAD_DIM, D), (NSA_HEADS * HEAD_DIM) ** -0.5),
        'mla_w_in': nrm((lb, D, MLA_IN), D ** -0.5),
        'mla_q_a_norm': gain((lb, MLA_Q_LORA)),
        'mla_kv_a_norm': gain((lb, MLA_KV_LORA)),
        'mla_w_q_b': nrm((lb, MLA_Q_LORA, MLA_HEADS * (MLA_NOPE + MLA_ROPE)), MLA_Q_LORA ** -0.5),
        'mla_w_kv_b': nrm((lb, MLA_KV_LORA, MLA_HEADS * (MLA_NOPE + MLA_V)), MLA_KV_LORA ** -0.5),
        'mla_q_norm': gain((lb, MLA_NOPE + MLA_ROPE)),
        'mla_k_norm': gain((lb, MLA_NOPE + MLA_ROPE)),
        'mla_w_out': nrm((lb, MLA_HEADS * MLA_V, D), (MLA_HEADS * MLA_V) ** -0.5),
        'ret_w_in': nrm((lc, D, RET_IN), D ** -0.5),
        'ret_gn_w': gain((lc, RET_HEADS * RET_V)),
        'ret_w_out': nrm((lc, RET_HEADS * RET_V, D), (RET_HEADS * RET_V) ** -0.5),
        'ssd_w_in': nrm((ld, D, SSD_IN), D ** -0.5),
        'ssd_conv_w': nrm((ld, SSD_CONV, SSD_CONV_CH), SSD_CONV ** -0.5),
        'ssd_conv_b': nrm((ld, SSD_CONV_CH), 0.02),
        'ssd_dt_bias': dt0 + jnp.log(-jnp.expm1(-dt0)),
        'ssd_a_log': jnp.log(jax.random.uniform(next(ks), (ld, SSD_HEADS), f32, 1.0, 16.0)),
        'ssd_d': 1.0 + nrm((ld, SSD_HEADS), 0.1),
        'ssd_norm': gain((ld, SSD_INNER)),
        'ssd_w_out': nrm((ld, SSD_INNER, D), SSD_INNER ** -0.5),
    }


def reference(x, c, positions, ada_w, ada_b, norm_mix, norm_ffn, ffn_w_in, ffn_w_out,
              nsa_w_in, nsa_q_norm, nsa_k_norm, nsa_cmp_pe, nsa_cmp_w1, nsa_cmp_w2, nsa_w_out,
              mla_w_in, mla_q_a_norm, mla_kv_a_norm, mla_w_q_b, mla_w_kv_b, mla_q_norm, mla_k_norm, mla_w_out,
              ret_w_in, ret_gn_w, ret_w_out,
              ssd_w_in, ssd_conv_w, ssd_conv_b, ssd_dt_bias, ssd_a_log, ssd_d, ssd_norm, ssd_w_out):
    cond = jax.nn.silu(c)
    for i in range(DEPTH):
        mod = cond @ ada_w[i] + ada_b[i]
        sh_m, sc_m, g_m, sh_f, sc_f, g_f = [t[:, None, :] for t in jnp.split(mod, 6, axis=-1)]
        hm = _rms(x, norm_mix[i]) * (1.0 + sc_m) + sh_m
        kind, j = i % N_MIXERS, i // N_MIXERS
        if kind == 0:
            y = _nsa_mixer(hm, nsa_w_in[j], nsa_q_norm[j], nsa_k_norm[j], nsa_cmp_pe[j],
                           nsa_cmp_w1[j], nsa_cmp_w2[j], nsa_w_out[j])
        elif kind == 1:
            y = _mla_mixer(hm, positions, mla_w_in[j], mla_q_a_norm[j], mla_kv_a_norm[j], mla_w_q_b[j],
                           mla_w_kv_b[j], mla_q_norm[j], mla_k_norm[j], mla_w_out[j])
        elif kind == 2:
            y = _retention_mixer(hm, positions, ret_w_in[j], ret_gn_w[j], ret_w_out[j])
        else:
            y = _ssd_mixer(hm, ssd_w_in[j], ssd_conv_w[j], ssd_conv_b[j], ssd_dt_bias[j], ssd_a_log[j],
                           ssd_d[j], ssd_norm[j], ssd_w_out[j])
        x = x + g_m * y.astype(x.dtype)
        hf = _rms(x, norm_ffn[i]) * (1.0 + sc_f) + sh_f
        x = x + g_f * _swiglu(hf, ffn_w_in[i], ffn_w_out[i])
    return x
```

```python
import functools
import math

import numpy as np
import jax
import jax.numpy as jnp
from jax import lax
from jax.experimental import pallas as pl
from jax.experimental.pallas import tpu as pltpu

F32 = jnp.float32
BF16 = jnp.bfloat16

NORM_EPS = 1e-6
ROPE_THETA = 10000.0
LANES = 128

HEAD_DIM = 64
NSA_GROUPS = 4
NSA_CMP_STRIDE = 16
NSA_CMP_LEN = 32
NSA_SEL_LEN = 64
NSA_N_SEL = 16
NSA_N_LOCAL = 2
NSA_WINDOW = 512
NSA_TQ = 128
NSA_TK = 512
SEL_ONEHOT = 128
SEL_MASK_BIAS = -32768.0

MLA_Q_LORA = 384
MLA_KV_LORA = 256
MLA_NOPE = 64
MLA_ROPE = 32
MLA_V = 64
MLA_TQ = 512
MLA_TK = 512

RET_QK = 256
RET_V = 512
RET_CHUNK = 128

SSD_HEADDIM = 64
SSD_GROUPS = 4
SSD_STATE = 128
SSD_CONV = 4
SSD_CHUNK = 128

VMEM_LIMIT = 48 * 1024 * 1024


def _params(sem, vmem=VMEM_LIMIT):
    return pltpu.CompilerParams(dimension_semantics=sem, vmem_limit_bytes=vmem)


def _sigmoid(x):
    return 1.0 / (1.0 + jnp.exp(-x))


def _silu(x):
    return x * _sigmoid(x)


def _dot(a, b):
    return jnp.dot(a, b, preferred_element_type=F32)


def _dot_nt(a, b):
    return lax.dot_general(a, b, (((1,), (1,)), ((), ())), preferred_element_type=F32)


def _split3(x):
    hi = x.astype(BF16)
    r1 = x - hi.astype(F32)
    mid = r1.astype(BF16)
    lo = (r1 - mid.astype(F32)).astype(BF16)
    return hi, mid, lo


def _dot3(x, m):
    hi, mid, lo = _split3(x)
    return _dot(hi, m) + _dot(mid, m) + _dot(lo, m)


def _dot3_left(m, x):
    hi, mid, lo = _split3(x)
    return _dot(m, hi) + _dot(m, mid) + _dot(m, lo)


def _rms(x, gain):
    ms = jnp.mean(x * x, axis=-1, keepdims=True)
    return x * lax.rsqrt(ms + NORM_EPS) * gain


def _mod_kernel(c_ref, w_ref, b_ref, o_ref):
    cond = _silu(c_ref[...]).astype(BF16)
    o_ref[...] = _dot(cond, w_ref[...]) + b_ref[...]


def _modulation(c, ada_w, ada_b):
    depth, d, n = ada_w.shape
    b = c.shape[0]
    rows = 16
    c_pad = jnp.zeros((rows, d), F32).at[:b].set(c)
    tn = 1024
    out = pl.pallas_call(
        _mod_kernel,
        out_shape=jax.ShapeDtypeStruct((depth, rows, n), F32),
        grid=(depth, n // tn),
        in_specs=[
            pl.BlockSpec((rows, d), lambda l, j: (0, 0)),
            pl.BlockSpec((None, d, tn), lambda l, j: (l, 0, j)),
            pl.BlockSpec((None, 1, tn), lambda l, j: (l, 0, j)),
        ],
        out_specs=pl.BlockSpec((None, rows, tn), lambda l, j: (l, 0, j)),
        compiler_params=_params(("parallel", "parallel")),
        name="adaln_mod",
    )(c_pad, ada_w.astype(BF16), ada_b.reshape(depth, 1, n))
    return out[:, :b]


def _modnorm_matmul_kernel(x_ref, g_ref, sc_ref, sh_ref, w_ref, o_ref, h_scr):
    @pl.when(pl.program_id(1) == 0)
    def _():
        y = _rms(x_ref[...], g_ref[...])
        h_scr[...] = (y * (1.0 + sc_ref[...]) + sh_ref[...]).astype(BF16)

    o_ref[...] = _dot(h_scr[...], w_ref[...]).astype(o_ref.dtype)


def _pad_and_tile(n, max_tile=1024, min_tile=512):
    n_pad = -(-n // LANES) * LANES
    while True:
        if n_pad <= max_tile:
            return n_pad, n_pad
        for tn in range(max_tile, min_tile - 1, -LANES):
            if n_pad % tn == 0:
                return n_pad, tn
        n_pad += LANES


def _modnorm_matmul(x, gain, sc, sh, w, seq, out_dtype=F32, tm=512):
    t, d = x.shape
    n = w.shape[1]
    n_pad, tn = _pad_and_tile(n)
    w = w.astype(BF16)
    if n_pad != n:
        w = jnp.pad(w, ((0, 0), (0, n_pad - n)))
    tpb = seq // tm
    b = sc.shape[0]
    return pl.pallas_call(
        _modnorm_matmul_kernel,
        out_shape=jax.ShapeDtypeStruct((t, n_pad), out_dtype),
        grid=(t // tm, n_pad // tn),
        in_specs=[
            pl.BlockSpec((tm, d), lambda i, j: (i, 0)),
            pl.BlockSpec((1, d), lambda i, j: (0, 0)),
            pl.BlockSpec((None, 1, d), lambda i, j: (i // tpb, 0, 0)),
            pl.BlockSpec((None, 1, d), lambda i, j: (i // tpb, 0, 0)),
            pl.BlockSpec((d, tn), lambda i, j: (0, j)),
        ],
        out_specs=pl.BlockSpec((tm, tn), lambda i, j: (i, j)),
        scratch_shapes=[pltpu.VMEM((tm, d), BF16)],
        compiler_params=_params(("parallel", "arbitrary")),
        name="modnorm_matmul",
    )(x, gain.reshape(1, d), sc.reshape(b, 1, d), sh.reshape(b, 1, d), w)


def _norm_matmul(x, gain, w, out_dtype=F32):
    t, d = x.shape
    zeros = jnp.zeros((1, d), F32)
    return _modnorm_matmul(x, gain, zeros, zeros, w, seq=t, out_dtype=out_dtype)


def _out_proj_kernel(y_ref, w_ref, x_ref, gate_ref, o_ref):
    o_ref[...] = x_ref[...] + gate_ref[...] * _dot(y_ref[...], w_ref[...])


def _out_proj(y, w, x, gate, seq, tm=512):
    t, k = y.shape
    d = w.shape[1]
    b = gate.shape[0]
    tpb = seq // tm
    return pl.pallas_call(
        _out_proj_kernel,
        out_shape=jax.ShapeDtypeStruct((t, d), F32),
        grid=(t // tm,),
        in_specs=[
            pl.BlockSpec((tm, k), lambda i: (i, 0)),
            pl.BlockSpec((k, d), lambda i: (0, 0)),
            pl.BlockSpec((tm, d), lambda i: (i, 0)),
            pl.BlockSpec((None, 1, d), lambda i: (i // tpb, 0, 0)),
        ],
        out_specs=pl.BlockSpec((tm, d), lambda i: (i, 0)),
        compiler_params=_params(("parallel",)),
        name="out_proj",
    )(y, w.astype(BF16), x, gate.reshape(b, 1, d))


def _ffn_kernel(x_ref, g_ref, sc_ref, sh_ref, wa_ref, wb_ref, wo_ref, gate_ref, o_ref, h_scr, acc_scr):
    j = pl.program_id(1)

    @pl.when(j == 0)
    def _():
        y = _rms(x_ref[...], g_ref[...])
        h_scr[...] = (y * (1.0 + sc_ref[...]) + sh_ref[...]).astype(BF16)
        acc_scr[...] = jnp.zeros_like(acc_scr)

    h = h_scr[...]
    a = _dot(h, wa_ref[...])
    b = _dot(h, wb_ref[...])
    u = (_silu(a) * b).astype(BF16)
    acc_scr[...] += _dot(u, wo_ref[...])

    @pl.when(j == pl.num_programs(1) - 1)
    def _():
        o_ref[...] = x_ref[...] + gate_ref[...] * acc_scr[...]


def _ffn(x, gain, sc, sh, gate, w_in, w_out, seq, tm=512):
    t, d = x.shape
    hid = w_out.shape[0]
    n_h = 2
    th = hid // n_h
    assert th % LANES == 0
    b = sc.shape[0]
    tpb = seq // tm
    w_in = w_in.astype(BF16)
    w_out = w_out.astype(BF16)
    return pl.pallas_call(
        _ffn_kernel,
        out_shape=jax.ShapeDtypeStruct((t, d), F32),
        grid=(t // tm, n_h),
        in_specs=[
            pl.BlockSpec((tm, d), lambda i, j: (i, 0)),
            pl.BlockSpec((1, d), lambda i, j: (0, 0)),
            pl.BlockSpec((None, 1, d), lambda i, j: (i // tpb, 0, 0)),
            pl.BlockSpec((None, 1, d), lambda i, j: (i // tpb, 0, 0)),
            pl.BlockSpec((d, th), lambda i, j: (0, j)),
            pl.BlockSpec((d, th), lambda i, j: (0, n_h + j)),
            pl.BlockSpec((th, d), lambda i, j: (j, 0)),
            pl.BlockSpec((None, 1, d), lambda i, j: (i // tpb, 0, 0)),
        ],
        out_specs=pl.BlockSpec((tm, d), lambda i, j: (i, 0)),
        scratch_shapes=[pltpu.VMEM((tm, d), BF16), pltpu.VMEM((tm, d), F32)],
        compiler_params=_params(("parallel", "arbitrary")),
        name="ffn",
    )(x, gain.reshape(1, d), sc.reshape(b, 1, d), sh.reshape(b, 1, d), w_in, w_in, w_out,
      gate.reshape(b, 1, d))


def _norm_rope_kernel(*refs, scale, rope):
    if rope:
        x_ref, g_ref, c_ref, s_ref, p_ref, o_ref = refs
    else:
        x_ref, g_ref, o_ref = refs
    y = _rms(x_ref[...].astype(F32), g_ref[...])
    if rope:
        hi = y.astype(BF16)
        lo = (y - hi.astype(F32)).astype(BF16)
        rot = _dot(hi, p_ref[...]) + _dot(lo, p_ref[...])
        y = y * c_ref[...] + rot * s_ref[...]
    if scale != 1.0:
        y = y * scale
    o_ref[...] = y.astype(o_ref.dtype)


def _norm_rope(x, gain, cos_t=None, sin_t=None, perm=None, scale=1.0, out_dtype=BF16, ts=2048):
    b, h, s, d = x.shape
    ts = min(ts, s)
    rope = cos_t is not None
    in_specs = [
        pl.BlockSpec((None, None, ts, d), lambda bi, hi, si: (bi, hi, si, 0)),
        pl.BlockSpec((1, d), lambda bi, hi, si: (0, 0)),
    ]
    args = [x, gain.reshape(1, d)]
    if rope:
        in_specs += [
            pl.BlockSpec((None, ts, d), lambda bi, hi, si: (bi, si, 0)),
            pl.BlockSpec((None, ts, d), lambda bi, hi, si: (bi, si, 0)),
            pl.BlockSpec((d, d), lambda bi, hi, si: (0, 0)),
        ]
        args += [cos_t, sin_t, perm]
    return pl.pallas_call(
        functools.partial(_norm_rope_kernel, scale=scale, rope=rope),
        out_shape=jax.ShapeDtypeStruct((b, h, s, d), out_dtype),
        grid=(b, h, s // ts),
        in_specs=in_specs,
        out_specs=pl.BlockSpec((None, None, ts, d), lambda bi, hi, si: (bi, hi, si, 0)),
        compiler_params=_params(("parallel", "parallel", "parallel")),
        name="head_norm_rope" if rope else "head_norm",
    )(*args)


def _rope_tables(pos, d):
    inv = ROPE_THETA ** (-jnp.arange(0, d, 2, dtype=F32) / d)
    ang = pos.astype(F32)[..., None] * inv
    return jnp.cos(ang), jnp.sin(ang)


def _rotate_half_perm(d_total, start, half):
    p = np.zeros((d_total, d_total), np.float32)
    for i in range(half):
        p[start + half + i, start + i] = -1.0
        p[start + i, start + half + i] = 1.0
    return jnp.asarray(p, BF16)


def _flash_tile(q, ks, vs, carry, mask):
    m, l, acc = carry
    s = _dot_nt(q, ks)
    if mask is not None:
        s = jnp.where(mask, s, -jnp.inf)
    m_new = jnp.maximum(m, jnp.max(s, axis=-1, keepdims=True))
    alpha = jnp.exp(m - m_new)
    p = jnp.exp(s - m_new)
    l = alpha * l + jnp.sum(p, axis=-1, keepdims=True)
    acc = alpha * acc + _dot(p.astype(BF16), vs)
    return m_new, l, acc


def _mla_attn_kernel(q_ref, k_ref, v_ref, o_ref, *, tq, tk):
    qi = pl.program_id(2)
    q = q_ref[...]
    dv = v_ref.shape[-1]
    n_diag = tq // tk
    n_full = qi * n_diag

    def full_tile(j, carry):
        start = pl.multiple_of(j * tk, tk)
        return _flash_tile(q, k_ref[pl.ds(start, tk), :], v_ref[pl.ds(start, tk), :], carry, None)

    carry = (jnp.full((tq, 1), -jnp.inf, F32), jnp.zeros((tq, 1), F32), jnp.zeros((tq, dv), F32))
    carry = lax.fori_loop(0, n_full, full_tile, carry)
    row = lax.broadcasted_iota(jnp.int32, (tq, tk), 0)
    col = lax.broadcasted_iota(jnp.int32, (tq, tk), 1)
    for dgi in range(n_diag):
        start = pl.multiple_of((n_full + dgi) * tk, tk)
        mask = (col + dgi * tk) <= row
        carry = _flash_tile(q, k_ref[pl.ds(start, tk), :], v_ref[pl.ds(start, tk), :], carry, mask)
    _, l, acc = carry
    o_ref[...] = (acc / l).astype(o_ref.dtype)


def _mla_attention(q, k, v, tq=MLA_TQ, tk=MLA_TK):
    b, h, s, dq = q.shape
    dv = v.shape[-1]
    tq = min(tq, s)
    tk = min(tk, tq)
    return pl.pallas_call(
        functools.partial(_mla_attn_kernel, tq=tq, tk=tk),
        out_shape=jax.ShapeDtypeStruct((b, h, s, dv), F32),
        grid=(b, h, s // tq),
        in_specs=[
            pl.BlockSpec((None, None, tq, dq), lambda bi, hi, qi: (bi, hi, qi, 0)),
            pl.BlockSpec((None, None, s, dq), lambda bi, hi, qi: (bi, hi, 0, 0)),
            pl.BlockSpec((None, None, s, dv), lambda bi, hi, qi: (bi, hi, 0, 0)),
        ],
        out_specs=pl.BlockSpec((None, None, tq, dv), lambda bi, hi, qi: (bi, hi, qi, 0)),
        compiler_params=_params(("parallel", "parallel", "arbitrary")),
        name="mla_flash",
    )(q, k, v)


def _compress_kernel(x_ref, pelo_ref, pehi_ref, w1lo_ref, w1hi_ref, w2_ref, g_ref, o_ref, *, norm):
    x = x_ref[...]
    nc = x.shape[0]
    a = _dot((x + pelo_ref[...]).astype(BF16), w1lo_ref[...])
    b = _dot((x + pehi_ref[...]).astype(BF16), w1hi_ref[...])
    hid = _silu(a + pltpu.roll(b, shift=nc - 1, axis=0))
    y = _dot(hid.astype(BF16), w2_ref[...])
    if norm:
        y = _rms(y, g_ref[...])
    o_ref[...] = y.astype(o_ref.dtype)


def _compress(x, pe, w1, w2, gain, norm):
    b, g, nc, wdt = x.shape
    dh = w2.shape[1]
    hidden = w1.shape[1]
    half = NSA_CMP_STRIDE
    pelo = pe[:half].reshape(1, wdt)
    pehi = pe[half:].reshape(1, wdt)
    w1 = w1.astype(BF16)
    const = lambda bi, gi: (0, 0)
    return pl.pallas_call(
        functools.partial(_compress_kernel, norm=norm),
        out_shape=jax.ShapeDtypeStruct((b, g, nc, dh), BF16),
        grid=(b, g),
        in_specs=[
            pl.BlockSpec((None, None, nc, wdt), lambda bi, gi: (bi, gi, 0, 0)),
            pl.BlockSpec((1, wdt), const),
            pl.BlockSpec((1, wdt), const),
            pl.BlockSpec((wdt, hidden), const),
            pl.BlockSpec((wdt, hidden), const),
            pl.BlockSpec((hidden, dh), const),
            pl.BlockSpec((1, dh), const),
        ],
        out_specs=pl.BlockSpec((None, None, nc, dh), lambda bi, gi: (bi, gi, 0, 0)),
        compiler_params=_params(("parallel", "parallel")),
        name="nsa_compress",
    )(x, pelo, pehi, w1[:wdt], w1[wdt:], w2.astype(BF16), gain.reshape(1, dh))


def _nsa_cmp_kernel(q_ref, kc_ref, vc_ref, band_ref, o_ref, sb_ref, *, tq, n_rep):
    qi = pl.program_id(2)
    q0 = qi * tq
    rows = n_rep * tq
    dh = q_ref.shape[-1]
    q = q_ref[...].reshape(rows, dh)
    kc = kc_ref[...]
    nc = kc.shape[0]
    s = _dot_nt(q, kc)
    tpos = q0 + (lax.broadcasted_iota(jnp.int32, (rows, nc), 0) & (tq - 1))
    cmp_end = lax.broadcasted_iota(jnp.int32, (rows, nc), 1) * NSA_CMP_STRIDE + (NSA_CMP_LEN - 1)
    valid = (cmp_end <= tpos) & (cmp_end < nc * NSA_CMP_STRIDE)
    s = jnp.where(valid, s, -jnp.inf)
    m = jnp.max(s, axis=-1, keepdims=True)
    m = jnp.where(m > -jnp.inf, m, 0.0)
    p = jnp.exp(s - m)
    p = p / jnp.maximum(jnp.sum(p, axis=-1, keepdims=True), 1e-30)
    o_ref[...] = _dot(p.astype(BF16), vc_ref[...]).reshape(n_rep, tq, dh).astype(o_ref.dtype)

    psum = p[0:tq]
    for r in range(1, n_rep):
        psum = psum + p[r * tq:(r + 1) * tq]
    band = band_ref[...]
    hi, mid, lo = _split3(psum)
    imp = _dot_nt(band, hi) + _dot_nt(band, mid) + _dot_nt(band, lo)

    nb = band.shape[0]
    blk = lax.broadcasted_iota(jnp.int32, (nb, tq), 0)
    cur = (q0 + lax.broadcasted_iota(jnp.int32, (nb, tq), 1)) // NSA_SEL_LEN
    causal = blk <= cur
    forced = causal & ((blk == 0) | (blk > cur - NSA_N_LOCAL))
    score = jnp.where(forced, 1e30, jnp.where(causal, imp, -1.0))
    blk_f = blk.astype(F32)
    bias = jnp.full((nb, tq), SEL_MASK_BIAS, F32)
    for _ in range(NSA_N_SEL):
        best = jnp.max(score, axis=0, keepdims=True)
        idx = jnp.min(jnp.where(score == best, blk_f, float(nb)), axis=0, keepdims=True)
        pick = (blk_f == idx) & (best >= 0.0)
        bias = jnp.where(pick, 0.0, bias)
        score = jnp.where(pick, -1.0, score)
    bias_t = bias.T
    for hh in range(nb // SEL_ONEHOT):
        sb_ref[hh] = bias_t[:, hh * SEL_ONEHOT:(hh + 1) * SEL_ONEHOT].astype(sb_ref.dtype)


def _nsa_cmp_select(q, k_cmp, v_cmp, band, tq=NSA_TQ):
    b, g, r, s, dh = q.shape
    nc = k_cmp.shape[2]
    nb = band.shape[0]
    n_half = nb // SEL_ONEHOT
    return pl.pallas_call(
        functools.partial(_nsa_cmp_kernel, tq=tq, n_rep=r),
        out_shape=(jax.ShapeDtypeStruct((b, g, r, s, dh), F32),
                   jax.ShapeDtypeStruct((b, g, n_half, s, SEL_ONEHOT), BF16)),
        grid=(b, g, s // tq),
        in_specs=[
            pl.BlockSpec((None, None, r, tq, dh), lambda bi, gi, qi: (bi, gi, 0, qi, 0)),
            pl.BlockSpec((None, None, nc, dh), lambda bi, gi, qi: (bi, gi, 0, 0)),
            pl.BlockSpec((None, None, nc, dh), lambda bi, gi, qi: (bi, gi, 0, 0)),
            pl.BlockSpec((nb, nc), lambda bi, gi, qi: (0, 0)),
        ],
        out_specs=(pl.BlockSpec((None, None, r, tq, dh), lambda bi, gi, qi: (bi, gi, 0, qi, 0)),
                   pl.BlockSpec((None, None, n_half, tq, SEL_ONEHOT), lambda bi, gi, qi: (bi, gi, 0, qi, 0))),
        compiler_params=_params(("parallel", "parallel", "parallel")),
        name="nsa_cmp_select",
    )(q, k_cmp, v_cmp, band)


def _nsa_sel_kernel(q_ref, sb_ref, k_ref, v_ref, o_ref, qa_scr, *, tq, tk, n_rep, tiles_per_half):
    qi = pl.program_id(2)
    q0 = qi * tq
    rows = n_rep * tq
    dh = q_ref.shape[-1]
    n_half = sb_ref.shape[0]
    q = q_ref[...].reshape(rows, dh)
    for hh in range(n_half):
        sb = sb_ref[hh]
        qa_scr[hh] = jnp.concatenate([jnp.concatenate([sb] * n_rep, axis=0), q], axis=1)

    def kv(j):
        start = pl.multiple_of(j * tk, tk)
        return k_ref[pl.ds(start, tk), :], v_ref[pl.ds(start, tk), :]

    carry = (jnp.full((rows, 1), -jnp.inf, F32), jnp.zeros((rows, 1), F32), jnp.zeros((rows, dh), F32))
    jd = q0 // tk
    for hh in range(n_half):
        lo = hh * tiles_per_half
        hi = jnp.clip(jd, lo, lo + tiles_per_half)

        def body(j, c, hh=hh):
            ks, vs = kv(j)
            return _flash_tile(qa_scr[hh], ks, vs, c, None)

        carry = lax.fori_loop(lo, hi, body, carry)
    ks, vs = kv(jd)
    tpos = q0 + (lax.broadcasted_iota(jnp.int32, (rows, tk), 0) & (tq - 1))
    kpos = jd * tk + lax.broadcasted_iota(jnp.int32, (rows, tk), 1)
    carry = _flash_tile(qa_scr[jd // tiles_per_half], ks, vs, carry, kpos <= tpos)
    _, l, acc = carry
    o_ref[...] = (acc / l).reshape(n_rep, tq, dh).astype(o_ref.dtype)


def _nsa_sel_attention(q, sel_bias, k_aug, v, tq=NSA_TQ, tk=NSA_TK):
    b, g, r, s, dh = q.shape
    n_half = sel_bias.shape[2]
    ka = k_aug.shape[-1]
    tk = min(tk, s)
    tiles_per_half = SEL_ONEHOT * NSA_SEL_LEN // tk
    return pl.pallas_call(
        functools.partial(_nsa_sel_kernel, tq=tq, tk=tk, n_rep=r, tiles_per_half=tiles_per_half),
        out_shape=jax.ShapeDtypeStruct((b, g, r, s, dh), F32),
        grid=(b, g, s // tq),
        in_specs=[
            pl.BlockSpec((None, None, r, tq, dh), lambda bi, gi, qi: (bi, gi, 0, qi, 0)),
            pl.BlockSpec((None, None, n_half, tq, SEL_ONEHOT), lambda bi, gi, qi: (bi, gi, 0, qi, 0)),
            pl.BlockSpec((None, None, s, ka), lambda bi, gi, qi: (bi, gi, 0, 0)),
            pl.BlockSpec((None, None, s, dh), lambda bi, gi, qi: (bi, gi, 0, 0)),
        ],
        out_specs=pl.BlockSpec((None, None, r, tq, dh), lambda bi, gi, qi: (bi, gi, 0, qi, 0)),
        scratch_shapes=[pltpu.VMEM((n_half, r * tq, ka), BF16)],
        compiler_params=_params(("parallel", "parallel", "arbitrary")),
        name="nsa_sel_flash",
    )(q, sel_bias, k_aug, v)


def _nsa_win_kernel(q_ref, k_ref, v_ref, o_ref, *, tq, win, n_rep):
    qi = pl.program_id(2)
    q0 = pl.multiple_of(qi * tq, tq)
    rows = n_rep * tq
    dh = q_ref.shape[-1]
    span = tq + win
    q = q_ref[...].reshape(rows, dh)
    s = _dot_nt(q, k_ref[pl.ds(q0, span), :])
    tpos = q0 + (lax.broadcasted_iota(jnp.int32, (rows, span), 0) & (tq - 1))
    kpos = q0 - win + lax.broadcasted_iota(jnp.int32, (rows, span), 1)
    mask = (kpos <= tpos) & (kpos > tpos - win) & (kpos >= 0)
    s = jnp.where(mask, s, -jnp.inf)
    m = jnp.max(s, axis=-1, keepdims=True)
    p = jnp.exp(s - m)
    l = jnp.sum(p, axis=-1, keepdims=True)
    o = _dot(p.astype(BF16), v_ref[pl.ds(q0, span), :]) / l
    o_ref[...] = o.reshape(n_rep, tq, dh).astype(o_ref.dtype)


def _nsa_win_attention(q, k_pad, v_pad, tq=NSA_TQ):
    b, g, r, s, dh = q.shape
    sp = k_pad.shape[2]
    return pl.pallas_call(
        functools.partial(_nsa_win_kernel, tq=tq, win=NSA_WINDOW, n_rep=r),
        out_shape=jax.ShapeDtypeStruct((b, g, r, s, dh), F32),
        grid=(b, g, s // tq),
        in_specs=[
            pl.BlockSpec((None, None, r, tq, dh), lambda bi, gi, qi: (bi, gi, 0, qi, 0)),
            pl.BlockSpec((None, None, sp, dh), lambda bi, gi, qi: (bi, gi, 0, 0)),
            pl.BlockSpec((None, None, sp, dh), lambda bi, gi, qi: (bi, gi, 0, 0)),
        ],
        out_specs=pl.BlockSpec((None, None, r, tq, dh), lambda bi, gi, qi: (bi, gi, 0, qi, 0)),
        compiler_params=_params(("parallel", "parallel", "arbitrary")),
        name="nsa_window",
    )(q, k_pad, v_pad)


def _nsa_out_kernel(oc_ref, os_ref, ow_ref, gl_ref, e_ref, w_ref, x_ref, gate_ref, o_ref):
    sg = _sigmoid(gl_ref[...])
    hi = sg.astype(BF16)
    lo = (sg - hi.astype(F32)).astype(BF16)

    def expand(i):
        return _dot(hi, e_ref[i]) + _dot(lo, e_ref[i])

    o = expand(0) * oc_ref[...] + expand(1) * os_ref[...] + expand(2) * ow_ref[...]
    o_ref[...] = x_ref[...] + gate_ref[...] * _dot(o.astype(BF16), w_ref[...])


def _nsa_out(o_cmp, o_sel, o_win, gate_logits, expand, w, x, gate, seq, tm=512):
    t, d = x.shape
    k = o_cmp.shape[1]
    gw = gate_logits.shape[1]
    b = gate.shape[0]
    tpb = seq // tm
    row = lambda i: (i, 0)
    return pl.pallas_call(
        _nsa_out_kernel,
        out_shape=jax.ShapeDtypeStruct((t, d), F32),
        grid=(t // tm,),
        in_specs=[
            pl.BlockSpec((tm, k), row),
            pl.BlockSpec((tm, k), row),
            pl.BlockSpec((tm, k), row),
            pl.BlockSpec((tm, gw), row),
            pl.BlockSpec((3, gw, k), lambda i: (0, 0, 0)),
            pl.BlockSpec((k, d), lambda i: (0, 0)),
            pl.BlockSpec((tm, d), row),
            pl.BlockSpec((None, 1, d), lambda i: (i // tpb, 0, 0)),
        ],
        out_specs=pl.BlockSpec((tm, d), row),
        compiler_params=_params(("parallel",)),
        name="nsa_out",
    )(o_cmp, o_sel, o_win, gate_logits, expand, w.astype(BF16), x, gate.reshape(b, 1, d))


def _ret_kernel(lg_ref, q_ref, k_ref, v_ref, g_ref, cos_ref, sin_ref, gn_ref, o_ref, st_scr):
    h = pl.program_id(1)
    c = pl.program_id(2)

    @pl.when(c == 0)
    def _():
        st_scr[...] = jnp.zeros_like(st_scr)

    cn = q_ref.shape[0]
    dk = q_ref.shape[1]
    half = dk // 2
    lg = jnp.full((1, 1), lg_ref[h], F32)
    cos = cos_ref[...]
    sin = sin_ref[...]

    def rope(x):
        x1, x2 = x[:, :half], x[:, half:]
        return jnp.concatenate([x1 * cos - x2 * sin, x1 * sin + x2 * cos], axis=1)

    q = rope(q_ref[...])
    k = rope(k_ref[...]) * (dk ** -0.5)
    v = v_ref[...].astype(BF16)
    n = lax.broadcasted_iota(jnp.int32, (cn, 1), 0).astype(F32)
    rel = (lax.broadcasted_iota(jnp.int32, (cn, cn), 0) - lax.broadcasted_iota(jnp.int32, (cn, cn), 1)).astype(F32)
    decay = jnp.where(rel >= 0, jnp.exp(jnp.maximum(rel, 0.0) * lg), 0.0)
    xi = jnp.exp((n + 1.0) * lg)
    zeta = jnp.exp((cn - 1.0 - n) * lg)
    gamma_c = jnp.exp(cn * lg)

    qb = q.astype(BF16)
    inner = _dot((_dot_nt(qb, k.astype(BF16)) * decay).astype(BF16), v)
    st = st_scr[...]
    cross = _dot(qb, st.astype(BF16)) * xi
    kz_t = (k * zeta).T.astype(BF16)
    st_scr[...] = gamma_c * st + _dot(kz_t, v)
    y = inner + cross
    mu = jnp.mean(y, axis=-1, keepdims=True)
    yc = y - mu
    var = jnp.mean(yc * yc, axis=-1, keepdims=True)
    yn = yc * lax.rsqrt(var + NORM_EPS) * gn_ref[...]
    o_ref[...] = (_silu(g_ref[...]) * yn).astype(o_ref.dtype)


def _retention(proj, cos_t, sin_t, log_gamma, gn_w, batch, seq, heads):
    t = proj.shape[0]
    dk, dv, cn = RET_QK, RET_V, RET_CHUNK
    n_ch = seq // cn
    kb = heads * dk // dk
    vb = 2 * heads * dk // dv
    row = lambda bi, hi, ci, *_: bi * n_ch + ci
    grid_spec = pltpu.PrefetchScalarGridSpec(
        num_scalar_prefetch=1,
        grid=(batch, heads, n_ch),
        in_specs=[
            pl.BlockSpec((cn, dk), lambda bi, hi, ci, lg: (row(bi, hi, ci), hi)),
            pl.BlockSpec((cn, dk), lambda bi, hi, ci, lg: (row(bi, hi, ci), kb + hi)),
            pl.BlockSpec((cn, dv), lambda bi, hi, ci, lg: (row(bi, hi, ci), vb + hi)),
            pl.BlockSpec((cn, dv), lambda bi, hi, ci, lg: (row(bi, hi, ci), vb + heads + hi)),
            pl.BlockSpec((cn, dk // 2), lambda bi, hi, ci, lg: (row(bi, hi, ci), 0)),
            pl.BlockSpec((cn, dk // 2), lambda bi, hi, ci, lg: (row(bi, hi, ci), 0)),
            pl.BlockSpec((1, dv), lambda bi, hi, ci, lg: (0, hi)),
        ],
        out_specs=pl.BlockSpec((cn, dv), lambda bi, hi, ci, lg: (row(bi, hi, ci), hi)),
        scratch_shapes=[pltpu.VMEM((dk, dv), F32)],
    )
    return pl.pallas_call(
        _ret_kernel,
        out_shape=jax.ShapeDtypeStruct((t, heads * dv), BF16),
        grid_spec=grid_spec,
        compiler_params=_params(("parallel", "parallel", "arbitrary")),
        name="retention",
    )(log_gamma, proj, proj, proj, proj, cos_t, sin_t, gn_w.reshape(1, heads * dv))


def _softplus(x):
    return jnp.maximum(x, 0.0) + jnp.log1p(jnp.exp(-jnp.abs(x)))


def _ssd_kernel(row_ref, dtt_ref, cw_ref, cb_ref, dtb_ref, dtbt_ref, al_ref, alt_ref, dsk_ref, nw_ref,
                ex_ref, o_ref, buf_scr, st_scr, *, inner, groups, state, heads):
    c = pl.program_id(1)
    cn = row_ref.shape[0]
    conv_ch = inner + 2 * groups * state
    hp = inner // heads
    rep = heads // groups
    gw = rep * hp
    halo = 8

    @pl.when(c == 0)
    def _():
        buf_scr[0:halo, :] = jnp.zeros((halo, conv_ch), F32)
        st_scr[...] = jnp.zeros_like(st_scr)

    z = row_ref[:, 0:inner]
    xbc = row_ref[:, inner:inner + conv_ch]
    dt_raw = row_ref[:, inner + conv_ch:inner + conv_ch + LANES]

    buf_scr[halo:halo + cn, :] = xbc
    conv = cb_ref[...] + cw_ref[0:1, :] * buf_scr[halo - 3:halo - 3 + cn, :]
    for kk in range(1, SSD_CONV):
        conv = conv + cw_ref[kk:kk + 1, :] * buf_scr[halo - 3 + kk:halo - 3 + kk + cn, :]
    buf_scr[0:halo, :] = xbc[cn - halo:cn, :]
    act = _silu(conv)
    xs = act[:, 0:inner]
    bm = act[:, inner:inner + groups * state]
    cm = act[:, inner + groups * state:conv_ch]

    dt = _softplus(dt_raw + dtb_ref[...])
    dt_t = _softplus(dtt_ref[...] + dtbt_ref[...])
    a = -jnp.exp(al_ref[...])
    a_t = -jnp.exp(alt_ref[...])
    ri = lax.broadcasted_iota(jnp.int32, (cn, cn), 0)
    ci = lax.broadcasted_iota(jnp.int32, (cn, cn), 1)
    tril = ci <= ri
    lower = jnp.where(tril, 1.0, 0.0).astype(BF16)
    upper = jnp.where(ci >= ri, 1.0, 0.0).astype(BF16)
    cum = _dot3_left(lower, dt * a)
    cum_t = _dot3(dt_t * a_t, upper)
    cum_last = cum[cn - 1:cn, :]

    ex = ex_ref[...]
    e_cum = _dot3(jnp.exp(cum), ex)
    w_end = _dot3(jnp.exp(cum_last - cum) * dt, ex)
    e_last = _dot3(jnp.exp(cum_last), ex)

    lane = lax.broadcasted_iota(jnp.int32, (cn, 2 * hp), 1)
    ys = []
    for g in range(groups):
        cc = cm[:, g * state:(g + 1) * state].astype(BF16)
        bc = bm[:, g * state:(g + 1) * state].astype(BF16)
        cb = _dot_nt(cc, bc)
        xg = xs[:, g * gw:(g + 1) * gw]
        st = st_scr[g]
        cross = _dot(cc, st.astype(BF16)) * e_cum[:, g * gw:(g + 1) * gw]
        xw_t = bc.astype(F32).T.astype(BF16)
        st_scr[g] = e_last[:, g * gw:(g + 1) * gw] * st + _dot(
            xw_t, (xg * w_end[:, g * gw:(g + 1) * gw]).astype(BF16))
        intra = []
        for pr in range(rep // 2):
            ws = []
            for hh in (2 * pr, 2 * pr + 1):
                hd = g * rep + hh
                seg = jnp.where(tril, cum[:, hd:hd + 1] - cum_t[hd:hd + 1, :], -jnp.inf)
                ws.append((jnp.exp(seg) * cb * dt_t[hd:hd + 1, :]).astype(BF16))
            w2 = jnp.concatenate(ws, axis=1)
            xp = xg[:, 2 * pr * hp:(2 * pr + 2) * hp]
            x2 = jnp.concatenate([jnp.where(lane < hp, xp, 0.0), jnp.where(lane >= hp, xp, 0.0)], axis=0)
            intra.append(_dot(w2, x2.astype(BF16)))
        ys.append(jnp.concatenate(intra, axis=1) + cross)
    y = jnp.concatenate(ys, axis=1) + dsk_ref[...] * xs
    y = y * _silu(z)
    outs = []
    for g in range(groups):
        outs.append(_rms(y[:, g * gw:(g + 1) * gw], nw_ref[:, g * gw:(g + 1) * gw]))
    o_ref[...] = jnp.concatenate(outs, axis=1).astype(o_ref.dtype)


def _ssd(proj, dt_t, conv_w, conv_b, dt_bias, a_log, d_skip, norm_w, batch, seq, inner, heads):
    t, n_pad = proj.shape
    groups, state, cn = SSD_GROUPS, SSD_STATE, SSD_CHUNK
    conv_ch = inner + 2 * groups * state
    n_ch = seq // cn
    hp = inner // heads
    gw = inner // groups
    hpad = LANES
    assert heads <= hpad and n_pad >= inner + conv_ch + hpad
    ex_np = np.zeros((hpad, inner), np.float32)
    ex_np[:heads] = np.kron(np.eye(heads, dtype=np.float32), np.ones((1, hp), np.float32))
    ex = jnp.asarray(ex_np, BF16)
    dt_bias = jnp.pad(dt_bias, (0, hpad - heads))
    a_log = jnp.pad(a_log, (0, hpad - heads))
    heads_k = heads
    heads = hpad
    const = lambda bi, ci: (0, 0)
    return pl.pallas_call(
        functools.partial(_ssd_kernel, inner=inner, groups=groups, state=state, heads=heads_k),
        out_shape=jax.ShapeDtypeStruct((t, inner), BF16),
        grid=(batch, n_ch),
        in_specs=[
            pl.BlockSpec((cn, n_pad), lambda bi, ci: (bi * n_ch + ci, 0)),
            pl.BlockSpec((None, heads, cn), lambda bi, ci: (bi, 0, ci)),
            pl.BlockSpec((SSD_CONV, conv_ch), const),
            pl.BlockSpec((1, conv_ch), const),
            pl.BlockSpec((1, heads), const),
            pl.BlockSpec((heads, 1), const),
            pl.BlockSpec((1, heads), const),
            pl.BlockSpec((heads, 1), const),
            pl.BlockSpec((1, inner), const),
            pl.BlockSpec((1, inner), const),
            pl.BlockSpec((heads, inner), const),
        ],
        out_specs=pl.BlockSpec((cn, inner), lambda bi, ci: (bi * n_ch + ci, 0)),
        scratch_shapes=[pltpu.VMEM((8 + cn, conv_ch), F32), pltpu.VMEM((groups, state, gw), F32)],
        compiler_params=_params(("parallel", "arbitrary")),
        name="ssd_scan",
    )(proj, dt_t, conv_w, conv_b.reshape(1, conv_ch), dt_bias.reshape(1, heads), dt_bias.reshape(heads, 1),
      a_log.reshape(1, heads), a_log.reshape(heads, 1), jnp.repeat(d_skip, hp).reshape(1, inner),
      norm_w.reshape(1, inner), ex)


def _nsa_layer(x, mods, norm_g, batch, seq, w_in, q_norm, k_norm, cmp_pe, cmp_w1, cmp_w2, w_out):
    sh, sc, gate = mods
    d = x.shape[1]
    dh, g = HEAD_DIM, NSA_GROUPS
    heads = d // dh
    r = heads // g
    kvw = g * dh
    proj = _modnorm_matmul(x, norm_g, sc, sh, w_in, seq)
    off = heads * dh

    def heads_major(cols, n_heads):
        return cols.reshape(batch, seq, n_heads, dh).transpose(0, 2, 1, 3)

    q = heads_major(proj[:, :off], heads)
    parts = [proj[:, off + i * kvw: off + (i + 1) * kvw] for i in range(6)]
    k_c, v_c, k_s, v_s, k_w, v_w = parts
    gate_logits = proj[:, off + 6 * kvw: off + 6 * kvw + LANES]

    qn = _norm_rope(q, q_norm, scale=dh ** -0.5).reshape(batch, g, r, seq, dh)

    def chunk_rows(cols):
        st = NSA_CMP_STRIDE
        return cols.reshape(batch, seq // st, st, g, dh).transpose(0, 3, 1, 2, 4).reshape(batch, g, seq // st, st * dh)

    k_cmp = _compress(chunk_rows(k_c), cmp_pe[0], cmp_w1[0], cmp_w2[0], k_norm[0], norm=True)
    v_cmp = _compress(chunk_rows(v_c), cmp_pe[1], cmp_w1[1], cmp_w2[1], k_norm[0], norm=False)

    n_blk = seq // NSA_SEL_LEN
    n_cmp = seq // NSA_CMP_STRIDE
    assert n_blk % SEL_ONEHOT == 0
    ratio = NSA_SEL_LEN // NSA_CMP_STRIDE
    jj = np.arange(n_blk)[:, None]
    cc = np.arange(n_cmp)[None, :]
    band = jnp.asarray(((cc >= ratio * jj - 1) & (cc <= ratio * jj + ratio - 1)).astype(np.float32), BF16)
    o_cmp, sel_bias = _nsa_cmp_select(qn, k_cmp, v_cmp, band)

    ks_n = _norm_rope(heads_major(k_s, g), k_norm[1])
    onehot = jnp.asarray(np.eye(SEL_ONEHOT, dtype=np.float32)[(np.arange(seq) // NSA_SEL_LEN) % SEL_ONEHOT], BF16)
    k_aug = jnp.concatenate([jnp.broadcast_to(onehot, (batch, g, seq, SEL_ONEHOT)), ks_n], axis=-1)
    o_sel = _nsa_sel_attention(qn, sel_bias, k_aug, heads_major(v_s, g).astype(BF16))

    kw_n = _norm_rope(heads_major(k_w, g), k_norm[2])
    pad = ((0, 0), (0, 0), (NSA_WINDOW, 0), (0, 0))
    o_win = _nsa_win_attention(qn, jnp.pad(kw_n, pad), jnp.pad(heads_major(v_w, g).astype(BF16), pad))

    def token_major(o):
        return o.transpose(0, 3, 1, 2, 4).reshape(batch * seq, heads * dh)

    e = np.zeros((3, LANES, heads * dh), np.float32)
    for hd in range(heads):
        for i in range(3):
            e[i, hd * 3 + i, hd * dh:(hd + 1) * dh] = 1.0
    return _nsa_out(token_major(o_cmp), token_major(o_sel), token_major(o_win), gate_logits,
                    jnp.asarray(e, BF16), w_out, x, gate, seq)


def _mla_layer(x, mods, norm_g, batch, seq, pos, w_in, q_a_norm, kv_a_norm, w_q_b, w_kv_b, q_norm, k_norm, w_out):
    sh, sc, gate = mods
    d = x.shape[1]
    heads = w_out.shape[0] // MLA_V
    dq = MLA_NOPE + MLA_ROPE
    proj = _modnorm_matmul(x, norm_g, sc, sh, w_in, seq)
    cq = proj[:, :MLA_Q_LORA]
    ckv = proj[:, MLA_Q_LORA:MLA_Q_LORA + MLA_KV_LORA]
    k_rot = proj[:, MLA_Q_LORA + MLA_KV_LORA:MLA_Q_LORA + MLA_KV_LORA + MLA_ROPE]
    q_raw = _norm_matmul(cq, q_a_norm, w_q_b)
    kv_raw = _norm_matmul(ckv, kv_a_norm, w_kv_b)
    q_raw = q_raw.reshape(batch, seq, heads, dq).transpose(0, 2, 1, 3)
    kv_raw = kv_raw.reshape(batch, seq, heads, MLA_NOPE + MLA_V).transpose(0, 2, 1, 3)

    cos, sin = _rope_tables(pos, MLA_ROPE)
    ones = jnp.ones((batch, seq, MLA_NOPE), F32)
    cos_q = jnp.concatenate([ones, cos, cos], axis=-1)
    sin_q = jnp.concatenate([0.0 * ones, sin, sin], axis=-1)
    qn = _norm_rope(q_raw, q_norm, cos_q, sin_q, _rotate_half_perm(dq, MLA_NOPE, MLA_ROPE // 2), scale=dq ** -0.5)
    k_nope = _norm_rope(kv_raw[..., :MLA_NOPE], k_norm[:MLA_NOPE])
    k_rope = _norm_rope(k_rot.reshape(batch, 1, seq, MLA_ROPE), k_norm[MLA_NOPE:],
                        jnp.concatenate([cos, cos], axis=-1), jnp.concatenate([sin, sin], axis=-1),
                        _rotate_half_perm(MLA_ROPE, 0, MLA_ROPE // 2))
    k = jnp.concatenate([k_nope, jnp.broadcast_to(k_rope, (batch, heads, seq, MLA_ROPE))], axis=-1)
    v = kv_raw[..., MLA_NOPE:].astype(BF16)
    o = _mla_attention(qn, k, v)
    o = o.transpose(0, 2, 1, 3).reshape(batch * seq, heads * MLA_V).astype(BF16)
    return _out_proj(o, w_out, x, gate, seq)


def _ret_layer(x, mods, norm_g, batch, seq, pos, w_in, gn_w, w_out):
    sh, sc, gate = mods
    heads = w_out.shape[0] // RET_V
    proj = _modnorm_matmul(x, norm_g, sc, sh, w_in, seq)
    cos, sin = _rope_tables(pos, RET_QK)
    cos = cos.reshape(batch * seq, RET_QK // 2)
    sin = sin.reshape(batch * seq, RET_QK // 2)
    log_gamma = jnp.log1p(-(2.0 ** (-5.0 - jnp.arange(heads, dtype=F32))))
    y = _retention(proj, cos, sin, log_gamma, gn_w, batch, seq, heads)
    return _out_proj(y, w_out, x, gate, seq)


def _ssd_layer(x, mods, norm_g, batch, seq, w_in, conv_w, conv_b, dt_bias, a_log, d_skip, norm_w, w_out):
    sh, sc, gate = mods
    inner = w_out.shape[0]
    heads = dt_bias.shape[0]
    conv_ch = conv_w.shape[1]
    proj = _modnorm_matmul(x, norm_g, sc, sh, w_in, seq)
    dt_t = proj[:, inner + conv_ch:inner + conv_ch + LANES].reshape(batch, seq, LANES).transpose(0, 2, 1)
    y = _ssd(proj, dt_t, conv_w, conv_b, dt_bias, a_log, d_skip, norm_w, batch, seq, inner, heads)
    return _out_proj(y, w_out, x, gate, seq)


def kernel(x, c, positions, ada_w, ada_b, norm_mix, norm_ffn, ffn_w_in, ffn_w_out, nsa_w_in, nsa_q_norm, nsa_k_norm, nsa_cmp_pe, nsa_cmp_w1, nsa_cmp_w2, nsa_w_out, mla_w_in, mla_q_a_norm, mla_kv_a_norm, mla_w_q_b, mla_w_kv_b, mla_q_norm, mla_k_norm, mla_w_out, ret_w_in, ret_gn_w, ret_w_out, ssd_w_in, ssd_conv_w, ssd_conv_b, ssd_dt_bias, ssd_a_log, ssd_d, ssd_norm, ssd_w_out):
    batch, seq, d = x.shape
    depth = ada_w.shape[0]
    n_mixers = 4
    mod = _modulation(c, ada_w, ada_b)
    xt = x.reshape(batch * seq, d)
    for i in range(depth):
        sh_m, sc_m, g_m, sh_f, sc_f, g_f = [mod[i, :, k * d:(k + 1) * d] for k in range(6)]
        mods = (sh_m, sc_m, g_m)
        kind, j = i % n_mixers, i // n_mixers
        if kind == 0:
            xt = _nsa_layer(xt, mods, norm_mix[i], batch, seq, nsa_w_in[j], nsa_q_norm[j], nsa_k_norm[j],
                            nsa_cmp_pe[j], nsa_cmp_w1[j], nsa_cmp_w2[j], nsa_w_out[j])
        elif kind == 1:
            xt = _mla_layer(xt, mods, norm_mix[i], batch, seq, positions, mla_w_in[j], mla_q_a_norm[j],
                            mla_kv_a_norm[j], mla_w_q_b[j], mla_w_kv_b[j], mla_q_norm[j], mla_k_norm[j], mla_w_out[j])
        elif kind == 2:
            xt = _ret_layer(xt, mods, norm_mix[i], batch, seq, positions, ret_w_in[j], ret_gn_w[j], ret_w_out[j])
        else:
            xt = _ssd_layer(xt, mods, norm_mix[i], batch, seq, ssd_w_in[j], ssd_conv_w[j], ssd_conv_b[j],
                            ssd_dt_bias[j], ssd_a_log[j], ssd_d[j], ssd_norm[j], ssd_w_out[j])
        xt = _ffn(xt, norm_ffn[i], sc_f, sh_f, g_f, ffn_w_in[i], ffn_w_out[i], seq)
    return xt.reshape(batch, seq, d)
```

```python
import functools
import math

import numpy as np
import jax
import jax.numpy as jnp
from jax import lax
from jax.experimental import pallas as pl
from jax.experimental.pallas import tpu as pltpu

F32 = jnp.float32
BF16 = jnp.bfloat16

NORM_EPS = 1e-6
ROPE_THETA = 10000.0
LANES = 128

HEAD_DIM = 64
NSA_GROUPS = 4
NSA_CMP_STRIDE = 16
NSA_CMP_LEN = 32
NSA_SEL_LEN = 64
NSA_N_SEL = 16
NSA_N_LOCAL = 2
NSA_WINDOW = 512
NSA_TQ = 128
NSA_TK = 512
SEL_ONEHOT = 128
SEL_MASK_BIAS = -32768.0

MLA_Q_LORA = 384
MLA_KV_LORA = 256
MLA_NOPE = 64
MLA_ROPE = 32
MLA_V = 64
MLA_TQ = 512
MLA_TK = 512

RET_QK = 256
RET_V = 512
RET_CHUNK = 128

SSD_HEADDIM = 64
SSD_GROUPS = 4
SSD_STATE = 128
SSD_CONV = 4
SSD_CHUNK = 128

VMEM_LIMIT = 48 * 1024 * 1024


def _params(sem, vmem=VMEM_LIMIT, flags=None):
    return pltpu.CompilerParams(dimension_semantics=sem, vmem_limit_bytes=vmem, flags=flags)


def _sigmoid(x):
    return 1.0 / (1.0 + jnp.exp(-x))


def _silu(x):
    return x * _sigmoid(x)


def _dot(a, b):
    return jnp.dot(a, b, preferred_element_type=F32)


def _dot_nt(a, b):
    return lax.dot_general(a, b, (((1,), (1,)), ((), ())), preferred_element_type=F32)


def _split3(x):
    hi = x.astype(BF16)
    r1 = x - hi.astype(F32)
    mid = r1.astype(BF16)
    lo = (r1 - mid.astype(F32)).astype(BF16)
    return hi, mid, lo


def _dot3(x, m):
    hi, mid, lo = _split3(x)
    return _dot(hi, m) + _dot(mid, m) + _dot(lo, m)


def _dot3_left(m, x):
    hi, mid, lo = _split3(x)
    return _dot(m, hi) + _dot(m, mid) + _dot(m, lo)


def _rms(x, gain):
    ms = jnp.mean(x * x, axis=-1, keepdims=True)
    return x * lax.rsqrt(ms + NORM_EPS) * gain


def _mod_kernel(c_ref, w_ref, b_ref, o_ref):
    cond = _silu(c_ref[...]).astype(BF16)
    o_ref[...] = _dot(cond, w_ref[...]) + b_ref[...]


def _modulation(c, ada_w, ada_b):
    depth, d, n = ada_w.shape
    b = c.shape[0]
    rows = 16
    c_pad = jnp.zeros((rows, d), F32).at[:b].set(c)
    tn = 1024
    out = pl.pallas_call(
        _mod_kernel,
        out_shape=jax.ShapeDtypeStruct((depth, rows, n), F32),
        grid=(depth, n // tn),
        in_specs=[
            pl.BlockSpec((rows, d), lambda l, j: (0, 0)),
            pl.BlockSpec((None, d, tn), lambda l, j: (l, 0, j)),
            pl.BlockSpec((None, 1, tn), lambda l, j: (l, 0, j)),
        ],
        out_specs=pl.BlockSpec((None, rows, tn), lambda l, j: (l, 0, j)),
        compiler_params=_params(("parallel", "parallel")),
        name="adaln_mod",
    )(c_pad, ada_w.astype(BF16), ada_b.reshape(depth, 1, n))
    return out[:, :b]


def _modnorm_matmul_kernel(x_ref, g_ref, sc_ref, sh_ref, w_ref, o_ref, h_scr):
    @pl.when(pl.program_id(1) == 0)
    def _():
        y = _rms(x_ref[...], g_ref[...])
        h_scr[...] = (y * (1.0 + sc_ref[...]) + sh_ref[...]).astype(BF16)

    o_ref[...] = _dot(h_scr[...], w_ref[...]).astype(o_ref.dtype)


def _pad_and_tile(n, max_tile=1024, min_tile=512):
    n_pad = -(-n // LANES) * LANES
    while True:
        if n_pad <= max_tile:
            return n_pad, n_pad
        for tn in range(max_tile, min_tile - 1, -LANES):
            if n_pad % tn == 0:
                return n_pad, tn
        n_pad += LANES


def _modnorm_matmul(x, gain, sc, sh, w, seq, out_dtype=F32, tm=512):
    t, d = x.shape
    n = w.shape[1]
    n_pad, tn = _pad_and_tile(n)
    w = w.astype(BF16)
    if n_pad != n:
        w = jnp.pad(w, ((0, 0), (0, n_pad - n)))
    tpb = seq // tm
    b = sc.shape[0]
    return pl.pallas_call(
        _modnorm_matmul_kernel,
        out_shape=jax.ShapeDtypeStruct((t, n_pad), out_dtype),
        grid=(t // tm, n_pad // tn),
        in_specs=[
            pl.BlockSpec((tm, d), lambda i, j: (i, 0)),
            pl.BlockSpec((1, d), lambda i, j: (0, 0)),
            pl.BlockSpec((None, 1, d), lambda i, j: (i // tpb, 0, 0)),
            pl.BlockSpec((None, 1, d), lambda i, j: (i // tpb, 0, 0)),
            pl.BlockSpec((d, tn), lambda i, j: (0, j)),
        ],
        out_specs=pl.BlockSpec((tm, tn), lambda i, j: (i, j)),
        scratch_shapes=[pltpu.VMEM((tm, d), BF16)],
        compiler_params=_params(("parallel", "arbitrary")),
        name="modnorm_matmul",
    )(x, gain.reshape(1, d), sc.reshape(b, 1, d), sh.reshape(b, 1, d), w)


def _norm_matmul(x, gain, w, out_dtype=F32):
    t, d = x.shape
    zeros = jnp.zeros((1, d), F32)
    return _modnorm_matmul(x, gain, zeros, zeros, w, seq=t, out_dtype=out_dtype)


def _out_proj_kernel(y_ref, w_ref, x_ref, gate_ref, o_ref):
    o_ref[...] = x_ref[...] + gate_ref[...] * _dot(y_ref[...], w_ref[...])


def _out_proj(y, w, x, gate, seq, tm=512):
    t, k = y.shape
    d = w.shape[1]
    b = gate.shape[0]
    tpb = seq // tm
    return pl.pallas_call(
        _out_proj_kernel,
        out_shape=jax.ShapeDtypeStruct((t, d), F32),
        grid=(t // tm,),
        in_specs=[
            pl.BlockSpec((tm, k), lambda i: (i, 0)),
            pl.BlockSpec((k, d), lambda i: (0, 0)),
            pl.BlockSpec((tm, d), lambda i: (i, 0)),
            pl.BlockSpec((None, 1, d), lambda i: (i // tpb, 0, 0)),
        ],
        out_specs=pl.BlockSpec((tm, d), lambda i: (i, 0)),
        compiler_params=_params(("parallel",)),
        name="out_proj",
    )(y, w.astype(BF16), x, gate.reshape(b, 1, d))


def _ffn_kernel(x_ref, g_ref, sc_ref, sh_ref, wa_ref, wb_ref, wo_ref, gate_ref, o_ref, h_scr, acc_scr):
    j = pl.program_id(1)

    @pl.when(j == 0)
    def _():
        y = _rms(x_ref[...], g_ref[...])
        h_scr[...] = (y * (1.0 + sc_ref[...]) + sh_ref[...]).astype(BF16)
        acc_scr[...] = jnp.zeros_like(acc_scr)

    h = h_scr[...]
    a = _dot(h, wa_ref[...])
    b = _dot(h, wb_ref[...])
    u = (_silu(a) * b).astype(BF16)
    acc_scr[...] += _dot(u, wo_ref[...])

    @pl.when(j == pl.num_programs(1) - 1)
    def _():
        o_ref[...] = x_ref[...] + gate_ref[...] * acc_scr[...]


def _ffn(x, gain, sc, sh, gate, w_in, w_out, seq, tm=512):
    t, d = x.shape
    hid = w_out.shape[0]
    n_h = 2
    th = hid // n_h
    assert th % LANES == 0
    b = sc.shape[0]
    tpb = seq // tm
    w_in = w_in.astype(BF16)
    w_out = w_out.astype(BF16)
    return pl.pallas_call(
        _ffn_kernel,
        out_shape=jax.ShapeDtypeStruct((t, d), F32),
        grid=(t // tm, n_h),
        in_specs=[
            pl.BlockSpec((tm, d), lambda i, j: (i, 0)),
            pl.BlockSpec((1, d), lambda i, j: (0, 0)),
            pl.BlockSpec((None, 1, d), lambda i, j: (i // tpb, 0, 0)),
            pl.BlockSpec((None, 1, d), lambda i, j: (i // tpb, 0, 0)),
            pl.BlockSpec((d, th), lambda i, j: (0, j)),
            pl.BlockSpec((d, th), lambda i, j: (0, n_h + j)),
            pl.BlockSpec((th, d), lambda i, j: (j, 0)),
            pl.BlockSpec((None, 1, d), lambda i, j: (i // tpb, 0, 0)),
        ],
        out_specs=pl.BlockSpec((tm, d), lambda i, j: (i, 0)),
        scratch_shapes=[pltpu.VMEM((tm, d), BF16), pltpu.VMEM((tm, d), F32)],
        compiler_params=_params(("parallel", "arbitrary")),
        name="ffn",
    )(x, gain.reshape(1, d), sc.reshape(b, 1, d), sh.reshape(b, 1, d), w_in, w_in, w_out,
      gate.reshape(b, 1, d))


def _norm_rope_kernel(*refs, scale, rope):
    if rope:
        x_ref, g_ref, c_ref, s_ref, p_ref, o_ref = refs
    else:
        x_ref, g_ref, o_ref = refs
    y = _rms(x_ref[...].astype(F32), g_ref[...])
    if rope:
        hi = y.astype(BF16)
        lo = (y - hi.astype(F32)).astype(BF16)
        rot = _dot(hi, p_ref[...]) + _dot(lo, p_ref[...])
        y = y * c_ref[...] + rot * s_ref[...]
    if scale != 1.0:
        y = y * scale
    o_ref[...] = y.astype(o_ref.dtype)


def _norm_rope(x, gain, cos_t=None, sin_t=None, perm=None, scale=1.0, out_dtype=BF16, ts=2048):
    b, h, s, d = x.shape
    ts = min(ts, s)
    rope = cos_t is not None
    in_specs = [
        pl.BlockSpec((None, None, ts, d), lambda bi, hi, si: (bi, hi, si, 0)),
        pl.BlockSpec((1, d), lambda bi, hi, si: (0, 0)),
    ]
    args = [x, gain.reshape(1, d)]
    if rope:
        in_specs += [
            pl.BlockSpec((None, ts, d), lambda bi, hi, si: (bi, si, 0)),
            pl.BlockSpec((None, ts, d), lambda bi, hi, si: (bi, si, 0)),
            pl.BlockSpec((d, d), lambda bi, hi, si: (0, 0)),
        ]
        args += [cos_t, sin_t, perm]
    return pl.pallas_call(
        functools.partial(_norm_rope_kernel, scale=scale, rope=rope),
        out_shape=jax.ShapeDtypeStruct((b, h, s, d), out_dtype),
        grid=(b, h, s // ts),
        in_specs=in_specs,
        out_specs=pl.BlockSpec((None, None, ts, d), lambda bi, hi, si: (bi, hi, si, 0)),
        compiler_params=_params(("parallel", "parallel", "parallel")),
        name="head_norm_rope" if rope else "head_norm",
    )(*args)


def _rope_tables(pos, d):
    inv = ROPE_THETA ** (-jnp.arange(0, d, 2, dtype=F32) / d)
    ang = pos.astype(F32)[..., None] * inv
    return jnp.cos(ang), jnp.sin(ang)


def _rotate_half_perm(d_total, start, half):
    p = np.zeros((d_total, d_total), np.float32)
    for i in range(half):
        p[start + half + i, start + i] = -1.0
        p[start + i, start + half + i] = 1.0
    return jnp.asarray(p, BF16)


V_ONES_ROWS = 16
FLASH_STRIP = 256


def _flash_scratch(tk, nq, dv):
    return [pltpu.VMEM((2, tk, nq), F32), pltpu.VMEM((1, nq), F32), pltpu.VMEM((1, nq), F32),
            pltpu.VMEM((dv + V_ONES_ROWS, nq), F32)]


def _flash_causal_t(get_q_t, k_ref, vt_ref, scratch, n_full, tk, last_mask):
    s_scr, m_scr, cm_scr, acc_scr = scratch
    dv = acc_scr.shape[0] - V_ONES_ROWS
    strips = [slice(r * FLASH_STRIP, (r + 1) * FLASH_STRIP) for r in range(tk // FLASH_STRIP)]

    def scores(j, slot):
        cm = None
        for r, rows in enumerate(strips):
            start = pl.multiple_of(j * tk + r * FLASH_STRIP, FLASH_STRIP)
            s = _dot(k_ref[pl.ds(start, FLASH_STRIP), :], get_q_t(j))
            s_scr[slot, rows, :] = s
            c = jnp.max(s, axis=0, keepdims=True)
            cm = c if cm is None else jnp.maximum(cm, c)
        return cm

    def stage(j, cur, nxt):
        cm_next = scores(j + 1, nxt)
        m = m_scr[...]
        m_new = jnp.maximum(m, cm_scr[...])
        pv = None
        for rows in strips:
            p = jnp.exp(s_scr[cur, rows, :] - m_new).astype(BF16)
            d = _dot(vt_ref[j, :, rows], p)
            pv = d if pv is None else pv + d
        acc_scr[...] = jnp.exp(m - m_new) * acc_scr[...] + pv
        m_scr[...] = m_new
        cm_scr[...] = cm_next

    m_scr[...] = jnp.full(m_scr.shape, -jnp.inf, F32)
    acc_scr[...] = jnp.zeros(acc_scr.shape, F32)
    cm_scr[...] = scores(0, 0)

    def pair(i, carry):
        stage(2 * i, 0, 1)
        stage(2 * i + 1, 1, 0)
        return carry

    lax.fori_loop(0, n_full // 2, pair, 0)

    @pl.when(n_full % 2 == 1)
    def _():
        stage(n_full - 1, 0, 1)

    s = jnp.where(last_mask, s_scr[n_full % 2], -jnp.inf)
    m = m_scr[...]
    m_new = jnp.maximum(m, jnp.max(s, axis=0, keepdims=True))
    p = jnp.exp(s - m_new).astype(BF16)
    acc = jnp.exp(m - m_new) * acc_scr[...] + _dot(vt_ref[n_full], p)
    return acc[0:dv, :] / acc[dv:dv + 1, :]


def _with_ones_rows_t(v, tk):
    *lead, s, dv = v.shape
    extra = jnp.zeros((*lead, s, V_ONES_ROWS), v.dtype).at[..., 0].set(1.0)
    va = jnp.concatenate([v, extra], axis=-1).astype(BF16)
    va = va.reshape(*lead, s // tk, tk, dv + V_ONES_ROWS)
    return jnp.swapaxes(va, -1, -2)


def _mla_attn_kernel(qt_ref, k_ref, vt_ref, o_ref, *scratch, tk):
    qi = pl.program_id(2)
    tq = qt_ref.shape[1]
    kpos = lax.broadcasted_iota(jnp.int32, (tk, tq), 0)
    qpos = lax.broadcasted_iota(jnp.int32, (tk, tq), 1)
    o = _flash_causal_t(lambda j: qt_ref[...], k_ref, vt_ref, scratch, qi, tk, kpos <= qpos)
    o_ref[...] = o.astype(o_ref.dtype)


def _mla_attention(q_t, k, v_t, tq, tk):
    b, h, dq, s = q_t.shape
    nk, dva = v_t.shape[2], v_t.shape[3]
    dv = dva - V_ONES_ROWS
    assert tq == tk
    return pl.pallas_call(
        functools.partial(_mla_attn_kernel, tk=tk),
        out_shape=jax.ShapeDtypeStruct((b, h, dv, s), F32),
        grid=(b, h, s // tq),
        in_specs=[
            pl.BlockSpec((None, None, dq, tq), lambda bi, hi, qi: (bi, hi, 0, qi)),
            pl.BlockSpec((None, None, s, dq), lambda bi, hi, qi: (bi, hi, 0, 0)),
            pl.BlockSpec((None, None, nk, dva, tk), lambda bi, hi, qi: (bi, hi, 0, 0, 0)),
        ],
        out_specs=pl.BlockSpec((None, None, dv, tq), lambda bi, hi, qi: (bi, hi, 0, qi)),
        scratch_shapes=_flash_scratch(tk, tq, dv),
        compiler_params=_params(("parallel", "parallel", "arbitrary")),
        name="mla_flash",
    )(q_t, k, v_t)


def _compress_kernel(x_ref, pelo_ref, pehi_ref, w1lo_ref, w1hi_ref, w2_ref, g_ref, o_ref, *, norm):
    x = x_ref[...]
    nc = x.shape[0]
    a = _dot((x + pelo_ref[...]).astype(BF16), w1lo_ref[...])
    b = _dot((x + pehi_ref[...]).astype(BF16), w1hi_ref[...])
    hid = _silu(a + pltpu.roll(b, shift=nc - 1, axis=0))
    y = _dot(hid.astype(BF16), w2_ref[...])
    if norm:
        y = _rms(y, g_ref[...])
    o_ref[...] = y.astype(o_ref.dtype)


def _compress(x, pe, w1, w2, gain, norm):
    b, g, nc, wdt = x.shape
    dh = w2.shape[1]
    hidden = w1.shape[1]
    half = NSA_CMP_STRIDE
    pelo = pe[:half].reshape(1, wdt)
    pehi = pe[half:].reshape(1, wdt)
    w1 = w1.astype(BF16)
    const = lambda bi, gi: (0, 0)
    return pl.pallas_call(
        functools.partial(_compress_kernel, norm=norm),
        out_shape=jax.ShapeDtypeStruct((b, g, nc, dh), BF16),
        grid=(b, g),
        in_specs=[
            pl.BlockSpec((None, None, nc, wdt), lambda bi, gi: (bi, gi, 0, 0)),
            pl.BlockSpec((1, wdt), const),
            pl.BlockSpec((1, wdt), const),
            pl.BlockSpec((wdt, hidden), const),
            pl.BlockSpec((wdt, hidden), const),
            pl.BlockSpec((hidden, dh), const),
            pl.BlockSpec((1, dh), const),
        ],
        out_specs=pl.BlockSpec((None, None, nc, dh), lambda bi, gi: (bi, gi, 0, 0)),
        compiler_params=_params(("parallel", "parallel")),
        name="nsa_compress",
    )(x, pelo, pehi, w1[:wdt], w1[wdt:], w2.astype(BF16), gain.reshape(1, dh))


def _nsa_cmp_kernel(q_ref, kc_ref, vc_ref, band_ref, o_ref, sb_ref, *, tq, n_rep):
    qi = pl.program_id(2)
    q0 = qi * tq
    rows = n_rep * tq
    dh = q_ref.shape[-1]
    q = q_ref[...].reshape(rows, dh)
    kc = kc_ref[...]
    nc = kc.shape[0]
    s = _dot_nt(q, kc)
    tpos = q0 + (lax.broadcasted_iota(jnp.int32, (rows, nc), 0) & (tq - 1))
    cmp_end = lax.broadcasted_iota(jnp.int32, (rows, nc), 1) * NSA_CMP_STRIDE + (NSA_CMP_LEN - 1)
    valid = (cmp_end <= tpos) & (cmp_end < nc * NSA_CMP_STRIDE)
    s = jnp.where(valid, s, -jnp.inf)
    m = jnp.max(s, axis=-1, keepdims=True)
    m = jnp.where(m > -jnp.inf, m, 0.0)
    p = jnp.exp(s - m)
    p = p / jnp.maximum(jnp.sum(p, axis=-1, keepdims=True), 1e-30)
    o_ref[...] = _dot(p.astype(BF16), vc_ref[...]).reshape(n_rep, tq, dh).astype(o_ref.dtype)

    psum = p[0:tq]
    for r in range(1, n_rep):
        psum = psum + p[r * tq:(r + 1) * tq]
    band = band_ref[...]
    hi, mid, lo = _split3(psum)
    imp = _dot_nt(band, hi) + _dot_nt(band, mid) + _dot_nt(band, lo)

    nb = band.shape[0]
    blk = lax.broadcasted_iota(jnp.int32, (nb, tq), 0)
    cur = (q0 + lax.broadcasted_iota(jnp.int32, (nb, tq), 1)) // NSA_SEL_LEN
    causal = blk <= cur
    forced = causal & ((blk == 0) | (blk > cur - NSA_N_LOCAL))
    score = jnp.where(forced, 1e30, jnp.where(causal, imp, -1.0))
    blk_f = blk.astype(F32)
    bias = jnp.full((nb, tq), SEL_MASK_BIAS, F32)
    for _ in range(NSA_N_SEL):
        best = jnp.max(score, axis=0, keepdims=True)
        idx = jnp.min(jnp.where(score == best, blk_f, float(nb)), axis=0, keepdims=True)
        pick = (blk_f == idx) & (best >= 0.0)
        bias = jnp.where(pick, 0.0, bias)
        score = jnp.where(pick, -1.0, score)
    sb_ref[...] = bias.astype(sb_ref.dtype)


def _nsa_cmp_select(q, k_cmp, v_cmp, band, tq=NSA_TQ):
    b, g, r, s, dh = q.shape
    nc = k_cmp.shape[2]
    nb = band.shape[0]
    return pl.pallas_call(
        functools.partial(_nsa_cmp_kernel, tq=tq, n_rep=r),
        out_shape=(jax.ShapeDtypeStruct((b, g, r, s, dh), F32),
                   jax.ShapeDtypeStruct((b, g, nb, s), BF16)),
        grid=(b, g, s // tq),
        in_specs=[
            pl.BlockSpec((None, None, r, tq, dh), lambda bi, gi, qi: (bi, gi, 0, qi, 0)),
            pl.BlockSpec((None, None, nc, dh), lambda bi, gi, qi: (bi, gi, 0, 0)),
            pl.BlockSpec((None, None, nc, dh), lambda bi, gi, qi: (bi, gi, 0, 0)),
            pl.BlockSpec((nb, nc), lambda bi, gi, qi: (0, 0)),
        ],
        out_specs=(pl.BlockSpec((None, None, r, tq, dh), lambda bi, gi, qi: (bi, gi, 0, qi, 0)),
                   pl.BlockSpec((None, None, nb, tq), lambda bi, gi, qi: (bi, gi, 0, qi))),
        compiler_params=_params(("parallel", "parallel", "parallel")),
        name="nsa_cmp_select",
    )(q, k_cmp, v_cmp, band)


def _nsa_sel_kernel(qt_ref, sb_ref, k_ref, vt_ref, o_ref, qa_scr, *scratch, tq, tk, n_rep, tiles_per_half):
    qi = pl.program_id(2)
    q0 = qi * tq
    nq = n_rep * tq
    n_half = qa_scr.shape[0]
    q_t = qt_ref[...]
    for hh in range(n_half):
        sb = sb_ref[hh * SEL_ONEHOT:(hh + 1) * SEL_ONEHOT, :]
        qa_scr[hh] = jnp.concatenate([jnp.concatenate([sb] * n_rep, axis=1), q_t], axis=0)
    jd = q0 // tk
    kpos = jd * tk + lax.broadcasted_iota(jnp.int32, (tk, nq), 0)
    tpos = q0 + (lax.broadcasted_iota(jnp.int32, (tk, nq), 1) & (tq - 1))
    o = _flash_causal_t(lambda j: qa_scr[j // tiles_per_half], k_ref, vt_ref, scratch, jd, tk, kpos <= tpos)
    o_ref[...] = o.astype(o_ref.dtype)


def _nsa_sel_attention(q_t, sel_bias, k_aug, v_t, tq, tk):
    b, g, n_q, dh, nq = q_t.shape
    nb, s = sel_bias.shape[2], sel_bias.shape[3]
    n_half = nb // SEL_ONEHOT
    ka = k_aug.shape[-1]
    nk, dva = v_t.shape[2], v_t.shape[3]
    tiles_per_half = SEL_ONEHOT * NSA_SEL_LEN // tk
    return pl.pallas_call(
        functools.partial(_nsa_sel_kernel, tq=tq, tk=tk, n_rep=nq // tq, tiles_per_half=tiles_per_half),
        out_shape=jax.ShapeDtypeStruct((b, g, n_q, dh, nq), F32),
        grid=(b, g, n_q),
        in_specs=[
            pl.BlockSpec((None, None, None, dh, nq), lambda bi, gi, qi: (bi, gi, qi, 0, 0)),
            pl.BlockSpec((None, None, nb, tq), lambda bi, gi, qi: (bi, gi, 0, qi)),
            pl.BlockSpec((None, None, s, ka), lambda bi, gi, qi: (bi, gi, 0, 0)),
            pl.BlockSpec((None, None, nk, dva, tk), lambda bi, gi, qi: (bi, gi, 0, 0, 0)),
        ],
        out_specs=pl.BlockSpec((None, None, None, dh, nq), lambda bi, gi, qi: (bi, gi, qi, 0, 0)),
        scratch_shapes=[pltpu.VMEM((n_half, ka, nq), BF16)] + _flash_scratch(tk, nq, dva - V_ONES_ROWS),
        compiler_params=_params(("parallel", "parallel", "arbitrary")),
        name="nsa_sel_flash",
    )(q_t, sel_bias, k_aug, v_t)


def _nsa_win_kernel(q_ref, k_ref, v_ref, o_ref, *, tq, win, n_rep):
    qi = pl.program_id(2)
    q0 = pl.multiple_of(qi * tq, tq)
    rows = n_rep * tq
    dh = q_ref.shape[-1]
    span = tq + win
    q = q_ref[...].reshape(rows, dh)
    s = _dot_nt(q, k_ref[pl.ds(q0, span), :])
    tpos = q0 + (lax.broadcasted_iota(jnp.int32, (rows, span), 0) & (tq - 1))
    kpos = q0 - win + lax.broadcasted_iota(jnp.int32, (rows, span), 1)
    mask = (kpos <= tpos) & (kpos > tpos - win) & (kpos >= 0)
    s = jnp.where(mask, s, -jnp.inf)
    m = jnp.max(s, axis=-1, keepdims=True)
    p = jnp.exp(s - m)
    l = jnp.sum(p, axis=-1, keepdims=True)
    o = _dot(p.astype(BF16), v_ref[pl.ds(q0, span), :]) / l
    o_ref[...] = o.reshape(n_rep, tq, dh).astype(o_ref.dtype)


def _nsa_win_attention(q, k_pad, v_pad, tq=NSA_TQ):
    b, g, r, s, dh = q.shape
    sp = k_pad.shape[2]
    return pl.pallas_call(
        functools.partial(_nsa_win_kernel, tq=tq, win=NSA_WINDOW, n_rep=r),
        out_shape=jax.ShapeDtypeStruct((b, g, r, s, dh), F32),
        grid=(b, g, s // tq),
        in_specs=[
            pl.BlockSpec((None, None, r, tq, dh), lambda bi, gi, qi: (bi, gi, 0, qi, 0)),
            pl.BlockSpec((None, None, sp, dh), lambda bi, gi, qi: (bi, gi, 0, 0)),
            pl.BlockSpec((None, None, sp, dh), lambda bi, gi, qi: (bi, gi, 0, 0)),
        ],
        out_specs=pl.BlockSpec((None, None, r, tq, dh), lambda bi, gi, qi: (bi, gi, 0, qi, 0)),
        compiler_params=_params(("parallel", "parallel", "arbitrary")),
        name="nsa_window",
    )(q, k_pad, v_pad)


def _nsa_out_kernel(oc_ref, os_ref, ow_ref, gl_ref, e_ref, w_ref, x_ref, gate_ref, o_ref):
    sg = _sigmoid(gl_ref[...])
    hi = sg.astype(BF16)
    lo = (sg - hi.astype(F32)).astype(BF16)

    def expand(i):
        return _dot(hi, e_ref[i]) + _dot(lo, e_ref[i])

    o = expand(0) * oc_ref[...] + expand(1) * os_ref[...] + expand(2) * ow_ref[...]
    o_ref[...] = x_ref[...] + gate_ref[...] * _dot(o.astype(BF16), w_ref[...])


def _nsa_out(o_cmp, o_sel, o_win, gate_logits, expand, w, x, gate, seq, tm=512):
    t, d = x.shape
    k = o_cmp.shape[1]
    gw = gate_logits.shape[1]
    b = gate.shape[0]
    tpb = seq // tm
    row = lambda i: (i, 0)
    return pl.pallas_call(
        _nsa_out_kernel,
        out_shape=jax.ShapeDtypeStruct((t, d), F32),
        grid=(t // tm,),
        in_specs=[
            pl.BlockSpec((tm, k), row),
            pl.BlockSpec((tm, k), row),
            pl.BlockSpec((tm, k), row),
            pl.BlockSpec((tm, gw), row),
            pl.BlockSpec((3, gw, k), lambda i: (0, 0, 0)),
            pl.BlockSpec((k, d), lambda i: (0, 0)),
            pl.BlockSpec((tm, d), row),
            pl.BlockSpec((None, 1, d), lambda i: (i // tpb, 0, 0)),
        ],
        out_specs=pl.BlockSpec((tm, d), row),
        compiler_params=_params(("parallel",)),
        name="nsa_out",
    )(o_cmp, o_sel, o_win, gate_logits, expand, w.astype(BF16), x, gate.reshape(b, 1, d))


def _ret_kernel(lg_ref, q_ref, k_ref, v_ref, g_ref, cos_ref, sin_ref, gn_ref, o_ref, st_scr):
    h = pl.program_id(1)
    c = pl.program_id(2)

    @pl.when(c == 0)
    def _():
        st_scr[...] = jnp.zeros_like(st_scr)

    cn = q_ref.shape[0]
    dk = q_ref.shape[1]
    half = dk // 2
    lg = jnp.full((1, 1), lg_ref[h], F32)
    cos = cos_ref[...]
    sin = sin_ref[...]

    def rope(x):
        x1, x2 = x[:, :half], x[:, half:]
        return jnp.concatenate([x1 * cos - x2 * sin, x1 * sin + x2 * cos], axis=1)

    q = rope(q_ref[...])
    k = rope(k_ref[...]) * (dk ** -0.5)
    v = v_ref[...].astype(BF16)
    n = lax.broadcasted_iota(jnp.int32, (cn, 1), 0).astype(F32)
    rel = (lax.broadcasted_iota(jnp.int32, (cn, cn), 0) - lax.broadcasted_iota(jnp.int32, (cn, cn), 1)).astype(F32)
    decay = jnp.where(rel >= 0, jnp.exp(jnp.maximum(rel, 0.0) * lg), 0.0)
    xi = jnp.exp((n + 1.0) * lg)
    zeta = jnp.exp((cn - 1.0 - n) * lg)
    gamma_c = jnp.exp(cn * lg)

    qb = q.astype(BF16)
    inner = _dot((_dot_nt(qb, k.astype(BF16)) * decay).astype(BF16), v)
    st = st_scr[...]
    cross = _dot(qb, st.astype(BF16)) * xi
    kz_t = (k * zeta).T.astype(BF16)
    st_scr[...] = gamma_c * st + _dot(kz_t, v)
    y = inner + cross
    mu = jnp.mean(y, axis=-1, keepdims=True)
    yc = y - mu
    var = jnp.mean(yc * yc, axis=-1, keepdims=True)
    yn = yc * lax.rsqrt(var + NORM_EPS) * gn_ref[...]
    o_ref[...] = (_silu(g_ref[...]) * yn).astype(o_ref.dtype)


def _retention(proj, cos_t, sin_t, log_gamma, gn_w, batch, seq, heads):
    t = proj.shape[0]
    dk, dv, cn = RET_QK, RET_V, RET_CHUNK
    n_ch = seq // cn
    kb = heads * dk // dk
    vb = 2 * heads * dk // dv
    row = lambda bi, hi, ci, *_: bi * n_ch + ci
    grid_spec = pltpu.PrefetchScalarGridSpec(
        num_scalar_prefetch=1,
        grid=(batch, heads, n_ch),
        in_specs=[
            pl.BlockSpec((cn, dk), lambda bi, hi, ci, lg: (row(bi, hi, ci), hi)),
            pl.BlockSpec((cn, dk), lambda bi, hi, ci, lg: (row(bi, hi, ci), kb + hi)),
            pl.BlockSpec((cn, dv), lambda bi, hi, ci, lg: (row(bi, hi, ci), vb + hi)),
            pl.BlockSpec((cn, dv), lambda bi, hi, ci, lg: (row(bi, hi, ci), vb + heads + hi)),
            pl.BlockSpec((cn, dk // 2), lambda bi, hi, ci, lg: (row(bi, hi, ci), 0)),
            pl.BlockSpec((cn, dk // 2), lambda bi, hi, ci, lg: (row(bi, hi, ci), 0)),
            pl.BlockSpec((1, dv), lambda bi, hi, ci, lg: (0, hi)),
        ],
        out_specs=pl.BlockSpec((cn, dv), lambda bi, hi, ci, lg: (row(bi, hi, ci), hi)),
        scratch_shapes=[pltpu.VMEM((dk, dv), F32)],
    )
    return pl.pallas_call(
        _ret_kernel,
        out_shape=jax.ShapeDtypeStruct((t, heads * dv), BF16),
        grid_spec=grid_spec,
        compiler_params=_params(("parallel", "parallel", "arbitrary")),
        name="retention",
    )(log_gamma, proj, proj, proj, proj, cos_t, sin_t, gn_w.reshape(1, heads * dv))


def _softplus(x):
    return jnp.maximum(x, 0.0) + jnp.log1p(jnp.exp(-jnp.abs(x)))


def _ssd_kernel(row_ref, dtt_ref, cw_ref, cb_ref, dtb_ref, dtbt_ref, al_ref, alt_ref, dsk_ref, nw_ref,
                ex_ref, o_ref, buf_scr, st_scr, *, inner, groups, state, heads):
    c = pl.program_id(1)
    cn = row_ref.shape[0]
    conv_ch = inner + 2 * groups * state
    hp = inner // heads
    rep = heads // groups
    gw = rep * hp
    halo = 8

    @pl.when(c == 0)
    def _():
        buf_scr[0:halo, :] = jnp.zeros((halo, conv_ch), F32)
        st_scr[...] = jnp.zeros_like(st_scr)

    z = row_ref[:, 0:inner]
    xbc = row_ref[:, inner:inner + conv_ch]
    dt_raw = row_ref[:, inner + conv_ch:inner + conv_ch + LANES]

    buf_scr[halo:halo + cn, :] = xbc
    conv = cb_ref[...] + cw_ref[0:1, :] * buf_scr[halo - 3:halo - 3 + cn, :]
    for kk in range(1, SSD_CONV):
        conv = conv + cw_ref[kk:kk + 1, :] * buf_scr[halo - 3 + kk:halo - 3 + kk + cn, :]
    buf_scr[0:halo, :] = xbc[cn - halo:cn, :]
    act = _silu(conv)
    xs = act[:, 0:inner]
    bm = act[:, inner:inner + groups * state]
    cm = act[:, inner + groups * state:conv_ch]

    dt = _softplus(dt_raw + dtb_ref[...])
    dt_t = _softplus(dtt_ref[...] + dtbt_ref[...])
    a = -jnp.exp(al_ref[...])
    a_t = -jnp.exp(alt_ref[...])
    ri = lax.broadcasted_iota(jnp.int32, (cn, cn), 0)
    ci = lax.broadcasted_iota(jnp.int32, (cn, cn), 1)
    tril = ci <= ri
    lower = jnp.where(tril, 1.0, 0.0).astype(BF16)
    upper = jnp.where(ci >= ri, 1.0, 0.0).astype(BF16)
    cum = _dot3_left(lower, dt * a)
    cum_t = _dot3(dt_t * a_t, upper)
    cum_last = cum[cn - 1:cn, :]

    ex = ex_ref[...]
    e_cum = _dot3(jnp.exp(cum), ex)
    w_end = _dot3(jnp.exp(cum_last - cum) * dt, ex)
    e_last = _dot3(jnp.exp(cum_last), ex)

    lane = lax.broadcasted_iota(jnp.int32, (cn, 2 * hp), 1)
    ys = []
    for g in range(groups):
        cc = cm[:, g * state:(g + 1) * state].astype(BF16)
        bc = bm[:, g * state:(g + 1) * state].astype(BF16)
        cb = _dot_nt(cc, bc)
        xg = xs[:, g * gw:(g + 1) * gw]
        st = st_scr[g]
        cross = _dot(cc, st.astype(BF16)) * e_cum[:, g * gw:(g + 1) * gw]
        xw_t = bc.astype(F32).T.astype(BF16)
        st_scr[g] = e_last[:, g * gw:(g + 1) * gw] * st + _dot(
            xw_t, (xg * w_end[:, g * gw:(g + 1) * gw]).astype(BF16))
        intra = []
        for pr in range(rep // 2):
            ws = []
            for hh in (2 * pr, 2 * pr + 1):
                hd = g * rep + hh
                seg = jnp.where(tril, cum[:, hd:hd + 1] - cum_t[hd:hd + 1, :], -jnp.inf)
                ws.append((jnp.exp(seg) * cb * dt_t[hd:hd + 1, :]).astype(BF16))
            w2 = jnp.concatenate(ws, axis=1)
            xp = xg[:, 2 * pr * hp:(2 * pr + 2) * hp]
            x2 = jnp.concatenate([jnp.where(lane < hp, xp, 0.0), jnp.where(lane >= hp, xp, 0.0)], axis=0)
            intra.append(_dot(w2, x2.astype(BF16)))
        ys.append(jnp.concatenate(intra, axis=1) + cross)
    y = jnp.concatenate(ys, axis=1) + dsk_ref[...] * xs
    y = y * _silu(z)
    outs = []
    for g in range(groups):
        outs.append(_rms(y[:, g * gw:(g + 1) * gw], nw_ref[:, g * gw:(g + 1) * gw]))
    o_ref[...] = jnp.concatenate(outs, axis=1).astype(o_ref.dtype)


def _ssd(proj, dt_t, conv_w, conv_b, dt_bias, a_log, d_skip, norm_w, batch, seq, inner, heads):
    t, n_pad = proj.shape
    groups, state, cn = SSD_GROUPS, SSD_STATE, SSD_CHUNK
    conv_ch = inner + 2 * groups * state
    n_ch = seq // cn
    hp = inner // heads
    gw = inner // groups
    hpad = LANES
    assert heads <= hpad and n_pad >= inner + conv_ch + hpad
    ex_np = np.zeros((hpad, inner), np.float32)
    ex_np[:heads] = np.kron(np.eye(heads, dtype=np.float32), np.ones((1, hp), np.float32))
    ex = jnp.asarray(ex_np, BF16)
    dt_bias = jnp.pad(dt_bias, (0, hpad - heads))
    a_log = jnp.pad(a_log, (0, hpad - heads))
    heads_k = heads
    heads = hpad
    const = lambda bi, ci: (0, 0)
    return pl.pallas_call(
        functools.partial(_ssd_kernel, inner=inner, groups=groups, state=state, heads=heads_k),
        out_shape=jax.ShapeDtypeStruct((t, inner), BF16),
        grid=(batch, n_ch),
        in_specs=[
            pl.BlockSpec((cn, n_pad), lambda bi, ci: (bi * n_ch + ci, 0)),
            pl.BlockSpec((None, heads, cn), lambda bi, ci: (bi, 0, ci)),
            pl.BlockSpec((SSD_CONV, conv_ch), const),
            pl.BlockSpec((1, conv_ch), const),
            pl.BlockSpec((1, heads), const),
            pl.BlockSpec((heads, 1), const),
            pl.BlockSpec((1, heads), const),
            pl.BlockSpec((heads, 1), const),
            pl.BlockSpec((1, inner), const),
            pl.BlockSpec((1, inner), const),
            pl.BlockSpec((heads, inner), const),
        ],
        out_specs=pl.BlockSpec((cn, inner), lambda bi, ci: (bi * n_ch + ci, 0)),
        scratch_shapes=[pltpu.VMEM((8 + cn, conv_ch), F32), pltpu.VMEM((groups, state, gw), F32)],
        compiler_params=_params(("parallel", "arbitrary")),
        name="ssd_scan",
    )(proj, dt_t, conv_w, conv_b.reshape(1, conv_ch), dt_bias.reshape(1, heads), dt_bias.reshape(heads, 1),
      a_log.reshape(1, heads), a_log.reshape(heads, 1), jnp.repeat(d_skip, hp).reshape(1, inner),
      norm_w.reshape(1, inner), ex)


def _nsa_layer(x, mods, norm_g, batch, seq, w_in, q_norm, k_norm, cmp_pe, cmp_w1, cmp_w2, w_out):
    sh, sc, gate = mods
    d = x.shape[1]
    dh, g = HEAD_DIM, NSA_GROUPS
    heads = d // dh
    r = heads // g
    kvw = g * dh
    proj = _modnorm_matmul(x, norm_g, sc, sh, w_in, seq)
    off = heads * dh

    def heads_major(cols, n_heads):
        return cols.reshape(batch, seq, n_heads, dh).transpose(0, 2, 1, 3)

    q = heads_major(proj[:, :off], heads)
    parts = [proj[:, off + i * kvw: off + (i + 1) * kvw] for i in range(6)]
    k_c, v_c, k_s, v_s, k_w, v_w = parts
    gate_logits = proj[:, off + 6 * kvw: off + 6 * kvw + LANES]

    qn = _norm_rope(q, q_norm, scale=dh ** -0.5).reshape(batch, g, r, seq, dh)

    def chunk_rows(cols):
        st = NSA_CMP_STRIDE
        return cols.reshape(batch, seq // st, st, g, dh).transpose(0, 3, 1, 2, 4).reshape(batch, g, seq // st, st * dh)

    k_cmp = _compress(chunk_rows(k_c), cmp_pe[0], cmp_w1[0], cmp_w2[0], k_norm[0], norm=True)
    v_cmp = _compress(chunk_rows(v_c), cmp_pe[1], cmp_w1[1], cmp_w2[1], k_norm[0], norm=False)

    n_blk = seq // NSA_SEL_LEN
    n_cmp = seq // NSA_CMP_STRIDE
    assert n_blk % SEL_ONEHOT == 0
    ratio = NSA_SEL_LEN // NSA_CMP_STRIDE
    jj = np.arange(n_blk)[:, None]
    cc = np.arange(n_cmp)[None, :]
    band = jnp.asarray(((cc >= ratio * jj - 1) & (cc <= ratio * jj + ratio - 1)).astype(np.float32), BF16)
    o_cmp, sel_bias = _nsa_cmp_select(qn, k_cmp, v_cmp, band)

    ks_n = _norm_rope(heads_major(k_s, g), k_norm[1])
    onehot = jnp.asarray(np.eye(SEL_ONEHOT, dtype=np.float32)[(np.arange(seq) // NSA_SEL_LEN) % SEL_ONEHOT], BF16)
    k_aug = jnp.concatenate([jnp.broadcast_to(onehot, (batch, g, seq, SEL_ONEHOT)), ks_n], axis=-1)
    tq, tk = NSA_TQ, min(NSA_TK, seq)
    n_q = seq // tq
    q_t = qn.reshape(batch, g, r, n_q, tq, dh).transpose(0, 1, 3, 5, 2, 4).reshape(batch, g, n_q, dh, r * tq)
    o_sel_t = _nsa_sel_attention(q_t, sel_bias, k_aug, _with_ones_rows_t(heads_major(v_s, g), tk), tq, tk)
    o_sel = o_sel_t.reshape(batch, g, n_q, dh, r, tq).transpose(0, 1, 4, 2, 5, 3).reshape(batch, g, r, seq, dh)

    kw_n = _norm_rope(heads_major(k_w, g), k_norm[2])
    pad = ((0, 0), (0, 0), (NSA_WINDOW, 0), (0, 0))
    o_win = _nsa_win_attention(qn, jnp.pad(kw_n, pad), jnp.pad(heads_major(v_w, g).astype(BF16), pad))

    def token_major(o):
        return o.transpose(0, 3, 1, 2, 4).reshape(batch * seq, heads * dh)

    e = np.zeros((3, LANES, heads * dh), np.float32)
    for hd in range(heads):
        for i in range(3):
            e[i, hd * 3 + i, hd * dh:(hd + 1) * dh] = 1.0
    return _nsa_out(token_major(o_cmp), token_major(o_sel), token_major(o_win), gate_logits,
                    jnp.asarray(e, BF16), w_out, x, gate, seq)


def _mla_layer(x, mods, norm_g, batch, seq, pos, w_in, q_a_norm, kv_a_norm, w_q_b, w_kv_b, q_norm, k_norm, w_out):
    sh, sc, gate = mods
    d = x.shape[1]
    heads = w_out.shape[0] // MLA_V
    dq = MLA_NOPE + MLA_ROPE
    proj = _modnorm_matmul(x, norm_g, sc, sh, w_in, seq)
    cq = proj[:, :MLA_Q_LORA]
    ckv = proj[:, MLA_Q_LORA:MLA_Q_LORA + MLA_KV_LORA]
    k_rot = proj[:, MLA_Q_LORA + MLA_KV_LORA:MLA_Q_LORA + MLA_KV_LORA + MLA_ROPE]
    q_raw = _norm_matmul(cq, q_a_norm, w_q_b)
    kv_raw = _norm_matmul(ckv, kv_a_norm, w_kv_b)
    q_raw = q_raw.reshape(batch, seq, heads, dq).transpose(0, 2, 1, 3)
    kv_raw = kv_raw.reshape(batch, seq, heads, MLA_NOPE + MLA_V).transpose(0, 2, 1, 3)

    cos, sin = _rope_tables(pos, MLA_ROPE)
    ones = jnp.ones((batch, seq, MLA_NOPE), F32)
    cos_q = jnp.concatenate([ones, cos, cos], axis=-1)
    sin_q = jnp.concatenate([0.0 * ones, sin, sin], axis=-1)
    qn = _norm_rope(q_raw, q_norm, cos_q, sin_q, _rotate_half_perm(dq, MLA_NOPE, MLA_ROPE // 2), scale=dq ** -0.5)
    k_nope = _norm_rope(kv_raw[..., :MLA_NOPE], k_norm[:MLA_NOPE])
    k_rope = _norm_rope(k_rot.reshape(batch, 1, seq, MLA_ROPE), k_norm[MLA_NOPE:],
                        jnp.concatenate([cos, cos], axis=-1), jnp.concatenate([sin, sin], axis=-1),
                        _rotate_half_perm(MLA_ROPE, 0, MLA_ROPE // 2))
    k = jnp.concatenate([k_nope, jnp.broadcast_to(k_rope, (batch, heads, seq, MLA_ROPE))], axis=-1)
    tq = min(MLA_TQ, seq)
    tk = min(MLA_TK, tq)
    o_t = _mla_attention(jnp.swapaxes(qn, -1, -2), k, _with_ones_rows_t(kv_raw[..., MLA_NOPE:], tk), tq, tk)
    o = o_t.transpose(0, 3, 1, 2).reshape(batch * seq, heads * MLA_V).astype(BF16)
    return _out_proj(o, w_out, x, gate, seq)


def _ret_layer(x, mods, norm_g, batch, seq, pos, w_in, gn_w, w_out):
    sh, sc, gate = mods
    heads = w_out.shape[0] // RET_V
    proj = _modnorm_matmul(x, norm_g, sc, sh, w_in, seq)
    cos, sin = _rope_tables(pos, RET_QK)
    cos = cos.reshape(batch * seq, RET_QK // 2)
    sin = sin.reshape(batch * seq, RET_QK // 2)
    log_gamma = jnp.log1p(-(2.0 ** (-5.0 - jnp.arange(heads, dtype=F32))))
    y = _retention(proj, cos, sin, log_gamma, gn_w, batch, seq, heads)
    return _out_proj(y, w_out, x, gate, seq)


def _ssd_layer(x, mods, norm_g, batch, seq, w_in, conv_w, conv_b, dt_bias, a_log, d_skip, norm_w, w_out):
    sh, sc, gate = mods
    inner = w_out.shape[0]
    heads = dt_bias.shape[0]
    conv_ch = conv_w.shape[1]
    proj = _modnorm_matmul(x, norm_g, sc, sh, w_in, seq)
    dt_t = proj[:, inner + conv_ch:inner + conv_ch + LANES].reshape(batch, seq, LANES).transpose(0, 2, 1)
    y = _ssd(proj, dt_t, conv_w, conv_b, dt_bias, a_log, d_skip, norm_w, batch, seq, inner, heads)
    return _out_proj(y, w_out, x, gate, seq)


def kernel(x, c, positions, ada_w, ada_b, norm_mix, norm_ffn, ffn_w_in, ffn_w_out, nsa_w_in, nsa_q_norm, nsa_k_norm, nsa_cmp_pe, nsa_cmp_w1, nsa_cmp_w2, nsa_w_out, mla_w_in, mla_q_a_norm, mla_kv_a_norm, mla_w_q_b, mla_w_kv_b, mla_q_norm, mla_k_norm, mla_w_out, ret_w_in, ret_gn_w, ret_w_out, ssd_w_in, ssd_conv_w, ssd_conv_b, ssd_dt_bias, ssd_a_log, ssd_d, ssd_norm, ssd_w_out):
    batch, seq, d = x.shape
    depth = ada_w.shape[0]
    n_mixers = 4
    mod = _modulation(c, ada_w, ada_b)
    xt = x.reshape(batch * seq, d)
    for i in range(depth):
        sh_m, sc_m, g_m, sh_f, sc_f, g_f = [mod[i, :, k * d:(k + 1) * d] for k in range(6)]
        mods = (sh_m, sc_m, g_m)
        kind, j = i % n_mixers, i // n_mixers
        if kind == 0:
            xt = _nsa_layer(xt, mods, norm_mix[i], batch, seq, nsa_w_in[j], nsa_q_norm[j], nsa_k_norm[j],
                            nsa_cmp_pe[j], nsa_cmp_w1[j], nsa_cmp_w2[j], nsa_w_out[j])
        elif kind == 1:
            xt = _mla_layer(xt, mods, norm_mix[i], batch, seq, positions, mla_w_in[j], mla_q_a_norm[j],
                            mla_kv_a_norm[j], mla_w_q_b[j], mla_w_kv_b[j], mla_q_norm[j], mla_k_norm[j], mla_w_out[j])
        elif kind == 2:
            xt = _ret_layer(xt, mods, norm_mix[i], batch, seq, positions, ret_w_in[j], ret_gn_w[j], ret_w_out[j])
        else:
            xt = _ssd_layer(xt, mods, norm_mix[i], batch, seq, ssd_w_in[j], ssd_conv_w[j], ssd_conv_b[j],
                            ssd_dt_bias[j], ssd_a_log[j], ssd_d[j], ssd_norm[j], ssd_w_out[j])
        xt = _ffn(xt, norm_ffn[i], sc_f, sh_f, g_f, ffn_w_in[i], ffn_w_out[i], seq)
    return xt.reshape(batch, seq, d)
```

```python
import functools
import math

import numpy as np
import jax
import jax.numpy as jnp
from jax import lax
from jax.experimental import pallas as pl
from jax.experimental.pallas import tpu as pltpu

F32 = jnp.float32
BF16 = jnp.bfloat16

NORM_EPS = 1e-6
ROPE_THETA = 10000.0
LANES = 128

HEAD_DIM = 64
NSA_GROUPS = 4
NSA_CMP_STRIDE = 16
NSA_CMP_LEN = 32
NSA_SEL_LEN = 64
NSA_N_SEL = 16
NSA_N_LOCAL = 2
NSA_WINDOW = 512
NSA_TQ = 128
NSA_TK = 512
SEL_ONEHOT = 128
SEL_MASK_BIAS = -32768.0

MLA_Q_LORA = 384
MLA_KV_LORA = 256
MLA_NOPE = 64
MLA_ROPE = 32
MLA_V = 64
MLA_TQ = 512
MLA_TK = 512

RET_QK = 256
RET_V = 512
RET_CHUNK = 128

SSD_HEADDIM = 64
SSD_GROUPS = 4
SSD_STATE = 128
SSD_CONV = 4
SSD_CHUNK = 128

VMEM_LIMIT = 48 * 1024 * 1024


def _params(sem, vmem=VMEM_LIMIT, flags=None):
    return pltpu.CompilerParams(dimension_semantics=sem, vmem_limit_bytes=vmem, flags=flags)


def _sigmoid(x):
    return 1.0 / (1.0 + jnp.exp(-x))


def _silu(x):
    return x * _sigmoid(x)


def _dot(a, b):
    return jnp.dot(a, b, preferred_element_type=F32)


def _dot_nt(a, b):
    return lax.dot_general(a, b, (((1,), (1,)), ((), ())), preferred_element_type=F32)


def _split3(x):
    hi = x.astype(BF16)
    r1 = x - hi.astype(F32)
    mid = r1.astype(BF16)
    lo = (r1 - mid.astype(F32)).astype(BF16)
    return hi, mid, lo


def _dot3(x, m):
    hi, mid, lo = _split3(x)
    return _dot(hi, m) + _dot(mid, m) + _dot(lo, m)


def _dot3_left(m, x):
    hi, mid, lo = _split3(x)
    return _dot(m, hi) + _dot(m, mid) + _dot(m, lo)


def _rms(x, gain):
    ms = jnp.mean(x * x, axis=-1, keepdims=True)
    return x * lax.rsqrt(ms + NORM_EPS) * gain


def _mod_kernel(c_ref, w_ref, b_ref, o_ref):
    cond = _silu(c_ref[...]).astype(BF16)
    o_ref[...] = _dot(cond, w_ref[...]) + b_ref[...]


def _modulation(c, ada_w, ada_b):
    depth, d, n = ada_w.shape
    b = c.shape[0]
    rows = 16
    c_pad = jnp.zeros((rows, d), F32).at[:b].set(c)
    tn = 1024
    out = pl.pallas_call(
        _mod_kernel,
        out_shape=jax.ShapeDtypeStruct((depth, rows, n), F32),
        grid=(depth, n // tn),
        in_specs=[
            pl.BlockSpec((rows, d), lambda l, j: (0, 0)),
            pl.BlockSpec((None, d, tn), lambda l, j: (l, 0, j)),
            pl.BlockSpec((None, 1, tn), lambda l, j: (l, 0, j)),
        ],
        out_specs=pl.BlockSpec((None, rows, tn), lambda l, j: (l, 0, j)),
        compiler_params=_params(("parallel", "parallel")),
        name="adaln_mod",
    )(c_pad, ada_w.astype(BF16), ada_b.reshape(depth, 1, n))
    return out[:, :b]


def _modnorm_matmul_kernel(x_ref, g_ref, sc_ref, sh_ref, w_ref, o_ref, h_scr):
    @pl.when(pl.program_id(1) == 0)
    def _():
        y = _rms(x_ref[...].astype(F32), g_ref[...])
        h_scr[...] = (y * (1.0 + sc_ref[...]) + sh_ref[...]).astype(BF16)

    o_ref[...] = _dot(h_scr[...], w_ref[...]).astype(o_ref.dtype)


def _pad_and_tile(n, max_tile=1024, min_tile=512):
    n_pad = -(-n // LANES) * LANES
    while True:
        if n_pad <= max_tile:
            return n_pad, n_pad
        for tn in range(max_tile, min_tile - 1, -LANES):
            if n_pad % tn == 0:
                return n_pad, tn
        n_pad += LANES


def _modnorm_matmul(x, gain, sc, sh, w, seq, out_dtype=F32, tm=512):
    t, d = x.shape
    n = w.shape[1]
    n_pad, tn = _pad_and_tile(n)
    w = w.astype(BF16)
    if n_pad != n:
        w = jnp.pad(w, ((0, 0), (0, n_pad - n)))
    tpb = seq // tm
    b = sc.shape[0]
    return pl.pallas_call(
        _modnorm_matmul_kernel,
        out_shape=jax.ShapeDtypeStruct((t, n_pad), out_dtype),
        grid=(t // tm, n_pad // tn),
        in_specs=[
            pl.BlockSpec((tm, d), lambda i, j: (i, 0)),
            pl.BlockSpec((1, d), lambda i, j: (0, 0)),
            pl.BlockSpec((None, 1, d), lambda i, j: (i // tpb, 0, 0)),
            pl.BlockSpec((None, 1, d), lambda i, j: (i // tpb, 0, 0)),
            pl.BlockSpec((d, tn), lambda i, j: (0, j)),
        ],
        out_specs=pl.BlockSpec((tm, tn), lambda i, j: (i, j)),
        scratch_shapes=[pltpu.VMEM((tm, d), BF16)],
        compiler_params=_params(("parallel", "arbitrary")),
        name="modnorm_matmul",
    )(x, gain.reshape(1, d), sc.reshape(b, 1, d), sh.reshape(b, 1, d), w)


def _norm_matmul(x, gain, w, out_dtype=F32):
    t, d = x.shape
    zeros = jnp.zeros((1, d), F32)
    return _modnorm_matmul(x, gain, zeros, zeros, w, seq=t, out_dtype=out_dtype)


def _out_proj_kernel(y_ref, w_ref, x_ref, gate_ref, o_ref):
    o_ref[...] = x_ref[...] + gate_ref[...] * _dot(y_ref[...], w_ref[...])


def _out_proj(y, w, x, gate, seq, tm=512):
    t, k = y.shape
    d = w.shape[1]
    b = gate.shape[0]
    tpb = seq // tm
    return pl.pallas_call(
        _out_proj_kernel,
        out_shape=jax.ShapeDtypeStruct((t, d), F32),
        grid=(t // tm,),
        in_specs=[
            pl.BlockSpec((tm, k), lambda i: (i, 0)),
            pl.BlockSpec((k, d), lambda i: (0, 0)),
            pl.BlockSpec((tm, d), lambda i: (i, 0)),
            pl.BlockSpec((None, 1, d), lambda i: (i // tpb, 0, 0)),
        ],
        out_specs=pl.BlockSpec((tm, d), lambda i: (i, 0)),
        compiler_params=_params(("parallel",)),
        name="out_proj",
    )(y, w.astype(BF16), x, gate.reshape(b, 1, d))


def _ffn_kernel(x_ref, g_ref, sc_ref, sh_ref, wa_ref, wb_ref, wo_ref, gate_ref, o_ref, h_scr, acc_scr):
    j = pl.program_id(1)

    @pl.when(j == 0)
    def _():
        y = _rms(x_ref[...], g_ref[...])
        h_scr[...] = (y * (1.0 + sc_ref[...]) + sh_ref[...]).astype(BF16)
        acc_scr[...] = jnp.zeros_like(acc_scr)

    h = h_scr[...]
    a = _dot(h, wa_ref[...])
    b = _dot(h, wb_ref[...])
    u = (_silu(a) * b).astype(BF16)
    acc_scr[...] += _dot(u, wo_ref[...])

    @pl.when(j == pl.num_programs(1) - 1)
    def _():
        o_ref[...] = x_ref[...] + gate_ref[...] * acc_scr[...]


def _ffn(x, gain, sc, sh, gate, w_in, w_out, seq, tm=512):
    t, d = x.shape
    hid = w_out.shape[0]
    n_h = 2
    th = hid // n_h
    assert th % LANES == 0
    b = sc.shape[0]
    tpb = seq // tm
    w_in = w_in.astype(BF16)
    w_out = w_out.astype(BF16)
    return pl.pallas_call(
        _ffn_kernel,
        out_shape=jax.ShapeDtypeStruct((t, d), F32),
        grid=(t // tm, n_h),
        in_specs=[
            pl.BlockSpec((tm, d), lambda i, j: (i, 0)),
            pl.BlockSpec((1, d), lambda i, j: (0, 0)),
            pl.BlockSpec((None, 1, d), lambda i, j: (i // tpb, 0, 0)),
            pl.BlockSpec((None, 1, d), lambda i, j: (i // tpb, 0, 0)),
            pl.BlockSpec((d, th), lambda i, j: (0, j)),
            pl.BlockSpec((d, th), lambda i, j: (0, n_h + j)),
            pl.BlockSpec((th, d), lambda i, j: (j, 0)),
            pl.BlockSpec((None, 1, d), lambda i, j: (i // tpb, 0, 0)),
        ],
        out_specs=pl.BlockSpec((tm, d), lambda i, j: (i, 0)),
        scratch_shapes=[pltpu.VMEM((tm, d), BF16), pltpu.VMEM((tm, d), F32)],
        compiler_params=_params(("parallel", "arbitrary")),
        name="ffn",
    )(x, gain.reshape(1, d), sc.reshape(b, 1, d), sh.reshape(b, 1, d), w_in, w_in, w_out,
      gate.reshape(b, 1, d))


def _norm_rope_kernel(*refs, scale, rope):
    if rope:
        x_ref, g_ref, c_ref, s_ref, p_ref, o_ref = refs
    else:
        x_ref, g_ref, o_ref = refs
    y = _rms(x_ref[...].astype(F32), g_ref[...])
    if rope:
        hi = y.astype(BF16)
        lo = (y - hi.astype(F32)).astype(BF16)
        rot = _dot(hi, p_ref[...]) + _dot(lo, p_ref[...])
        y = y * c_ref[...] + rot * s_ref[...]
    if scale != 1.0:
        y = y * scale
    o_ref[...] = y.astype(o_ref.dtype)


def _norm_rope(x, gain, cos_t=None, sin_t=None, perm=None, scale=1.0, out_dtype=BF16, ts=2048):
    b, h, s, d = x.shape
    ts = min(ts, s)
    rope = cos_t is not None
    in_specs = [
        pl.BlockSpec((None, None, ts, d), lambda bi, hi, si: (bi, hi, si, 0)),
        pl.BlockSpec((1, d), lambda bi, hi, si: (0, 0)),
    ]
    args = [x, gain.reshape(1, d)]
    if rope:
        in_specs += [
            pl.BlockSpec((None, ts, d), lambda bi, hi, si: (bi, si, 0)),
            pl.BlockSpec((None, ts, d), lambda bi, hi, si: (bi, si, 0)),
            pl.BlockSpec((d, d), lambda bi, hi, si: (0, 0)),
        ]
        args += [cos_t, sin_t, perm]
    return pl.pallas_call(
        functools.partial(_norm_rope_kernel, scale=scale, rope=rope),
        out_shape=jax.ShapeDtypeStruct((b, h, s, d), out_dtype),
        grid=(b, h, s // ts),
        in_specs=in_specs,
        out_specs=pl.BlockSpec((None, None, ts, d), lambda bi, hi, si: (bi, hi, si, 0)),
        compiler_params=_params(("parallel", "parallel", "parallel")),
        name="head_norm_rope" if rope else "head_norm",
    )(*args)


def _rope_tables(pos, d):
    inv = ROPE_THETA ** (-jnp.arange(0, d, 2, dtype=F32) / d)
    ang = pos.astype(F32)[..., None] * inv
    return jnp.cos(ang), jnp.sin(ang)


def _rotate_half_perm(d_total, start, half):
    p = np.zeros((d_total, d_total), np.float32)
    for i in range(half):
        p[start + half + i, start + i] = -1.0
        p[start + i, start + half + i] = 1.0
    return jnp.asarray(p, BF16)


V_ONES_ROWS = 16
FLASH_STRIP = 256


def _flash_scratch(tk, nq, dv):
    return [pltpu.VMEM((2, tk, nq), F32), pltpu.VMEM((1, nq), F32), pltpu.VMEM((1, nq), F32),
            pltpu.VMEM((dv + V_ONES_ROWS, nq), F32)]


def _flash_causal_t(get_q_t, k_ref, vt_ref, scratch, n_full, tk, last_mask):
    s_scr, m_scr, cm_scr, acc_scr = scratch
    dv = acc_scr.shape[0] - V_ONES_ROWS
    strips = [slice(r * FLASH_STRIP, (r + 1) * FLASH_STRIP) for r in range(tk // FLASH_STRIP)]

    def scores(j, slot):
        cm = None
        for r, rows in enumerate(strips):
            start = pl.multiple_of(j * tk + r * FLASH_STRIP, FLASH_STRIP)
            s = _dot(k_ref[pl.ds(start, FLASH_STRIP), :], get_q_t(j))
            s_scr[slot, rows, :] = s
            c = jnp.max(s, axis=0, keepdims=True)
            cm = c if cm is None else jnp.maximum(cm, c)
        return cm

    def stage(j, cur, nxt):
        cm_next = scores(j + 1, nxt)
        m = m_scr[...]
        m_new = jnp.maximum(m, cm_scr[...])
        pv = None
        for rows in strips:
            p = jnp.exp(s_scr[cur, rows, :] - m_new).astype(BF16)
            d = _dot(vt_ref[j, :, rows], p)
            pv = d if pv is None else pv + d
        acc_scr[...] = jnp.exp(m - m_new) * acc_scr[...] + pv
        m_scr[...] = m_new
        cm_scr[...] = cm_next

    m_scr[...] = jnp.full(m_scr.shape, -jnp.inf, F32)
    acc_scr[...] = jnp.zeros(acc_scr.shape, F32)
    cm_scr[...] = scores(0, 0)

    def pair(i, carry):
        stage(2 * i, 0, 1)
        stage(2 * i + 1, 1, 0)
        return carry

    lax.fori_loop(0, n_full // 2, pair, 0)

    @pl.when(n_full % 2 == 1)
    def _():
        stage(n_full - 1, 0, 1)

    s = jnp.where(last_mask, s_scr[n_full % 2], -jnp.inf)
    m = m_scr[...]
    m_new = jnp.maximum(m, jnp.max(s, axis=0, keepdims=True))
    p = jnp.exp(s - m_new).astype(BF16)
    acc = jnp.exp(m - m_new) * acc_scr[...] + _dot(vt_ref[n_full], p)
    return acc[0:dv, :] / acc[dv:dv + 1, :]


def _with_ones_rows_t(v, tk):
    *lead, s, dv = v.shape
    extra = jnp.zeros((*lead, s, V_ONES_ROWS), v.dtype).at[..., 0].set(1.0)
    va = jnp.concatenate([v, extra], axis=-1).astype(BF16)
    va = va.reshape(*lead, s // tk, tk, dv + V_ONES_ROWS)
    return jnp.swapaxes(va, -1, -2)


def _mla_attn_kernel(qt_ref, k_ref, vt_ref, o_ref, *scratch, tk):
    qi = pl.program_id(2)
    tq = qt_ref.shape[1]
    kpos = lax.broadcasted_iota(jnp.int32, (tk, tq), 0)
    qpos = lax.broadcasted_iota(jnp.int32, (tk, tq), 1)
    o = _flash_causal_t(lambda j: qt_ref[...], k_ref, vt_ref, scratch, qi, tk, kpos <= qpos)
    o_ref[...] = o.astype(o_ref.dtype)


def _mla_attention(q_t, k, v_t, tq, tk):
    b, h, dq, s = q_t.shape
    nk, dva = v_t.shape[2], v_t.shape[3]
    dv = dva - V_ONES_ROWS
    assert tq == tk
    return pl.pallas_call(
        functools.partial(_mla_attn_kernel, tk=tk),
        out_shape=jax.ShapeDtypeStruct((b, h, dv, s), BF16),
        grid=(b, h, s // tq),
        in_specs=[
            pl.BlockSpec((None, None, dq, tq), lambda bi, hi, qi: (bi, hi, 0, qi)),
            pl.BlockSpec((None, None, s, dq), lambda bi, hi, qi: (bi, hi, 0, 0)),
            pl.BlockSpec((None, None, nk, dva, tk), lambda bi, hi, qi: (bi, hi, 0, 0, 0)),
        ],
        out_specs=pl.BlockSpec((None, None, dv, tq), lambda bi, hi, qi: (bi, hi, 0, qi)),
        scratch_shapes=_flash_scratch(tk, tq, dv),
        compiler_params=_params(("parallel", "parallel", "arbitrary")),
        name="mla_flash",
    )(q_t, k, v_t)


def _compress_kernel(x_ref, pelo_ref, pehi_ref, w1lo_ref, w1hi_ref, w2_ref, g_ref, o_ref, *, norm):
    x = x_ref[...]
    nc = x.shape[0]
    a = _dot((x + pelo_ref[...]).astype(BF16), w1lo_ref[...])
    b = _dot((x + pehi_ref[...]).astype(BF16), w1hi_ref[...])
    hid = _silu(a + pltpu.roll(b, shift=nc - 1, axis=0))
    y = _dot(hid.astype(BF16), w2_ref[...])
    if norm:
        y = _rms(y, g_ref[...])
    o_ref[...] = y.astype(o_ref.dtype)


def _compress(x, pe, w1, w2, gain, norm):
    b, g, nc, wdt = x.shape
    dh = w2.shape[1]
    hidden = w1.shape[1]
    half = NSA_CMP_STRIDE
    pelo = pe[:half].reshape(1, wdt)
    pehi = pe[half:].reshape(1, wdt)
    w1 = w1.astype(BF16)
    const = lambda bi, gi: (0, 0)
    return pl.pallas_call(
        functools.partial(_compress_kernel, norm=norm),
        out_shape=jax.ShapeDtypeStruct((b, g, nc, dh), BF16),
        grid=(b, g),
        in_specs=[
            pl.BlockSpec((None, None, nc, wdt), lambda bi, gi: (bi, gi, 0, 0)),
            pl.BlockSpec((1, wdt), const),
            pl.BlockSpec((1, wdt), const),
            pl.BlockSpec((wdt, hidden), const),
            pl.BlockSpec((wdt, hidden), const),
            pl.BlockSpec((hidden, dh), const),
            pl.BlockSpec((1, dh), const),
        ],
        out_specs=pl.BlockSpec((None, None, nc, dh), lambda bi, gi: (bi, gi, 0, 0)),
        compiler_params=_params(("parallel", "parallel")),
        name="nsa_compress",
    )(x, pelo, pehi, w1[:wdt], w1[wdt:], w2.astype(BF16), gain.reshape(1, dh))


def _to_token_major(o_t, n_rep, tq):
    stacked = jnp.concatenate([o_t[:, r * tq:(r + 1) * tq] for r in range(n_rep)], axis=0)
    return stacked.T


def _nsa_cmp_kernel(qt_ref, kc_ref, vct_ref, band_ref, o_ref, sb_ref, s_scr, *, tq, n_rep):
    qi = pl.program_id(2)
    q0 = qi * tq
    nq = n_rep * tq
    n_ch, ch = s_scr.shape[0], s_scr.shape[1]
    dh = qt_ref.shape[0]
    nb = band_ref.shape[1]
    q_t = qt_ref[...]
    n_vis = (q0 + tq - NSA_CMP_STRIDE) // NSA_CMP_STRIDE
    n_used = (n_vis + ch - 1) // ch
    tpos = q0 + (lax.broadcasted_iota(jnp.int32, (ch, nq), 1) & (tq - 1))
    end0 = lax.broadcasted_iota(jnp.int32, (ch, nq), 0) * NSA_CMP_STRIDE + (NSA_CMP_LEN - 1)

    def scores(c, cm):
        s = _dot(kc_ref[pl.ds(pl.multiple_of(c * ch, ch), ch), :], q_t)
        s = jnp.where(end0 + c * (ch * NSA_CMP_STRIDE) <= tpos, s, -jnp.inf)
        s_scr[c] = s
        return jnp.maximum(cm, jnp.max(s, axis=0, keepdims=True))

    m = lax.fori_loop(0, n_used, scores, jnp.full((1, nq), -jnp.inf, F32))
    m = jnp.where(m > -jnp.inf, m, 0.0)

    def exps(c, l):
        p = jnp.exp(s_scr[c] - m)
        s_scr[c] = p
        return l + jnp.sum(p, axis=0, keepdims=True)

    l = lax.fori_loop(0, n_used, exps, jnp.zeros((1, nq), F32))
    inv = 1.0 / jnp.maximum(l, 1e-30)

    def outputs(c, carry):
        o_t, imp = carry
        p = s_scr[c] * inv
        o_t = o_t + _dot(vct_ref[c], p.astype(BF16))
        psum = p[:, 0:tq]
        for r in range(1, n_rep):
            psum = psum + p[:, r * tq:(r + 1) * tq]
        return o_t, imp + _dot3_left(band_ref[c], psum)

    o_t, imp = lax.fori_loop(0, n_used, outputs, (jnp.zeros((dh, nq), F32), jnp.zeros((nb, tq), F32)))
    o_ref[...] = _to_token_major(o_t, n_rep, tq).astype(o_ref.dtype)

    blk = lax.broadcasted_iota(jnp.int32, (nb, tq), 0)
    cur = (q0 + lax.broadcasted_iota(jnp.int32, (nb, tq), 1)) // NSA_SEL_LEN
    causal = blk <= cur
    forced = causal & ((blk == 0) | (blk > cur - NSA_N_LOCAL))
    score = jnp.where(forced, 1e30, jnp.where(causal, imp, -1.0))
    blk_f = blk.astype(F32)
    bias = jnp.full((nb, tq), SEL_MASK_BIAS, F32)
    for _ in range(NSA_N_SEL):
        best = jnp.max(score, axis=0, keepdims=True)
        idx = jnp.min(jnp.where(score == best, blk_f, float(nb)), axis=0, keepdims=True)
        pick = (blk_f == idx) & (best >= 0.0)
        bias = jnp.where(pick, 0.0, bias)
        score = jnp.where(pick, -1.0, score)
    sb_ref[...] = bias.astype(sb_ref.dtype)


NSA_CMP_CHUNK = 256


def _nsa_cmp_select(q_t, k_cmp, v_cmp, band, tq):
    b, g, n_q, dh, nq = q_t.shape
    r = nq // tq
    nc = k_cmp.shape[2]
    nb = band.shape[0]
    ch = min(NSA_CMP_CHUNK, nc)
    n_ch = nc // ch
    vc_t = jnp.swapaxes(v_cmp.reshape(b, g, n_ch, ch, dh), -1, -2)
    band_c = band.reshape(nb, n_ch, ch).transpose(1, 0, 2)
    return pl.pallas_call(
        functools.partial(_nsa_cmp_kernel, tq=tq, n_rep=r),
        out_shape=(jax.ShapeDtypeStruct((b * n_q * tq, g * r * dh), BF16),
                   jax.ShapeDtypeStruct((b, g, nb, n_q * tq), BF16)),
        grid=(b, g, n_q),
        in_specs=[
            pl.BlockSpec((None, None, None, dh, nq), lambda bi, gi, qi: (bi, gi, qi, 0, 0)),
            pl.BlockSpec((None, None, nc, dh), lambda bi, gi, qi: (bi, gi, 0, 0)),
            pl.BlockSpec((None, None, n_ch, dh, ch), lambda bi, gi, qi: (bi, gi, 0, 0, 0)),
            pl.BlockSpec((n_ch, nb, ch), lambda bi, gi, qi: (0, 0, 0)),
        ],
        out_specs=(pl.BlockSpec((tq, r * dh), lambda bi, gi, qi: (bi * n_q + qi, gi)),
                   pl.BlockSpec((None, None, nb, tq), lambda bi, gi, qi: (bi, gi, 0, qi))),
        scratch_shapes=[pltpu.VMEM((n_ch, ch, nq), F32)],
        compiler_params=_params(("parallel", "parallel", "parallel")),
        name="nsa_cmp_select",
    )(q_t, k_cmp, vc_t, band_c)


def _nsa_sel_kernel(qt_ref, sb_ref, k_ref, vt_ref, o_ref, qa_scr, *scratch, tq, tk, n_rep, tiles_per_half):
    qi = pl.program_id(2)
    q0 = qi * tq
    nq = n_rep * tq
    n_half = qa_scr.shape[0]
    q_t = qt_ref[...]
    for hh in range(n_half):
        sb = sb_ref[hh * SEL_ONEHOT:(hh + 1) * SEL_ONEHOT, :]
        qa_scr[hh] = jnp.concatenate([jnp.concatenate([sb] * n_rep, axis=1), q_t], axis=0)
    jd = q0 // tk
    kpos = jd * tk + lax.broadcasted_iota(jnp.int32, (tk, nq), 0)
    tpos = q0 + (lax.broadcasted_iota(jnp.int32, (tk, nq), 1) & (tq - 1))
    o = _flash_causal_t(lambda j: qa_scr[j // tiles_per_half], k_ref, vt_ref, scratch, jd, tk, kpos <= tpos)
    o_ref[...] = _to_token_major(o, n_rep, tq).astype(o_ref.dtype)


def _nsa_sel_attention(q_t, sel_bias, k_aug, v_t, tq, tk):
    b, g, n_q, dh, nq = q_t.shape
    nb, s = sel_bias.shape[2], sel_bias.shape[3]
    n_half = nb // SEL_ONEHOT
    ka = k_aug.shape[-1]
    nk, dva = v_t.shape[2], v_t.shape[3]
    tiles_per_half = SEL_ONEHOT * NSA_SEL_LEN // tk
    return pl.pallas_call(
        functools.partial(_nsa_sel_kernel, tq=tq, tk=tk, n_rep=nq // tq, tiles_per_half=tiles_per_half),
        out_shape=jax.ShapeDtypeStruct((b * s, g * (nq // tq) * dh), BF16),
        grid=(b, g, n_q),
        in_specs=[
            pl.BlockSpec((None, None, None, dh, nq), lambda bi, gi, qi: (bi, gi, qi, 0, 0)),
            pl.BlockSpec((None, None, nb, tq), lambda bi, gi, qi: (bi, gi, 0, qi)),
            pl.BlockSpec((None, None, s, ka), lambda bi, gi, qi: (bi, gi, 0, 0)),
            pl.BlockSpec((None, None, nk, dva, tk), lambda bi, gi, qi: (bi, gi, 0, 0, 0)),
        ],
        out_specs=pl.BlockSpec((tq, (nq // tq) * dh), lambda bi, gi, qi: (bi * n_q + qi, gi)),
        scratch_shapes=[pltpu.VMEM((n_half, ka, nq), BF16)] + _flash_scratch(tk, nq, dva - V_ONES_ROWS),
        compiler_params=_params(("parallel", "parallel", "arbitrary")),
        name="nsa_sel_flash",
    )(q_t, sel_bias, k_aug, v_t)


def _nsa_win_kernel(qt_ref, k_ref, vt_ref, o_ref, *, tq, win, n_rep):
    qi = pl.program_id(2)
    q0 = qi * tq
    nq = n_rep * tq
    dv = vt_ref.shape[-2] - V_ONES_ROWS
    n_chunk = (tq + win) // tq
    q_t = qt_ref[...]
    tpos = q0 + (lax.broadcasted_iota(jnp.int32, (tq, nq), 1) & (tq - 1))
    krow = lax.broadcasted_iota(jnp.int32, (tq, nq), 0)
    ss = []
    m = None
    for c in range(n_chunk):
        s = _dot(k_ref[pl.ds(pl.multiple_of((qi + c) * tq, tq), tq), :], q_t)
        kpos = q0 - win + c * tq + krow
        if c < n_chunk - 1:
            mask = (kpos > tpos - win) & (kpos >= 0)
        else:
            mask = kpos <= tpos
        s = jnp.where(mask, s, -jnp.inf)
        ss.append(s)
        cm = jnp.max(s, axis=0, keepdims=True)
        m = cm if m is None else jnp.maximum(m, cm)
    acc = None
    for c in range(n_chunk):
        d = _dot(vt_ref[qi + c], jnp.exp(ss[c] - m).astype(BF16))
        acc = d if acc is None else acc + d
    o_t = acc[0:dv, :] / acc[dv:dv + 1, :]
    o_ref[...] = _to_token_major(o_t, n_rep, tq).astype(o_ref.dtype)


def _nsa_win_attention(q_t, k_pad, vt_pad, tq):
    b, g, n_q, dh, nq = q_t.shape
    sp = k_pad.shape[2]
    nkc, dva = vt_pad.shape[2], vt_pad.shape[3]
    r = nq // tq
    return pl.pallas_call(
        functools.partial(_nsa_win_kernel, tq=tq, win=NSA_WINDOW, n_rep=r),
        out_shape=jax.ShapeDtypeStruct((b * n_q * tq, g * r * dh), BF16),
        grid=(b, g, n_q),
        in_specs=[
            pl.BlockSpec((None, None, None, dh, nq), lambda bi, gi, qi: (bi, gi, qi, 0, 0)),
            pl.BlockSpec((None, None, sp, dh), lambda bi, gi, qi: (bi, gi, 0, 0)),
            pl.BlockSpec((None, None, nkc, dva, tq), lambda bi, gi, qi: (bi, gi, 0, 0, 0)),
        ],
        out_specs=pl.BlockSpec((tq, r * dh), lambda bi, gi, qi: (bi * n_q + qi, gi)),
        compiler_params=_params(("parallel", "parallel", "arbitrary")),
        name="nsa_window",
    )(q_t, k_pad, vt_pad)


def _nsa_out_kernel(oc_ref, os_ref, ow_ref, gl_ref, e_ref, w_ref, x_ref, gate_ref, o_ref):
    sg = _sigmoid(gl_ref[...].astype(F32))
    hi = sg.astype(BF16)
    lo = (sg - hi.astype(F32)).astype(BF16)

    def expand(i):
        return _dot(hi, e_ref[i]) + _dot(lo, e_ref[i])

    o = expand(0) * oc_ref[...] + expand(1) * os_ref[...] + expand(2) * ow_ref[...]
    o_ref[...] = x_ref[...] + gate_ref[...] * _dot(o.astype(BF16), w_ref[...])


def _nsa_out(o_cmp, o_sel, o_win, gate_logits, expand, w, x, gate, seq, tm=512):
    t, d = x.shape
    k = o_cmp.shape[1]
    gw = gate_logits.shape[1]
    b = gate.shape[0]
    tpb = seq // tm
    row = lambda i: (i, 0)
    return pl.pallas_call(
        _nsa_out_kernel,
        out_shape=jax.ShapeDtypeStruct((t, d), F32),
        grid=(t // tm,),
        in_specs=[
            pl.BlockSpec((tm, k), row),
            pl.BlockSpec((tm, k), row),
            pl.BlockSpec((tm, k), row),
            pl.BlockSpec((tm, gw), row),
            pl.BlockSpec((3, gw, k), lambda i: (0, 0, 0)),
            pl.BlockSpec((k, d), lambda i: (0, 0)),
            pl.BlockSpec((tm, d), row),
            pl.BlockSpec((None, 1, d), lambda i: (i // tpb, 0, 0)),
        ],
        out_specs=pl.BlockSpec((tm, d), row),
        compiler_params=_params(("parallel",)),
        name="nsa_out",
    )(o_cmp, o_sel, o_win, gate_logits, expand, w.astype(BF16), x, gate.reshape(b, 1, d))


def _ret_kernel(lg_ref, q_ref, k_ref, v_ref, g_ref, cos_ref, sin_ref, gn_ref, o_ref, st_scr):
    h = pl.program_id(1)
    c = pl.program_id(2)

    @pl.when(c == 0)
    def _():
        st_scr[...] = jnp.zeros_like(st_scr)

    cn = q_ref.shape[0]
    dk = q_ref.shape[1]
    half = dk // 2
    lg = jnp.full((1, 1), lg_ref[h], F32)
    cos = cos_ref[...]
    sin = sin_ref[...]

    def rope(x):
        x1, x2 = x[:, :half], x[:, half:]
        return jnp.concatenate([x1 * cos - x2 * sin, x1 * sin + x2 * cos], axis=1)

    q = rope(q_ref[...].astype(F32))
    k = rope(k_ref[...].astype(F32)) * (dk ** -0.5)
    v = v_ref[...].astype(BF16)
    n = lax.broadcasted_iota(jnp.int32, (cn, 1), 0).astype(F32)
    rel = (lax.broadcasted_iota(jnp.int32, (cn, cn), 0) - lax.broadcasted_iota(jnp.int32, (cn, cn), 1)).astype(F32)
    decay = jnp.where(rel >= 0, jnp.exp(jnp.maximum(rel, 0.0) * lg), 0.0)
    xi = jnp.exp((n + 1.0) * lg)
    zeta = jnp.exp((cn - 1.0 - n) * lg)
    gamma_c = jnp.exp(cn * lg)

    qb = q.astype(BF16)
    inner = _dot((_dot_nt(qb, k.astype(BF16)) * decay).astype(BF16), v)
    st = st_scr[...]
    cross = _dot(qb, st.astype(BF16)) * xi
    kz_t = (k * zeta).T.astype(BF16)
    st_scr[...] = gamma_c * st + _dot(kz_t, v)
    y = inner + cross
    mu = jnp.mean(y, axis=-1, keepdims=True)
    yc = y - mu
    var = jnp.mean(yc * yc, axis=-1, keepdims=True)
    yn = yc * lax.rsqrt(var + NORM_EPS) * gn_ref[...]
    o_ref[...] = (_silu(g_ref[...].astype(F32)) * yn).astype(o_ref.dtype)


def _retention(proj, cos_t, sin_t, log_gamma, gn_w, batch, seq, heads):
    t = proj.shape[0]
    dk, dv, cn = RET_QK, RET_V, RET_CHUNK
    n_ch = seq // cn
    kb = heads * dk // dk
    vb = 2 * heads * dk // dv
    row = lambda bi, hi, ci, *_: bi * n_ch + ci
    grid_spec = pltpu.PrefetchScalarGridSpec(
        num_scalar_prefetch=1,
        grid=(batch, heads, n_ch),
        in_specs=[
            pl.BlockSpec((cn, dk), lambda bi, hi, ci, lg: (row(bi, hi, ci), hi)),
            pl.BlockSpec((cn, dk), lambda bi, hi, ci, lg: (row(bi, hi, ci), kb + hi)),
            pl.BlockSpec((cn, dv), lambda bi, hi, ci, lg: (row(bi, hi, ci), vb + hi)),
            pl.BlockSpec((cn, dv), lambda bi, hi, ci, lg: (row(bi, hi, ci), vb + heads + hi)),
            pl.BlockSpec((cn, dk // 2), lambda bi, hi, ci, lg: (row(bi, hi, ci), 0)),
            pl.BlockSpec((cn, dk // 2), lambda bi, hi, ci, lg: (row(bi, hi, ci), 0)),
            pl.BlockSpec((1, dv), lambda bi, hi, ci, lg: (0, hi)),
        ],
        out_specs=pl.BlockSpec((cn, dv), lambda bi, hi, ci, lg: (row(bi, hi, ci), hi)),
        scratch_shapes=[pltpu.VMEM((dk, dv), F32)],
    )
    return pl.pallas_call(
        _ret_kernel,
        out_shape=jax.ShapeDtypeStruct((t, heads * dv), BF16),
        grid_spec=grid_spec,
        compiler_params=_params(("parallel", "parallel", "arbitrary")),
        name="retention",
    )(log_gamma, proj, proj, proj, proj, cos_t, sin_t, gn_w.reshape(1, heads * dv))


def _softplus(x):
    return jnp.maximum(x, 0.0) + jnp.log1p(jnp.exp(-jnp.abs(x)))


def _ssd_kernel(row_ref, dt_ref, dtt_ref, cw_ref, cb_ref, dtb_ref, dtbt_ref, al_ref, alt_ref, dsk_ref, nw_ref,
                ex_ref, o_ref, buf_scr, st_scr, *, inner, groups, state, heads):
    c = pl.program_id(1)
    cn = row_ref.shape[0]
    conv_ch = inner + 2 * groups * state
    hp = inner // heads
    rep = heads // groups
    gw = rep * hp
    halo = 8

    @pl.when(c == 0)
    def _():
        buf_scr[0:halo, :] = jnp.zeros((halo, conv_ch), F32)
        st_scr[...] = jnp.zeros_like(st_scr)

    z = row_ref[:, 0:inner].astype(F32)
    xbc = row_ref[:, inner:inner + conv_ch].astype(F32)
    dt_raw = dt_ref[...]

    buf_scr[halo:halo + cn, :] = xbc
    conv = cb_ref[...] + cw_ref[0:1, :] * buf_scr[halo - 3:halo - 3 + cn, :]
    for kk in range(1, SSD_CONV):
        conv = conv + cw_ref[kk:kk + 1, :] * buf_scr[halo - 3 + kk:halo - 3 + kk + cn, :]
    buf_scr[0:halo, :] = xbc[cn - halo:cn, :]
    act = _silu(conv)
    xs = act[:, 0:inner]
    bm = act[:, inner:inner + groups * state]
    cm = act[:, inner + groups * state:conv_ch]

    dt = _softplus(dt_raw + dtb_ref[...])
    dt_t = _softplus(dtt_ref[...] + dtbt_ref[...])
    a = -jnp.exp(al_ref[...])
    a_t = -jnp.exp(alt_ref[...])
    ri = lax.broadcasted_iota(jnp.int32, (cn, cn), 0)
    ci = lax.broadcasted_iota(jnp.int32, (cn, cn), 1)
    tril = ci <= ri
    lower = jnp.where(tril, 1.0, 0.0).astype(BF16)
    upper = jnp.where(ci >= ri, 1.0, 0.0).astype(BF16)
    cum = _dot3_left(lower, dt * a)
    cum_t = _dot3(dt_t * a_t, upper)
    cum_last = cum[cn - 1:cn, :]

    ex = ex_ref[...]
    e_cum = _dot3(jnp.exp(cum), ex)
    w_end = _dot3(jnp.exp(cum_last - cum) * dt, ex)
    e_last = _dot3(jnp.exp(cum_last), ex)

    lane = lax.broadcasted_iota(jnp.int32, (cn, 2 * hp), 1)
    ys = []
    for g in range(groups):
        cc = cm[:, g * state:(g + 1) * state].astype(BF16)
        bc = bm[:, g * state:(g + 1) * state].astype(BF16)
        cb = _dot_nt(cc, bc)
        xg = xs[:, g * gw:(g + 1) * gw]
        st = st_scr[g]
        cross = _dot(cc, st.astype(BF16)) * e_cum[:, g * gw:(g + 1) * gw]
        xw_t = bc.astype(F32).T.astype(BF16)
        st_scr[g] = e_last[:, g * gw:(g + 1) * gw] * st + _dot(
            xw_t, (xg * w_end[:, g * gw:(g + 1) * gw]).astype(BF16))
        intra = []
        for pr in range(rep // 2):
            ws = []
            for hh in (2 * pr, 2 * pr + 1):
                hd = g * rep + hh
                seg = jnp.where(tril, cum[:, hd:hd + 1] - cum_t[hd:hd + 1, :], -jnp.inf)
                ws.append((jnp.exp(seg) * cb * dt_t[hd:hd + 1, :]).astype(BF16))
            w2 = jnp.concatenate(ws, axis=1)
            xp = xg[:, 2 * pr * hp:(2 * pr + 2) * hp]
            x2 = jnp.concatenate([jnp.where(lane < hp, xp, 0.0), jnp.where(lane >= hp, xp, 0.0)], axis=0)
            intra.append(_dot(w2, x2.astype(BF16)))
        ys.append(jnp.concatenate(intra, axis=1) + cross)
    y = jnp.concatenate(ys, axis=1) + dsk_ref[...] * xs
    y = y * _silu(z)
    outs = []
    for g in range(groups):
        outs.append(_rms(y[:, g * gw:(g + 1) * gw], nw_ref[:, g * gw:(g + 1) * gw]))
    o_ref[...] = jnp.concatenate(outs, axis=1).astype(o_ref.dtype)


def _ssd(proj, dt_raw, conv_w, conv_b, dt_bias, a_log, d_skip, norm_w, batch, seq, inner, heads):
    t, n_pad = proj.shape
    groups, state, cn = SSD_GROUPS, SSD_STATE, SSD_CHUNK
    conv_ch = inner + 2 * groups * state
    n_ch = seq // cn
    hp = inner // heads
    gw = inner // groups
    hpad = dt_raw.shape[1]
    assert heads <= hpad and n_pad >= inner + conv_ch
    dt_t = dt_raw.reshape(batch, seq, hpad).transpose(0, 2, 1)
    ex_np = np.zeros((hpad, inner), np.float32)
    ex_np[:heads] = np.kron(np.eye(heads, dtype=np.float32), np.ones((1, hp), np.float32))
    ex = jnp.asarray(ex_np, BF16)
    dt_bias = jnp.pad(dt_bias, (0, hpad - heads))
    a_log = jnp.pad(a_log, (0, hpad - heads))
    heads_k = heads
    heads = hpad
    const = lambda bi, ci: (0, 0)
    return pl.pallas_call(
        functools.partial(_ssd_kernel, inner=inner, groups=groups, state=state, heads=heads_k),
        out_shape=jax.ShapeDtypeStruct((t, inner), BF16),
        grid=(batch, n_ch),
        in_specs=[
            pl.BlockSpec((cn, n_pad), lambda bi, ci: (bi * n_ch + ci, 0)),
            pl.BlockSpec((cn, heads), lambda bi, ci: (bi * n_ch + ci, 0)),
            pl.BlockSpec((None, heads, cn), lambda bi, ci: (bi, 0, ci)),
            pl.BlockSpec((SSD_CONV, conv_ch), const),
            pl.BlockSpec((1, conv_ch), const),
            pl.BlockSpec((1, heads), const),
            pl.BlockSpec((heads, 1), const),
            pl.BlockSpec((1, heads), const),
            pl.BlockSpec((heads, 1), const),
            pl.BlockSpec((1, inner), const),
            pl.BlockSpec((1, inner), const),
            pl.BlockSpec((heads, inner), const),
        ],
        out_specs=pl.BlockSpec((cn, inner), lambda bi, ci: (bi * n_ch + ci, 0)),
        scratch_shapes=[pltpu.VMEM((8 + cn, conv_ch), F32), pltpu.VMEM((groups, state, gw), F32)],
        compiler_params=_params(("parallel", "arbitrary")),
        name="ssd_scan",
    )(proj, dt_raw, dt_t, conv_w, conv_b.reshape(1, conv_ch), dt_bias.reshape(1, heads), dt_bias.reshape(heads, 1),
      a_log.reshape(1, heads), a_log.reshape(heads, 1), jnp.repeat(d_skip, hp).reshape(1, inner),
      norm_w.reshape(1, inner), ex)


def _nsa_layer(x, mods, norm_g, batch, seq, w_in, q_norm, k_norm, cmp_pe, cmp_w1, cmp_w2, w_out):
    sh, sc, gate = mods
    d = x.shape[1]
    dh, g = HEAD_DIM, NSA_GROUPS
    heads = d // dh
    r = heads // g
    kvw = g * dh
    proj = _modnorm_matmul(x, norm_g, sc, sh, w_in, seq, out_dtype=BF16)
    off = heads * dh

    def heads_major(cols, n_heads):
        return cols.reshape(batch, seq, n_heads, dh).transpose(0, 2, 1, 3)

    q = heads_major(proj[:, :off], heads)
    parts = [proj[:, off + i * kvw: off + (i + 1) * kvw] for i in range(6)]
    k_c, v_c, k_s, v_s, k_w, v_w = parts
    gate_logits = proj[:, off + 6 * kvw: off + 6 * kvw + LANES]

    tq, tk = NSA_TQ, min(NSA_TK, seq)
    n_q = seq // tq
    qn = _norm_rope(q, q_norm, scale=dh ** -0.5)
    q_t = qn.reshape(batch, g, r, n_q, tq, dh).transpose(0, 1, 3, 5, 2, 4).reshape(batch, g, n_q, dh, r * tq)

    def chunk_rows(cols):
        st = NSA_CMP_STRIDE
        return cols.reshape(batch, seq // st, st, g, dh).transpose(0, 3, 1, 2, 4).reshape(batch, g, seq // st, st * dh)

    k_cmp = _compress(chunk_rows(k_c), cmp_pe[0], cmp_w1[0], cmp_w2[0], k_norm[0], norm=True)
    v_cmp = _compress(chunk_rows(v_c), cmp_pe[1], cmp_w1[1], cmp_w2[1], k_norm[0], norm=False)

    n_blk = seq // NSA_SEL_LEN
    n_cmp = seq // NSA_CMP_STRIDE
    assert n_blk % SEL_ONEHOT == 0
    ratio = NSA_SEL_LEN // NSA_CMP_STRIDE
    jj = np.arange(n_blk)[:, None]
    cc = np.arange(n_cmp)[None, :]
    band = jnp.asarray(((cc >= ratio * jj - 1) & (cc <= ratio * jj + ratio - 1)).astype(np.float32), BF16)
    o_cmp, sel_bias = _nsa_cmp_select(q_t, k_cmp, v_cmp, band, tq)

    ks_n = _norm_rope(heads_major(k_s, g), k_norm[1])
    onehot = jnp.asarray(np.eye(SEL_ONEHOT, dtype=np.float32)[(np.arange(seq) // NSA_SEL_LEN) % SEL_ONEHOT], BF16)
    k_aug = jnp.concatenate([jnp.broadcast_to(onehot, (batch, g, seq, SEL_ONEHOT)), ks_n], axis=-1)
    o_sel = _nsa_sel_attention(q_t, sel_bias, k_aug, _with_ones_rows_t(heads_major(v_s, g), tk), tq, tk)

    kw_n = _norm_rope(heads_major(k_w, g), k_norm[2])
    pad = ((0, 0), (0, 0), (NSA_WINDOW, 0), (0, 0))
    o_win = _nsa_win_attention(q_t, jnp.pad(kw_n, pad), _with_ones_rows_t(jnp.pad(heads_major(v_w, g), pad), tq), tq)

    e = np.zeros((3, LANES, heads * dh), np.float32)
    for hd in range(heads):
        for i in range(3):
            e[i, hd * 3 + i, hd * dh:(hd + 1) * dh] = 1.0
    return _nsa_out(o_cmp, o_sel, o_win, gate_logits, jnp.asarray(e, BF16), w_out, x, gate, seq)


def _mla_layer(x, mods, norm_g, batch, seq, pos, w_in, q_a_norm, kv_a_norm, w_q_b, w_kv_b, q_norm, k_norm, w_out):
    sh, sc, gate = mods
    d = x.shape[1]
    heads = w_out.shape[0] // MLA_V
    dq = MLA_NOPE + MLA_ROPE
    proj = _modnorm_matmul(x, norm_g, sc, sh, w_in, seq, out_dtype=BF16)
    cq = proj[:, :MLA_Q_LORA]
    ckv = proj[:, MLA_Q_LORA:MLA_Q_LORA + MLA_KV_LORA]
    k_rot = proj[:, MLA_Q_LORA + MLA_KV_LORA:MLA_Q_LORA + MLA_KV_LORA + MLA_ROPE]
    q_raw = _norm_matmul(cq, q_a_norm, w_q_b, out_dtype=BF16)
    kv_raw = _norm_matmul(ckv, kv_a_norm, w_kv_b, out_dtype=BF16)
    q_raw = q_raw.reshape(batch, seq, heads, dq).transpose(0, 2, 1, 3)
    kv_raw = kv_raw.reshape(batch, seq, heads, MLA_NOPE + MLA_V).transpose(0, 2, 1, 3)

    cos, sin = _rope_tables(pos, MLA_ROPE)
    ones = jnp.ones((batch, seq, MLA_NOPE), F32)
    cos_q = jnp.concatenate([ones, cos, cos], axis=-1)
    sin_q = jnp.concatenate([0.0 * ones, sin, sin], axis=-1)
    qn = _norm_rope(q_raw, q_norm, cos_q, sin_q, _rotate_half_perm(dq, MLA_NOPE, MLA_ROPE // 2), scale=dq ** -0.5)
    k_nope = _norm_rope(kv_raw[..., :MLA_NOPE], k_norm[:MLA_NOPE])
    k_rope = _norm_rope(k_rot.reshape(batch, 1, seq, MLA_ROPE), k_norm[MLA_NOPE:],
                        jnp.concatenate([cos, cos], axis=-1), jnp.concatenate([sin, sin], axis=-1),
                        _rotate_half_perm(MLA_ROPE, 0, MLA_ROPE // 2))
    k = jnp.concatenate([k_nope, jnp.broadcast_to(k_rope, (batch, heads, seq, MLA_ROPE))], axis=-1)
    tq = min(MLA_TQ, seq)
    tk = min(MLA_TK, tq)
    o_t = _mla_attention(jnp.swapaxes(qn, -1, -2), k, _with_ones_rows_t(kv_raw[..., MLA_NOPE:], tk), tq, tk)
    o = o_t.transpose(0, 3, 1, 2).reshape(batch * seq, heads * MLA_V).astype(BF16)
    return _out_proj(o, w_out, x, gate, seq)


def _ret_layer(x, mods, norm_g, batch, seq, pos, w_in, gn_w, w_out):
    sh, sc, gate = mods
    heads = w_out.shape[0] // RET_V
    proj = _modnorm_matmul(x, norm_g, sc, sh, w_in, seq, out_dtype=BF16)
    cos, sin = _rope_tables(pos, RET_QK)
    cos = cos.reshape(batch * seq, RET_QK // 2)
    sin = sin.reshape(batch * seq, RET_QK // 2)
    log_gamma = jnp.log1p(-(2.0 ** (-5.0 - jnp.arange(heads, dtype=F32))))
    y = _retention(proj, cos, sin, log_gamma, gn_w, batch, seq, heads)
    return _out_proj(y, w_out, x, gate, seq)


def _ssd_layer(x, mods, norm_g, batch, seq, w_in, conv_w, conv_b, dt_bias, a_log, d_skip, norm_w, w_out):
    sh, sc, gate = mods
    inner = w_out.shape[0]
    heads = dt_bias.shape[0]
    conv_ch = conv_w.shape[1]
    proj = _modnorm_matmul(x, norm_g, sc, sh, w_in[:, :inner + conv_ch], seq, out_dtype=BF16)
    dt_raw = _modnorm_matmul(x, norm_g, sc, sh, w_in[:, inner + conv_ch:], seq)
    y = _ssd(proj, dt_raw, conv_w, conv_b, dt_bias, a_log, d_skip, norm_w, batch, seq, inner, heads)
    return _out_proj(y, w_out, x, gate, seq)


def kernel(x, c, positions, ada_w, ada_b, norm_mix, norm_ffn, ffn_w_in, ffn_w_out, nsa_w_in, nsa_q_norm, nsa_k_norm, nsa_cmp_pe, nsa_cmp_w1, nsa_cmp_w2, nsa_w_out, mla_w_in, mla_q_a_norm, mla_kv_a_norm, mla_w_q_b, mla_w_kv_b, mla_q_norm, mla_k_norm, mla_w_out, ret_w_in, ret_gn_w, ret_w_out, ssd_w_in, ssd_conv_w, ssd_conv_b, ssd_dt_bias, ssd_a_log, ssd_d, ssd_norm, ssd_w_out):
    batch, seq, d = x.shape
    depth = ada_w.shape[0]
    n_mixers = 4
    mod = _modulation(c, ada_w, ada_b)
    xt = x.reshape(batch * seq, d)
    for i in range(depth):
        sh_m, sc_m, g_m, sh_f, sc_f, g_f = [mod[i, :, k * d:(k + 1) * d] for k in range(6)]
        mods = (sh_m, sc_m, g_m)
        kind, j = i % n_mixers, i // n_mixers
        if kind == 0:
            xt = _nsa_layer(xt, mods, norm_mix[i], batch, seq, nsa_w_in[j], nsa_q_norm[j], nsa_k_norm[j],
                            nsa_cmp_pe[j], nsa_cmp_w1[j], nsa_cmp_w2[j], nsa_w_out[j])
        elif kind == 1:
            xt = _mla_layer(xt, mods, norm_mix[i], batch, seq, positions, mla_w_in[j], mla_q_a_norm[j],
                            mla_kv_a_norm[j], mla_w_q_b[j], mla_w_kv_b[j], mla_q_norm[j], mla_k_norm[j], mla_w_out[j])
        elif kind == 2:
            xt = _ret_layer(xt, mods, norm_mix[i], batch, seq, positions, ret_w_in[j], ret_gn_w[j], ret_w_out[j])
        else:
            xt = _ssd_layer(xt, mods, norm_mix[i], batch, seq, ssd_w_in[j], ssd_conv_w[j], ssd_conv_b[j],
                            ssd_dt_bias[j], ssd_a_log[j], ssd_d[j], ssd_norm[j], ssd_w_out[j])
        xt = _ffn(xt, norm_ffn[i], sc_f, sh_f, g_f, ffn_w_in[i], ffn_w_out[i], seq)
    return xt.reshape(batch, seq, d)
```

```python
import functools
import math

import numpy as np
import jax
import jax.numpy as jnp
from jax import lax
from jax.experimental import pallas as pl
from jax.experimental.pallas import tpu as pltpu

F32 = jnp.float32
BF16 = jnp.bfloat16

NORM_EPS = 1e-6
ROPE_THETA = 10000.0
LANES = 128

HEAD_DIM = 64
NSA_GROUPS = 4
NSA_CMP_STRIDE = 16
NSA_CMP_LEN = 32
NSA_SEL_LEN = 64
NSA_N_SEL = 16
NSA_N_LOCAL = 2
NSA_WINDOW = 512
NSA_TQ = 128
NSA_TK = 512
SEL_ONEHOT = 128
SEL_MASK_BIAS = -32768.0

MLA_Q_LORA = 384
MLA_KV_LORA = 256
MLA_NOPE = 64
MLA_ROPE = 32
MLA_V = 64
MLA_TQ = 512
MLA_TK = 512

RET_QK = 256
RET_V = 512
RET_CHUNK = 128

SSD_HEADDIM = 64
SSD_GROUPS = 4
SSD_STATE = 128
SSD_CONV = 4
SSD_CHUNK = 128

VMEM_LIMIT = 48 * 1024 * 1024


def _params(sem, vmem=VMEM_LIMIT, flags=None):
    return pltpu.CompilerParams(dimension_semantics=sem, vmem_limit_bytes=vmem, flags=flags)


def _sigmoid(x):
    return 1.0 / (1.0 + jnp.exp(-x))


def _silu(x):
    return x * _sigmoid(x)


def _dot(a, b):
    return jnp.dot(a, b, preferred_element_type=F32)


def _dot_nt(a, b):
    return lax.dot_general(a, b, (((1,), (1,)), ((), ())), preferred_element_type=F32)


def _split3(x):
    hi = x.astype(BF16)
    r1 = x - hi.astype(F32)
    mid = r1.astype(BF16)
    lo = (r1 - mid.astype(F32)).astype(BF16)
    return hi, mid, lo


def _dot3(x, m):
    hi, mid, lo = _split3(x)
    return _dot(hi, m) + _dot(mid, m) + _dot(lo, m)


def _dot3_left(m, x):
    hi, mid, lo = _split3(x)
    return _dot(m, hi) + _dot(m, mid) + _dot(m, lo)


def _rms(x, gain):
    ms = jnp.mean(x * x, axis=-1, keepdims=True)
    return x * lax.rsqrt(ms + NORM_EPS) * gain


def _mod_kernel(c_ref, w_ref, b_ref, o_ref):
    cond = _silu(c_ref[...]).astype(BF16)
    o_ref[...] = _dot(cond, w_ref[...]) + b_ref[...]


def _modulation(c, ada_w, ada_b):
    depth, d, n = ada_w.shape
    b = c.shape[0]
    rows = 16
    c_pad = jnp.zeros((rows, d), F32).at[:b].set(c)
    tn = 1024
    out = pl.pallas_call(
        _mod_kernel,
        out_shape=jax.ShapeDtypeStruct((depth, rows, n), F32),
        grid=(depth, n // tn),
        in_specs=[
            pl.BlockSpec((rows, d), lambda l, j: (0, 0)),
            pl.BlockSpec((None, d, tn), lambda l, j: (l, 0, j)),
            pl.BlockSpec((None, 1, tn), lambda l, j: (l, 0, j)),
        ],
        out_specs=pl.BlockSpec((None, rows, tn), lambda l, j: (l, 0, j)),
        compiler_params=_params(("parallel", "parallel")),
        name="adaln_mod",
    )(c_pad, ada_w.astype(BF16), ada_b.reshape(depth, 1, n))
    return out[:, :b]


def _modnorm_matmul_kernel(x_ref, g_ref, sc_ref, sh_ref, w_ref, o_ref, h_scr):
    @pl.when(pl.program_id(1) == 0)
    def _():
        y = _rms(x_ref[...].astype(F32), g_ref[...])
        h_scr[...] = (y * (1.0 + sc_ref[...]) + sh_ref[...]).astype(BF16)

    o_ref[...] = _dot(h_scr[...], w_ref[...]).astype(o_ref.dtype)


def _pad_and_tile(n, max_tile=1024, min_tile=512):
    n_pad = -(-n // LANES) * LANES
    while True:
        if n_pad <= max_tile:
            return n_pad, n_pad
        for tn in range(max_tile, min_tile - 1, -LANES):
            if n_pad % tn == 0:
                return n_pad, tn
        n_pad += LANES


def _modnorm_matmul(x, gain, sc, sh, w, seq, out_dtype=F32, tm=1024):
    t, d = x.shape
    tm = min(tm, seq)
    n = w.shape[1]
    n_pad, tn = _pad_and_tile(n)
    w = w.astype(BF16)
    if n_pad != n:
        w = jnp.pad(w, ((0, 0), (0, n_pad - n)))
    tpb = seq // tm
    b = sc.shape[0]
    return pl.pallas_call(
        _modnorm_matmul_kernel,
        out_shape=jax.ShapeDtypeStruct((t, n_pad), out_dtype),
        grid=(t // tm, n_pad // tn),
        in_specs=[
            pl.BlockSpec((tm, d), lambda i, j: (i, 0)),
            pl.BlockSpec((1, d), lambda i, j: (0, 0)),
            pl.BlockSpec((None, 1, d), lambda i, j: (i // tpb, 0, 0)),
            pl.BlockSpec((None, 1, d), lambda i, j: (i // tpb, 0, 0)),
            pl.BlockSpec((d, tn), lambda i, j: (0, j)),
        ],
        out_specs=pl.BlockSpec((tm, tn), lambda i, j: (i, j)),
        scratch_shapes=[pltpu.VMEM((tm, d), BF16)],
        compiler_params=_params(("parallel", "arbitrary")),
        name="modnorm_matmul",
    )(x, gain.reshape(1, d), sc.reshape(b, 1, d), sh.reshape(b, 1, d), w)


def _norm_matmul(x, gain, w, out_dtype=F32):
    t, d = x.shape
    zeros = jnp.zeros((1, d), F32)
    return _modnorm_matmul(x, gain, zeros, zeros, w, seq=t, out_dtype=out_dtype)


def _out_proj_kernel(y_ref, w_ref, x_ref, gate_ref, o_ref):
    o_ref[...] = x_ref[...] + gate_ref[...] * _dot(y_ref[...], w_ref[...])


def _out_proj(y, w, x, gate, seq, tm=512):
    t, k = y.shape
    d = w.shape[1]
    b = gate.shape[0]
    tpb = seq // tm
    return pl.pallas_call(
        _out_proj_kernel,
        out_shape=jax.ShapeDtypeStruct((t, d), F32),
        grid=(t // tm,),
        in_specs=[
            pl.BlockSpec((tm, k), lambda i: (i, 0)),
            pl.BlockSpec((k, d), lambda i: (0, 0)),
            pl.BlockSpec((tm, d), lambda i: (i, 0)),
            pl.BlockSpec((None, 1, d), lambda i: (i // tpb, 0, 0)),
        ],
        out_specs=pl.BlockSpec((tm, d), lambda i: (i, 0)),
        compiler_params=_params(("parallel",)),
        name="out_proj",
    )(y, w.astype(BF16), x, gate.reshape(b, 1, d))


def _ffn_kernel(x_ref, g_ref, sc_ref, sh_ref, wa_ref, wb_ref, wo_ref, gate_ref, o_ref, h_scr, acc_scr):
    j = pl.program_id(1)

    @pl.when(j == 0)
    def _():
        y = _rms(x_ref[...], g_ref[...])
        h_scr[...] = (y * (1.0 + sc_ref[...]) + sh_ref[...]).astype(BF16)
        acc_scr[...] = jnp.zeros_like(acc_scr)

    h = h_scr[...]
    a = _dot(h, wa_ref[...])
    b = _dot(h, wb_ref[...])
    u = (_silu(a) * b).astype(BF16)
    acc_scr[...] += _dot(u, wo_ref[...])

    @pl.when(j == pl.num_programs(1) - 1)
    def _():
        o_ref[...] = x_ref[...] + gate_ref[...] * acc_scr[...]


def _ffn(x, gain, sc, sh, gate, w_in, w_out, seq, tm=512):
    t, d = x.shape
    hid = w_out.shape[0]
    n_h = 2
    th = hid // n_h
    assert th % LANES == 0
    b = sc.shape[0]
    tpb = seq // tm
    w_in = w_in.astype(BF16)
    w_out = w_out.astype(BF16)
    return pl.pallas_call(
        _ffn_kernel,
        out_shape=jax.ShapeDtypeStruct((t, d), F32),
        grid=(t // tm, n_h),
        in_specs=[
            pl.BlockSpec((tm, d), lambda i, j: (i, 0)),
            pl.BlockSpec((1, d), lambda i, j: (0, 0)),
            pl.BlockSpec((None, 1, d), lambda i, j: (i // tpb, 0, 0)),
            pl.BlockSpec((None, 1, d), lambda i, j: (i // tpb, 0, 0)),
            pl.BlockSpec((d, th), lambda i, j: (0, j)),
            pl.BlockSpec((d, th), lambda i, j: (0, n_h + j)),
            pl.BlockSpec((th, d), lambda i, j: (j, 0)),
            pl.BlockSpec((None, 1, d), lambda i, j: (i // tpb, 0, 0)),
        ],
        out_specs=pl.BlockSpec((tm, d), lambda i, j: (i, 0)),
        scratch_shapes=[pltpu.VMEM((tm, d), BF16), pltpu.VMEM((tm, d), F32)],
        compiler_params=_params(("parallel", "arbitrary")),
        name="ffn",
    )(x, gain.reshape(1, d), sc.reshape(b, 1, d), sh.reshape(b, 1, d), w_in, w_in, w_out,
      gate.reshape(b, 1, d))


def _norm_rope_kernel(*refs, scale, rope):
    if rope:
        x_ref, g_ref, c_ref, s_ref, p_ref, o_ref = refs
    else:
        x_ref, g_ref, o_ref = refs
    y = _rms(x_ref[...].astype(F32), g_ref[...])
    if rope:
        hi = y.astype(BF16)
        lo = (y - hi.astype(F32)).astype(BF16)
        rot = _dot(hi, p_ref[...]) + _dot(lo, p_ref[...])
        y = y * c_ref[...] + rot * s_ref[...]
    if scale != 1.0:
        y = y * scale
    o_ref[...] = y.astype(o_ref.dtype)


def _norm_rope(x, gain, cos_t=None, sin_t=None, perm=None, scale=1.0, out_dtype=BF16, ts=2048):
    b, h, s, d = x.shape
    ts = min(ts, s)
    rope = cos_t is not None
    in_specs = [
        pl.BlockSpec((None, None, ts, d), lambda bi, hi, si: (bi, hi, si, 0)),
        pl.BlockSpec((1, d), lambda bi, hi, si: (0, 0)),
    ]
    args = [x, gain.reshape(1, d)]
    if rope:
        in_specs += [
            pl.BlockSpec((None, ts, d), lambda bi, hi, si: (bi, si, 0)),
            pl.BlockSpec((None, ts, d), lambda bi, hi, si: (bi, si, 0)),
            pl.BlockSpec((d, d), lambda bi, hi, si: (0, 0)),
        ]
        args += [cos_t, sin_t, perm]
    return pl.pallas_call(
        functools.partial(_norm_rope_kernel, scale=scale, rope=rope),
        out_shape=jax.ShapeDtypeStruct((b, h, s, d), out_dtype),
        grid=(b, h, s // ts),
        in_specs=in_specs,
        out_specs=pl.BlockSpec((None, None, ts, d), lambda bi, hi, si: (bi, hi, si, 0)),
        compiler_params=_params(("parallel", "parallel", "parallel")),
        name="head_norm_rope" if rope else "head_norm",
    )(*args)


def _rope_tables(pos, d):
    inv = ROPE_THETA ** (-jnp.arange(0, d, 2, dtype=F32) / d)
    ang = pos.astype(F32)[..., None] * inv
    return jnp.cos(ang), jnp.sin(ang)


def _rotate_half_perm(d_total, start, half):
    p = np.zeros((d_total, d_total), np.float32)
    for i in range(half):
        p[start + half + i, start + i] = -1.0
        p[start + i, start + half + i] = 1.0
    return jnp.asarray(p, BF16)


V_ONES_ROWS = 16
FLASH_STRIP = 256
FLASH_UNROLL = 4


def _flash_scratch(tk, nq, dv):
    return [pltpu.VMEM((2, tk, nq), F32), pltpu.VMEM((1, nq), F32), pltpu.VMEM((1, nq), F32),
            pltpu.VMEM((dv + V_ONES_ROWS, nq), F32)]


def _flash_causal_t(get_q_t, k_ref, vt_ref, scratch, n_full, tk, last_mask):
    s_scr, m_scr, cm_scr, acc_scr = scratch
    dv = acc_scr.shape[0] - V_ONES_ROWS
    strips = [slice(r * FLASH_STRIP, (r + 1) * FLASH_STRIP) for r in range(tk // FLASH_STRIP)]

    def scores(j, slot):
        cm = None
        for r, rows in enumerate(strips):
            start = pl.multiple_of(j * tk + r * FLASH_STRIP, FLASH_STRIP)
            s = _dot(k_ref[pl.ds(start, FLASH_STRIP), :], get_q_t(j))
            s_scr[slot, rows, :] = s
            c = jnp.max(s, axis=0, keepdims=True)
            cm = c if cm is None else jnp.maximum(cm, c)
        return cm

    def stage(j, cur, nxt):
        cm_next = scores(j + 1, nxt)
        m = m_scr[...]
        m_new = jnp.maximum(m, cm_scr[...])
        pv = None
        for rows in strips:
            p = jnp.exp(s_scr[cur, rows, :] - m_new).astype(BF16)
            d = _dot(vt_ref[j, :, rows], p)
            pv = d if pv is None else pv + d
        acc_scr[...] = jnp.exp(m - m_new) * acc_scr[...] + pv
        m_scr[...] = m_new
        cm_scr[...] = cm_next

    m_scr[...] = jnp.full(m_scr.shape, -jnp.inf, F32)
    acc_scr[...] = jnp.zeros(acc_scr.shape, F32)
    cm_scr[...] = scores(0, 0)

    def group(i, carry):
        for u in range(FLASH_UNROLL):
            stage(FLASH_UNROLL * i + u, u % 2, (u + 1) % 2)
        return carry

    n_groups = n_full // FLASH_UNROLL
    lax.fori_loop(0, n_groups, group, 0)
    for u in range(FLASH_UNROLL - 1):

        @pl.when(n_full - n_groups * FLASH_UNROLL > u)
        def _():
            stage(n_groups * FLASH_UNROLL + u, u % 2, (u + 1) % 2)

    s = jnp.where(last_mask, s_scr[n_full % 2], -jnp.inf)
    m = m_scr[...]
    m_new = jnp.maximum(m, jnp.max(s, axis=0, keepdims=True))
    p = jnp.exp(s - m_new).astype(BF16)
    acc = jnp.exp(m - m_new) * acc_scr[...] + _dot(vt_ref[n_full], p)
    return acc[0:dv, :] / acc[dv:dv + 1, :]


def _with_ones_rows_t(v, tk):
    *lead, s, dv = v.shape
    extra = jnp.zeros((*lead, s, V_ONES_ROWS), v.dtype).at[..., 0].set(1.0)
    va = jnp.concatenate([v, extra], axis=-1).astype(BF16)
    va = va.reshape(*lead, s // tk, tk, dv + V_ONES_ROWS)
    return jnp.swapaxes(va, -1, -2)


def _mla_attn_kernel(qt_ref, k_ref, vt_ref, o_ref, *scratch, tk):
    qi = pl.program_id(2)
    tq = qt_ref.shape[1]
    kpos = lax.broadcasted_iota(jnp.int32, (tk, tq), 0)
    qpos = lax.broadcasted_iota(jnp.int32, (tk, tq), 1)
    o = _flash_causal_t(lambda j: qt_ref[...], k_ref, vt_ref, scratch, qi, tk, kpos <= qpos)
    o_ref[...] = o.astype(o_ref.dtype)


def _mla_attention(q_t, k, v_t, tq, tk):
    b, h, dq, s = q_t.shape
    nk, dva = v_t.shape[2], v_t.shape[3]
    dv = dva - V_ONES_ROWS
    assert tq == tk
    return pl.pallas_call(
        functools.partial(_mla_attn_kernel, tk=tk),
        out_shape=jax.ShapeDtypeStruct((b, h, dv, s), BF16),
        grid=(b, h, s // tq),
        in_specs=[
            pl.BlockSpec((None, None, dq, tq), lambda bi, hi, qi: (bi, hi, 0, qi)),
            pl.BlockSpec((None, None, s, dq), lambda bi, hi, qi: (bi, hi, 0, 0)),
            pl.BlockSpec((None, None, nk, dva, tk), lambda bi, hi, qi: (bi, hi, 0, 0, 0)),
        ],
        out_specs=pl.BlockSpec((None, None, dv, tq), lambda bi, hi, qi: (bi, hi, 0, qi)),
        scratch_shapes=_flash_scratch(tk, tq, dv),
        compiler_params=_params(("parallel", "parallel", "arbitrary")),
        name="mla_flash",
    )(q_t, k, v_t)


def _compress_kernel(x_ref, pelo_ref, pehi_ref, w1lo_ref, w1hi_ref, w2_ref, g_ref, o_ref, *, norm):
    x = x_ref[...]
    nc = x.shape[0]
    a = _dot((x + pelo_ref[...]).astype(BF16), w1lo_ref[...])
    b = _dot((x + pehi_ref[...]).astype(BF16), w1hi_ref[...])
    hid = _silu(a + pltpu.roll(b, shift=nc - 1, axis=0))
    y = _dot(hid.astype(BF16), w2_ref[...])
    if norm:
        y = _rms(y, g_ref[...])
    o_ref[...] = y.astype(o_ref.dtype)


def _compress(x, pe, w1, w2, gain, norm):
    b, g, nc, wdt = x.shape
    dh = w2.shape[1]
    hidden = w1.shape[1]
    half = NSA_CMP_STRIDE
    pelo = pe[:half].reshape(1, wdt)
    pehi = pe[half:].reshape(1, wdt)
    w1 = w1.astype(BF16)
    const = lambda bi, gi: (0, 0)
    return pl.pallas_call(
        functools.partial(_compress_kernel, norm=norm),
        out_shape=jax.ShapeDtypeStruct((b, g, nc, dh), BF16),
        grid=(b, g),
        in_specs=[
            pl.BlockSpec((None, None, nc, wdt), lambda bi, gi: (bi, gi, 0, 0)),
            pl.BlockSpec((1, wdt), const),
            pl.BlockSpec((1, wdt), const),
            pl.BlockSpec((wdt, hidden), const),
            pl.BlockSpec((wdt, hidden), const),
            pl.BlockSpec((hidden, dh), const),
            pl.BlockSpec((1, dh), const),
        ],
        out_specs=pl.BlockSpec((None, None, nc, dh), lambda bi, gi: (bi, gi, 0, 0)),
        compiler_params=_params(("parallel", "parallel")),
        name="nsa_compress",
    )(x, pelo, pehi, w1[:wdt], w1[wdt:], w2.astype(BF16), gain.reshape(1, dh))


def _to_token_major(o_t, n_rep, tq):
    stacked = jnp.concatenate([o_t[:, r * tq:(r + 1) * tq] for r in range(n_rep)], axis=0)
    return stacked.T


def _nsa_cmp_kernel(qt_ref, kc_ref, vct_ref, band_ref, o_ref, sb_ref, s_scr, *, tq, n_rep):
    qi = pl.program_id(2)
    q0 = qi * tq
    nq = n_rep * tq
    n_ch, ch = s_scr.shape[0], s_scr.shape[1]
    dh = qt_ref.shape[0]
    nb = band_ref.shape[1]
    q_t = qt_ref[...]
    n_vis = (q0 + tq - NSA_CMP_STRIDE) // NSA_CMP_STRIDE
    n_used = (n_vis + ch - 1) // ch
    tpos = q0 + (lax.broadcasted_iota(jnp.int32, (ch, nq), 1) & (tq - 1))
    end0 = lax.broadcasted_iota(jnp.int32, (ch, nq), 0) * NSA_CMP_STRIDE + (NSA_CMP_LEN - 1)

    def scores(c, cm):
        s = _dot(kc_ref[pl.ds(pl.multiple_of(c * ch, ch), ch), :], q_t)
        s = jnp.where(end0 + c * (ch * NSA_CMP_STRIDE) <= tpos, s, -jnp.inf)
        s_scr[c] = s
        return jnp.maximum(cm, jnp.max(s, axis=0, keepdims=True))

    m = lax.fori_loop(0, n_used, scores, jnp.full((1, nq), -jnp.inf, F32))
    m = jnp.where(m > -jnp.inf, m, 0.0)

    def exps(c, l):
        p = jnp.exp(s_scr[c] - m)
        s_scr[c] = p
        return l + jnp.sum(p, axis=0, keepdims=True)

    l = lax.fori_loop(0, n_used, exps, jnp.zeros((1, nq), F32))
    inv = 1.0 / jnp.maximum(l, 1e-30)

    def outputs(c, carry):
        o_t, imp = carry
        p = s_scr[c] * inv
        o_t = o_t + _dot(vct_ref[c], p.astype(BF16))
        psum = p[:, 0:tq]
        for r in range(1, n_rep):
            psum = psum + p[:, r * tq:(r + 1) * tq]
        return o_t, imp + _dot3_left(band_ref[c], psum)

    o_t, imp = lax.fori_loop(0, n_used, outputs, (jnp.zeros((dh, nq), F32), jnp.zeros((nb, tq), F32)))
    o_ref[...] = _to_token_major(o_t, n_rep, tq).astype(o_ref.dtype)

    blk = lax.broadcasted_iota(jnp.int32, (nb, tq), 0)
    cur = (q0 + lax.broadcasted_iota(jnp.int32, (nb, tq), 1)) // NSA_SEL_LEN
    causal = blk <= cur
    forced = causal & ((blk == 0) | (blk > cur - NSA_N_LOCAL))
    score = jnp.where(forced, 1e30, jnp.where(causal, imp, -1.0))
    blk_f = blk.astype(F32)
    bias = jnp.full((nb, tq), SEL_MASK_BIAS, F32)
    for _ in range(NSA_N_SEL):
        best = jnp.max(score, axis=0, keepdims=True)
        idx = jnp.min(jnp.where(score == best, blk_f, float(nb)), axis=0, keepdims=True)
        pick = (blk_f == idx) & (best >= 0.0)
        bias = jnp.where(pick, 0.0, bias)
        score = jnp.where(pick, -1.0, score)
    sb_ref[...] = bias.astype(sb_ref.dtype)


NSA_CMP_CHUNK = 256


def _nsa_cmp_select(q_t, k_cmp, v_cmp, band, tq):
    b, g, n_q, dh, nq = q_t.shape
    r = nq // tq
    nc = k_cmp.shape[2]
    nb = band.shape[0]
    ch = min(NSA_CMP_CHUNK, nc)
    n_ch = nc // ch
    vc_t = jnp.swapaxes(v_cmp.reshape(b, g, n_ch, ch, dh), -1, -2)
    band_c = band.reshape(nb, n_ch, ch).transpose(1, 0, 2)
    return pl.pallas_call(
        functools.partial(_nsa_cmp_kernel, tq=tq, n_rep=r),
        out_shape=(jax.ShapeDtypeStruct((b * n_q * tq, g * r * dh), BF16),
                   jax.ShapeDtypeStruct((b, g, nb, n_q * tq), BF16)),
        grid=(b, g, n_q),
        in_specs=[
            pl.BlockSpec((None, None, None, dh, nq), lambda bi, gi, qi: (bi, gi, qi, 0, 0)),
            pl.BlockSpec((None, None, nc, dh), lambda bi, gi, qi: (bi, gi, 0, 0)),
            pl.BlockSpec((None, None, n_ch, dh, ch), lambda bi, gi, qi: (bi, gi, 0, 0, 0)),
            pl.BlockSpec((n_ch, nb, ch), lambda bi, gi, qi: (0, 0, 0)),
        ],
        out_specs=(pl.BlockSpec((tq, r * dh), lambda bi, gi, qi: (bi * n_q + qi, gi)),
                   pl.BlockSpec((None, None, nb, tq), lambda bi, gi, qi: (bi, gi, 0, qi))),
        scratch_shapes=[pltpu.VMEM((n_ch, ch, nq), F32)],
        compiler_params=_params(("parallel", "parallel", "parallel")),
        name="nsa_cmp_select",
    )(q_t, k_cmp, vc_t, band_c)


def _nsa_sel_kernel(qt_ref, sb_ref, k_ref, vt_ref, o_ref, qa_scr, *scratch, tq, tk, n_rep, tiles_per_half):
    qi = pl.program_id(2)
    q0 = qi * tq
    nq = n_rep * tq
    n_half = qa_scr.shape[0]
    q_t = qt_ref[...]
    for hh in range(n_half):
        sb = sb_ref[hh * SEL_ONEHOT:(hh + 1) * SEL_ONEHOT, :]
        qa_scr[hh] = jnp.concatenate([jnp.concatenate([sb] * n_rep, axis=1), q_t], axis=0)
    jd = q0 // tk
    kpos = jd * tk + lax.broadcasted_iota(jnp.int32, (tk, nq), 0)
    tpos = q0 + (lax.broadcasted_iota(jnp.int32, (tk, nq), 1) & (tq - 1))
    o = _flash_causal_t(lambda j: qa_scr[j // tiles_per_half], k_ref, vt_ref, scratch, jd, tk, kpos <= tpos)
    o_ref[...] = _to_token_major(o, n_rep, tq).astype(o_ref.dtype)


def _nsa_sel_attention(q_t, sel_bias, k_aug, v_t, tq, tk):
    b, g, n_q, dh, nq = q_t.shape
    nb, s = sel_bias.shape[2], sel_bias.shape[3]
    n_half = nb // SEL_ONEHOT
    ka = k_aug.shape[-1]
    nk, dva = v_t.shape[2], v_t.shape[3]
    tiles_per_half = SEL_ONEHOT * NSA_SEL_LEN // tk
    return pl.pallas_call(
        functools.partial(_nsa_sel_kernel, tq=tq, tk=tk, n_rep=nq // tq, tiles_per_half=tiles_per_half),
        out_shape=jax.ShapeDtypeStruct((b * s, g * (nq // tq) * dh), BF16),
        grid=(b, g, n_q),
        in_specs=[
            pl.BlockSpec((None, None, None, dh, nq), lambda bi, gi, qi: (bi, gi, qi, 0, 0)),
            pl.BlockSpec((None, None, nb, tq), lambda bi, gi, qi: (bi, gi, 0, qi)),
            pl.BlockSpec((None, None, s, ka), lambda bi, gi, qi: (bi, gi, 0, 0)),
            pl.BlockSpec((None, None, nk, dva, tk), lambda bi, gi, qi: (bi, gi, 0, 0, 0)),
        ],
        out_specs=pl.BlockSpec((tq, (nq // tq) * dh), lambda bi, gi, qi: (bi * n_q + qi, gi)),
        scratch_shapes=[pltpu.VMEM((n_half, ka, nq), BF16)] + _flash_scratch(tk, nq, dva - V_ONES_ROWS),
        compiler_params=_params(("parallel", "parallel", "arbitrary")),
        name="nsa_sel_flash",
    )(q_t, sel_bias, k_aug, v_t)


def _nsa_win_kernel(qt_ref, k_ref, vt_ref, o_ref, *, tq, win, n_rep):
    qi = pl.program_id(2)
    q0 = qi * tq
    nq = n_rep * tq
    dv = vt_ref.shape[-2] - V_ONES_ROWS
    n_chunk = (tq + win) // tq
    q_t = qt_ref[...]
    tpos = q0 + (lax.broadcasted_iota(jnp.int32, (tq, nq), 1) & (tq - 1))
    krow = lax.broadcasted_iota(jnp.int32, (tq, nq), 0)
    chunk_ids = [jnp.maximum(qi + c - (n_chunk - 1), 0) for c in range(n_chunk)]
    ss = []
    m = None
    for c in range(n_chunk):
        s = _dot(k_ref[pl.ds(pl.multiple_of(chunk_ids[c] * tq, tq), tq), :], q_t)
        kpos = q0 - win + c * tq + krow
        if c < n_chunk - 1:
            mask = (kpos > tpos - win) & (kpos >= 0)
        else:
            mask = kpos <= tpos
        s = jnp.where(mask, s, -jnp.inf)
        ss.append(s)
        cm = jnp.max(s, axis=0, keepdims=True)
        m = cm if m is None else jnp.maximum(m, cm)
    acc = None
    for c in range(n_chunk):
        d = _dot(vt_ref[chunk_ids[c]], jnp.exp(ss[c] - m).astype(BF16))
        acc = d if acc is None else acc + d
    o_t = acc[0:dv, :] / acc[dv:dv + 1, :]
    o_ref[...] = _to_token_major(o_t, n_rep, tq).astype(o_ref.dtype)


def _nsa_win_attention(q_t, k_pad, vt_pad, tq):
    b, g, n_q, dh, nq = q_t.shape
    sp = k_pad.shape[2]
    nkc, dva = vt_pad.shape[2], vt_pad.shape[3]
    r = nq // tq
    return pl.pallas_call(
        functools.partial(_nsa_win_kernel, tq=tq, win=NSA_WINDOW, n_rep=r),
        out_shape=jax.ShapeDtypeStruct((b * n_q * tq, g * r * dh), BF16),
        grid=(b, g, n_q),
        in_specs=[
            pl.BlockSpec((None, None, None, dh, nq), lambda bi, gi, qi: (bi, gi, qi, 0, 0)),
            pl.BlockSpec((None, None, sp, dh), lambda bi, gi, qi: (bi, gi, 0, 0)),
            pl.BlockSpec((None, None, nkc, dva, tq), lambda bi, gi, qi: (bi, gi, 0, 0, 0)),
        ],
        out_specs=pl.BlockSpec((tq, r * dh), lambda bi, gi, qi: (bi * n_q + qi, gi)),
        compiler_params=_params(("parallel", "parallel", "arbitrary")),
        name="nsa_window",
    )(q_t, k_pad, vt_pad)


def _nsa_out_kernel(oc_ref, os_ref, ow_ref, gl_ref, e_ref, w_ref, x_ref, gate_ref, o_ref):
    sg = _sigmoid(gl_ref[...].astype(F32))
    hi = sg.astype(BF16)
    lo = (sg - hi.astype(F32)).astype(BF16)

    def expand(i):
        return _dot(hi, e_ref[i]) + _dot(lo, e_ref[i])

    o = expand(0) * oc_ref[...] + expand(1) * os_ref[...] + expand(2) * ow_ref[...]
    o_ref[...] = x_ref[...] + gate_ref[...] * _dot(o.astype(BF16), w_ref[...])


def _nsa_out(o_cmp, o_sel, o_win, gate_logits, expand, w, x, gate, seq, tm=512):
    t, d = x.shape
    k = o_cmp.shape[1]
    gw = gate_logits.shape[1]
    b = gate.shape[0]
    tpb = seq // tm
    row = lambda i: (i, 0)
    return pl.pallas_call(
        _nsa_out_kernel,
        out_shape=jax.ShapeDtypeStruct((t, d), F32),
        grid=(t // tm,),
        in_specs=[
            pl.BlockSpec((tm, k), row),
            pl.BlockSpec((tm, k), row),
            pl.BlockSpec((tm, k), row),
            pl.BlockSpec((tm, gw), row),
            pl.BlockSpec((3, gw, k), lambda i: (0, 0, 0)),
            pl.BlockSpec((k, d), lambda i: (0, 0)),
            pl.BlockSpec((tm, d), row),
            pl.BlockSpec((None, 1, d), lambda i: (i // tpb, 0, 0)),
        ],
        out_specs=pl.BlockSpec((tm, d), row),
        compiler_params=_params(("parallel",)),
        name="nsa_out",
    )(o_cmp, o_sel, o_win, gate_logits, expand, w.astype(BF16), x, gate.reshape(b, 1, d))


def _ret_kernel(lg_ref, q_ref, k_ref, v_ref, g_ref, cos_ref, sin_ref, gn_ref, o_ref, st_scr):
    h = pl.program_id(1)
    c = pl.program_id(2)

    @pl.when(c == 0)
    def _():
        st_scr[...] = jnp.zeros_like(st_scr)

    cn = q_ref.shape[0]
    dk = q_ref.shape[1]
    half = dk // 2
    lg = jnp.full((1, 1), lg_ref[h], F32)
    cos = cos_ref[...]
    sin = sin_ref[...]

    def rope(x):
        x1, x2 = x[:, :half], x[:, half:]
        return jnp.concatenate([x1 * cos - x2 * sin, x1 * sin + x2 * cos], axis=1)

    q = rope(q_ref[...].astype(F32))
    k = rope(k_ref[...].astype(F32)) * (dk ** -0.5)
    v = v_ref[...].astype(BF16)
    n = lax.broadcasted_iota(jnp.int32, (cn, 1), 0).astype(F32)
    rel = (lax.broadcasted_iota(jnp.int32, (cn, cn), 0) - lax.broadcasted_iota(jnp.int32, (cn, cn), 1)).astype(F32)
    decay = jnp.where(rel >= 0, jnp.exp(jnp.maximum(rel, 0.0) * lg), 0.0)
    xi = jnp.exp((n + 1.0) * lg)
    zeta = jnp.exp((cn - 1.0 - n) * lg)
    gamma_c = jnp.exp(cn * lg)

    qb = q.astype(BF16)
    inner = _dot((_dot_nt(qb, k.astype(BF16)) * decay).astype(BF16), v)
    st = st_scr[...]
    cross = _dot(qb, st.astype(BF16)) * xi
    kz_t = (k * zeta).T.astype(BF16)
    st_scr[...] = gamma_c * st + _dot(kz_t, v)
    y = inner + cross
    mu = jnp.mean(y, axis=-1, keepdims=True)
    yc = y - mu
    var = jnp.mean(yc * yc, axis=-1, keepdims=True)
    yn = yc * lax.rsqrt(var + NORM_EPS) * gn_ref[...]
    o_ref[...] = (_silu(g_ref[...].astype(F32)) * yn).astype(o_ref.dtype)


def _retention(proj, cos_t, sin_t, log_gamma, gn_w, batch, seq, heads):
    t = proj.shape[0]
    dk, dv, cn = RET_QK, RET_V, RET_CHUNK
    n_ch = seq // cn
    kb = heads * dk // dk
    vb = 2 * heads * dk // dv
    row = lambda bi, hi, ci, *_: bi * n_ch + ci
    grid_spec = pltpu.PrefetchScalarGridSpec(
        num_scalar_prefetch=1,
        grid=(batch, heads, n_ch),
        in_specs=[
            pl.BlockSpec((cn, dk), lambda bi, hi, ci, lg: (row(bi, hi, ci), hi)),
            pl.BlockSpec((cn, dk), lambda bi, hi, ci, lg: (row(bi, hi, ci), kb + hi)),
            pl.BlockSpec((cn, dv), lambda bi, hi, ci, lg: (row(bi, hi, ci), vb + hi)),
            pl.BlockSpec((cn, dv), lambda bi, hi, ci, lg: (row(bi, hi, ci), vb + heads + hi)),
            pl.BlockSpec((cn, dk // 2), lambda bi, hi, ci, lg: (row(bi, hi, ci), 0)),
            pl.BlockSpec((cn, dk // 2), lambda bi, hi, ci, lg: (row(bi, hi, ci), 0)),
            pl.BlockSpec((1, dv), lambda bi, hi, ci, lg: (0, hi)),
        ],
        out_specs=pl.BlockSpec((cn, dv), lambda bi, hi, ci, lg: (row(bi, hi, ci), hi)),
        scratch_shapes=[pltpu.VMEM((dk, dv), F32)],
    )
    return pl.pallas_call(
        _ret_kernel,
        out_shape=jax.ShapeDtypeStruct((t, heads * dv), BF16),
        grid_spec=grid_spec,
        compiler_params=_params(("parallel", "parallel", "arbitrary")),
        name="retention",
    )(log_gamma, proj, proj, proj, proj, cos_t, sin_t, gn_w.reshape(1, heads * dv))


def _softplus(x):
    return jnp.maximum(x, 0.0) + jnp.log1p(jnp.exp(-jnp.abs(x)))


def _ssd_kernel(row_ref, dt_ref, dtt_ref, cw_ref, cb_ref, dtb_ref, dtbt_ref, al_ref, alt_ref, dsk_ref, nw_ref,
                ex_ref, o_ref, buf_scr, st_scr, *, inner, groups, state, heads):
    c = pl.program_id(1)
    cn = row_ref.shape[0]
    conv_ch = inner + 2 * groups * state
    hp = inner // heads
    rep = heads // groups
    gw = rep * hp
    halo = 8

    @pl.when(c == 0)
    def _():
        buf_scr[0:halo, :] = jnp.zeros((halo, conv_ch), F32)
        st_scr[...] = jnp.zeros_like(st_scr)

    z = row_ref[:, 0:inner].astype(F32)
    xbc = row_ref[:, inner:inner + conv_ch].astype(F32)
    dt_raw = dt_ref[...]

    buf_scr[halo:halo + cn, :] = xbc
    conv = cb_ref[...] + cw_ref[0:1, :] * buf_scr[halo - 3:halo - 3 + cn, :]
    for kk in range(1, SSD_CONV):
        conv = conv + cw_ref[kk:kk + 1, :] * buf_scr[halo - 3 + kk:halo - 3 + kk + cn, :]
    buf_scr[0:halo, :] = xbc[cn - halo:cn, :]
    act = _silu(conv)
    xs = act[:, 0:inner]
    bm = act[:, inner:inner + groups * state]
    cm = act[:, inner + groups * state:conv_ch]

    dt = _softplus(dt_raw + dtb_ref[...])
    dt_t = _softplus(dtt_ref[...] + dtbt_ref[...])
    a = -jnp.exp(al_ref[...])
    a_t = -jnp.exp(alt_ref[...])
    ri = lax.broadcasted_iota(jnp.int32, (cn, cn), 0)
    ci = lax.broadcasted_iota(jnp.int32, (cn, cn), 1)
    tril = ci <= ri
    lower = jnp.where(tril, 1.0, 0.0).astype(BF16)
    upper = jnp.where(ci >= ri, 1.0, 0.0).astype(BF16)
    cum = _dot3_left(lower, dt * a)
    cum_t = _dot3(dt_t * a_t, upper)
    cum_last = cum[cn - 1:cn, :]

    ex = ex_ref[...]
    e_cum = _dot3(jnp.exp(cum), ex)
    w_end = _dot3(jnp.exp(cum_last - cum) * dt, ex)
    e_last = _dot3(jnp.exp(cum_last), ex)

    lane = lax.broadcasted_iota(jnp.int32, (cn, 2 * hp), 1)
    ys = []
    for g in range(groups):
        cc = cm[:, g * state:(g + 1) * state].astype(BF16)
        bc = bm[:, g * state:(g + 1) * state].astype(BF16)
        cb = _dot_nt(cc, bc)
        xg = xs[:, g * gw:(g + 1) * gw]
        st = st_scr[g]
        cross = _dot(cc, st.astype(BF16)) * e_cum[:, g * gw:(g + 1) * gw]
        xw_t = bc.astype(F32).T.astype(BF16)
        st_scr[g] = e_last[:, g * gw:(g + 1) * gw] * st + _dot(
            xw_t, (xg * w_end[:, g * gw:(g + 1) * gw]).astype(BF16))
        intra = []
        for pr in range(rep // 2):
            ws = []
            for hh in (2 * pr, 2 * pr + 1):
                hd = g * rep + hh
                seg = jnp.where(tril, cum[:, hd:hd + 1] - cum_t[hd:hd + 1, :], -jnp.inf)
                ws.append((jnp.exp(seg) * cb * dt_t[hd:hd + 1, :]).astype(BF16))
            w2 = jnp.concatenate(ws, axis=1)
            xp = xg[:, 2 * pr * hp:(2 * pr + 2) * hp]
            x2 = jnp.concatenate([jnp.where(lane < hp, xp, 0.0), jnp.where(lane >= hp, xp, 0.0)], axis=0)
            intra.append(_dot(w2, x2.astype(BF16)))
        ys.append(jnp.concatenate(intra, axis=1) + cross)
    y = jnp.concatenate(ys, axis=1) + dsk_ref[...] * xs
    y = y * _silu(z)
    outs = []
    for g in range(groups):
        outs.append(_rms(y[:, g * gw:(g + 1) * gw], nw_ref[:, g * gw:(g + 1) * gw]))
    o_ref[...] = jnp.concatenate(outs, axis=1).astype(o_ref.dtype)


def _ssd(proj, dt_raw, conv_w, conv_b, dt_bias, a_log, d_skip, norm_w, batch, seq, inner, heads):
    t, n_pad = proj.shape
    groups, state, cn = SSD_GROUPS, SSD_STATE, SSD_CHUNK
    conv_ch = inner + 2 * groups * state
    n_ch = seq // cn
    hp = inner // heads
    gw = inner // groups
    hpad = dt_raw.shape[1]
    assert heads <= hpad and n_pad >= inner + conv_ch
    dt_t = dt_raw.reshape(batch, seq, hpad).transpose(0, 2, 1)
    ex_np = np.zeros((hpad, inner), np.float32)
    ex_np[:heads] = np.kron(np.eye(heads, dtype=np.float32), np.ones((1, hp), np.float32))
    ex = jnp.asarray(ex_np, BF16)
    dt_bias = jnp.pad(dt_bias, (0, hpad - heads))
    a_log = jnp.pad(a_log, (0, hpad - heads))
    heads_k = heads
    heads = hpad
    const = lambda bi, ci: (0, 0)
    return pl.pallas_call(
        functools.partial(_ssd_kernel, inner=inner, groups=groups, state=state, heads=heads_k),
        out_shape=jax.ShapeDtypeStruct((t, inner), BF16),
        grid=(batch, n_ch),
        in_specs=[
            pl.BlockSpec((cn, n_pad), lambda bi, ci: (bi * n_ch + ci, 0)),
            pl.BlockSpec((cn, heads), lambda bi, ci: (bi * n_ch + ci, 0)),
            pl.BlockSpec((None, heads, cn), lambda bi, ci: (bi, 0, ci)),
            pl.BlockSpec((SSD_CONV, conv_ch), const),
            pl.BlockSpec((1, conv_ch), const),
            pl.BlockSpec((1, heads), const),
            pl.BlockSpec((heads, 1), const),
            pl.BlockSpec((1, heads), const),
            pl.BlockSpec((heads, 1), const),
            pl.BlockSpec((1, inner), const),
            pl.BlockSpec((1, inner), const),
            pl.BlockSpec((heads, inner), const),
        ],
        out_specs=pl.BlockSpec((cn, inner), lambda bi, ci: (bi * n_ch + ci, 0)),
        scratch_shapes=[pltpu.VMEM((8 + cn, conv_ch), F32), pltpu.VMEM((groups, state, gw), F32)],
        compiler_params=_params(("parallel", "arbitrary")),
        name="ssd_scan",
    )(proj, dt_raw, dt_t, conv_w, conv_b.reshape(1, conv_ch), dt_bias.reshape(1, heads), dt_bias.reshape(heads, 1),
      a_log.reshape(1, heads), a_log.reshape(heads, 1), jnp.repeat(d_skip, hp).reshape(1, inner),
      norm_w.reshape(1, inner), ex)


def _nsa_layer(x, mods, norm_g, batch, seq, w_in, q_norm, k_norm, cmp_pe, cmp_w1, cmp_w2, w_out):
    sh, sc, gate = mods
    d = x.shape[1]
    dh, g = HEAD_DIM, NSA_GROUPS
    heads = d // dh
    r = heads // g
    kvw = g * dh
    proj = _modnorm_matmul(x, norm_g, sc, sh, w_in, seq, out_dtype=BF16)
    off = heads * dh

    def heads_major(cols, n_heads):
        return cols.reshape(batch, seq, n_heads, dh).transpose(0, 2, 1, 3)

    q = heads_major(proj[:, :off], heads)
    parts = [proj[:, off + i * kvw: off + (i + 1) * kvw] for i in range(6)]
    k_c, v_c, k_s, v_s, k_w, v_w = parts
    gate_logits = proj[:, off + 6 * kvw: off + 6 * kvw + LANES]

    tq, tk = NSA_TQ, min(NSA_TK, seq)
    n_q = seq // tq
    qn = _norm_rope(q, q_norm, scale=dh ** -0.5)
    q_t = qn.reshape(batch, g, r, n_q, tq, dh).transpose(0, 1, 3, 5, 2, 4).reshape(batch, g, n_q, dh, r * tq)

    def chunk_rows(cols):
        st = NSA_CMP_STRIDE
        return cols.reshape(batch, seq // st, st, g, dh).transpose(0, 3, 1, 2, 4).reshape(batch, g, seq // st, st * dh)

    k_cmp = _compress(chunk_rows(k_c), cmp_pe[0], cmp_w1[0], cmp_w2[0], k_norm[0], norm=True)
    v_cmp = _compress(chunk_rows(v_c), cmp_pe[1], cmp_w1[1], cmp_w2[1], k_norm[0], norm=False)

    n_blk = seq // NSA_SEL_LEN
    n_cmp = seq // NSA_CMP_STRIDE
    assert n_blk % SEL_ONEHOT == 0
    ratio = NSA_SEL_LEN // NSA_CMP_STRIDE
    jj = np.arange(n_blk)[:, None]
    cc = np.arange(n_cmp)[None, :]
    band = jnp.asarray(((cc >= ratio * jj - 1) & (cc <= ratio * jj + ratio - 1)).astype(np.float32), BF16)
    o_cmp, sel_bias = _nsa_cmp_select(q_t, k_cmp, v_cmp, band, tq)

    ks_n = _norm_rope(heads_major(k_s, g), k_norm[1])
    onehot = jnp.asarray(np.eye(SEL_ONEHOT, dtype=np.float32)[(np.arange(seq) // NSA_SEL_LEN) % SEL_ONEHOT], BF16)
    k_aug = jnp.concatenate([jnp.broadcast_to(onehot, (batch, g, seq, SEL_ONEHOT)), ks_n], axis=-1)
    o_sel = _nsa_sel_attention(q_t, sel_bias, k_aug, _with_ones_rows_t(heads_major(v_s, g), tk), tq, tk)

    kw_n = _norm_rope(heads_major(k_w, g), k_norm[2])
    o_win = _nsa_win_attention(q_t, kw_n, _with_ones_rows_t(heads_major(v_w, g), tq), tq)

    e = np.zeros((3, LANES, heads * dh), np.float32)
    for hd in range(heads):
        for i in range(3):
            e[i, hd * 3 + i, hd * dh:(hd + 1) * dh] = 1.0
    return _nsa_out(o_cmp, o_sel, o_win, gate_logits, jnp.asarray(e, BF16), w_out, x, gate, seq)


def _mla_layer(x, mods, norm_g, batch, seq, pos, w_in, q_a_norm, kv_a_norm, w_q_b, w_kv_b, q_norm, k_norm, w_out):
    sh, sc, gate = mods
    d = x.shape[1]
    heads = w_out.shape[0] // MLA_V
    dq = MLA_NOPE + MLA_ROPE
    proj = _modnorm_matmul(x, norm_g, sc, sh, w_in, seq, out_dtype=BF16)
    cq = proj[:, :MLA_Q_LORA]
    ckv = proj[:, MLA_Q_LORA:MLA_Q_LORA + MLA_KV_LORA]
    k_rot = proj[:, MLA_Q_LORA + MLA_KV_LORA:MLA_Q_LORA + MLA_KV_LORA + MLA_ROPE]
    q_raw = _norm_matmul(cq, q_a_norm, w_q_b, out_dtype=BF16)
    kv_raw = _norm_matmul(ckv, kv_a_norm, w_kv_b, out_dtype=BF16)
    q_raw = q_raw.reshape(batch, seq, heads, dq).transpose(0, 2, 1, 3)
    kv_raw = kv_raw.reshape(batch, seq, heads, MLA_NOPE + MLA_V).transpose(0, 2, 1, 3)

    cos, sin = _rope_tables(pos, MLA_ROPE)
    ones = jnp.ones((batch, seq, MLA_NOPE), F32)
    cos_q = jnp.concatenate([ones, cos, cos], axis=-1)
    sin_q = jnp.concatenate([0.0 * ones, sin, sin], axis=-1)
    qn = _norm_rope(q_raw, q_norm, cos_q, sin_q, _rotate_half_perm(dq, MLA_NOPE, MLA_ROPE // 2), scale=dq ** -0.5)
    k_nope = _norm_rope(kv_raw[..., :MLA_NOPE], k_norm[:MLA_NOPE])
    k_rope = _norm_rope(k_rot.reshape(batch, 1, seq, MLA_ROPE), k_norm[MLA_NOPE:],
                        jnp.concatenate([cos, cos], axis=-1), jnp.concatenate([sin, sin], axis=-1),
                        _rotate_half_perm(MLA_ROPE, 0, MLA_ROPE // 2))
    k = jnp.concatenate([k_nope, jnp.broadcast_to(k_rope, (batch, heads, seq, MLA_ROPE))], axis=-1)
    tq = min(MLA_TQ, seq)
    tk = min(MLA_TK, tq)
    o_t = _mla_attention(jnp.swapaxes(qn, -1, -2), k, _with_ones_rows_t(kv_raw[..., MLA_NOPE:], tk), tq, tk)
    o = o_t.transpose(0, 3, 1, 2).reshape(batch * seq, heads * MLA_V).astype(BF16)
    return _out_proj(o, w_out, x, gate, seq)


def _ret_layer(x, mods, norm_g, batch, seq, pos, w_in, gn_w, w_out):
    sh, sc, gate = mods
    heads = w_out.shape[0] // RET_V
    proj = _modnorm_matmul(x, norm_g, sc, sh, w_in, seq, out_dtype=BF16)
    cos, sin = _rope_tables(pos, RET_QK)
    cos = cos.reshape(batch * seq, RET_QK // 2)
    sin = sin.reshape(batch * seq, RET_QK // 2)
    log_gamma = jnp.log1p(-(2.0 ** (-5.0 - jnp.arange(heads, dtype=F32))))
    y = _retention(proj, cos, sin, log_gamma, gn_w, batch, seq, heads)
    return _out_proj(y, w_out, x, gate, seq)


def _ssd_layer(x, mods, norm_g, batch, seq, w_in, conv_w, conv_b, dt_bias, a_log, d_skip, norm_w, w_out):
    sh, sc, gate = mods
    inner = w_out.shape[0]
    heads = dt_bias.shape[0]
    conv_ch = conv_w.shape[1]
    proj = _modnorm_matmul(x, norm_g, sc, sh, w_in[:, :inner + conv_ch], seq, out_dtype=BF16)
    dt_raw = _modnorm_matmul(x, norm_g, sc, sh, w_in[:, inner + conv_ch:], seq)
    y = _ssd(proj, dt_raw, conv_w, conv_b, dt_bias, a_log, d_skip, norm_w, batch, seq, inner, heads)
    return _out_proj(y, w_out, x, gate, seq)


def kernel(x, c, positions, ada_w, ada_b, norm_mix, norm_ffn, ffn_w_in, ffn_w_out, nsa_w_in, nsa_q_norm, nsa_k_norm, nsa_cmp_pe, nsa_cmp_w1, nsa_cmp_w2, nsa_w_out, mla_w_in, mla_q_a_norm, mla_kv_a_norm, mla_w_q_b, mla_w_kv_b, mla_q_norm, mla_k_norm, mla_w_out, ret_w_in, ret_gn_w, ret_w_out, ssd_w_in, ssd_conv_w, ssd_conv_b, ssd_dt_bias, ssd_a_log, ssd_d, ssd_norm, ssd_w_out):
    batch, seq, d = x.shape
    depth = ada_w.shape[0]
    n_mixers = 4
    mod = _modulation(c, ada_w, ada_b)
    xt = x.reshape(batch * seq, d)
    for i in range(depth):
        sh_m, sc_m, g_m, sh_f, sc_f, g_f = [mod[i, :, k * d:(k + 1) * d] for k in range(6)]
        mods = (sh_m, sc_m, g_m)
        kind, j = i % n_mixers, i // n_mixers
        if kind == 0:
            xt = _nsa_layer(xt, mods, norm_mix[i], batch, seq, nsa_w_in[j], nsa_q_norm[j], nsa_k_norm[j],
                            nsa_cmp_pe[j], nsa_cmp_w1[j], nsa_cmp_w2[j], nsa_w_out[j])
        elif kind == 1:
            xt = _mla_layer(xt, mods, norm_mix[i], batch, seq, positions, mla_w_in[j], mla_q_a_norm[j],
                            mla_kv_a_norm[j], mla_w_q_b[j], mla_w_kv_b[j], mla_q_norm[j], mla_k_norm[j], mla_w_out[j])
        elif kind == 2:
            xt = _ret_layer(xt, mods, norm_mix[i], batch, seq, positions, ret_w_in[j], ret_gn_w[j], ret_w_out[j])
        else:
            xt = _ssd_layer(xt, mods, norm_mix[i], batch, seq, ssd_w_in[j], ssd_conv_w[j], ssd_conv_b[j],
                            ssd_dt_bias[j], ssd_a_log[j], ssd_d[j], ssd_norm[j], ssd_w_out[j])
        xt = _ffn(xt, norm_ffn[i], sc_f, sh_f, g_f, ffn_w_in[i], ffn_w_out[i], seq)
    return xt.reshape(batch, seq, d)
```

```python
import functools
import math

import numpy as np
import jax
import jax.numpy as jnp
from jax import lax
from jax.experimental import pallas as pl
from jax.experimental.pallas import tpu as pltpu

F32 = jnp.float32
BF16 = jnp.bfloat16

NORM_EPS = 1e-6
ROPE_THETA = 10000.0
LANES = 128

HEAD_DIM = 64
NSA_GROUPS = 4
NSA_CMP_STRIDE = 16
NSA_CMP_LEN = 32
NSA_SEL_LEN = 64
NSA_N_SEL = 16
NSA_N_LOCAL = 2
NSA_WINDOW = 512
NSA_TQ = 128
NSA_TK = 512
SEL_ONEHOT = 128
SEL_MASK_BIAS = -32768.0

MLA_Q_LORA = 384
MLA_KV_LORA = 256
MLA_NOPE = 64
MLA_ROPE = 32
MLA_V = 64
MLA_TQ = 512
MLA_TK = 512

RET_QK = 256
RET_V = 512
RET_CHUNK = 128
RET_STEP_CHUNKS = 4

SSD_HEADDIM = 64
SSD_GROUPS = 4
SSD_STATE = 128
SSD_CONV = 4
SSD_CHUNK = 128

VMEM_LIMIT = 48 * 1024 * 1024


def _params(sem, vmem=VMEM_LIMIT, flags=None):
    return pltpu.CompilerParams(dimension_semantics=sem, vmem_limit_bytes=vmem, flags=flags)


def _sigmoid(x):
    return 1.0 / (1.0 + jnp.exp(-x))


def _silu(x):
    return x * _sigmoid(x)


def _dot(a, b):
    return jnp.dot(a, b, preferred_element_type=F32)


def _dot_nt(a, b):
    return lax.dot_general(a, b, (((1,), (1,)), ((), ())), preferred_element_type=F32)


def _split3(x):
    hi = x.astype(BF16)
    r1 = x - hi.astype(F32)
    mid = r1.astype(BF16)
    lo = (r1 - mid.astype(F32)).astype(BF16)
    return hi, mid, lo


def _dot3(x, m):
    hi, mid, lo = _split3(x)
    return _dot(hi, m) + _dot(mid, m) + _dot(lo, m)


def _dot3_left(m, x):
    hi, mid, lo = _split3(x)
    return _dot(m, hi) + _dot(m, mid) + _dot(m, lo)


def _rms(x, gain):
    ms = jnp.mean(x * x, axis=-1, keepdims=True)
    return x * lax.rsqrt(ms + NORM_EPS) * gain


def _mod_kernel(c_ref, w_ref, b_ref, o_ref):
    cond = _silu(c_ref[...]).astype(BF16)
    o_ref[...] = _dot(cond, w_ref[...]) + b_ref[...]


def _modulation(c, ada_w, ada_b):
    depth, d, n = ada_w.shape
    b = c.shape[0]
    rows = 16
    c_pad = jnp.zeros((rows, d), F32).at[:b].set(c)
    tn = 1024
    out = pl.pallas_call(
        _mod_kernel,
        out_shape=jax.ShapeDtypeStruct((depth, rows, n), F32),
        grid=(depth, n // tn),
        in_specs=[
            pl.BlockSpec((rows, d), lambda l, j: (0, 0)),
            pl.BlockSpec((None, d, tn), lambda l, j: (l, 0, j)),
            pl.BlockSpec((None, 1, tn), lambda l, j: (l, 0, j)),
        ],
        out_specs=pl.BlockSpec((None, rows, tn), lambda l, j: (l, 0, j)),
        compiler_params=_params(("parallel", "parallel")),
        name="adaln_mod",
    )(c_pad, ada_w.astype(BF16), ada_b.reshape(depth, 1, n))
    return out[:, :b]


def _modnorm_matmul_kernel(x_ref, g_ref, sc_ref, sh_ref, w_ref, o_ref, h_scr):
    @pl.when(pl.program_id(1) == 0)
    def _():
        y = _rms(x_ref[...].astype(F32), g_ref[...])
        h_scr[...] = (y * (1.0 + sc_ref[...]) + sh_ref[...]).astype(BF16)

    o_ref[...] = _dot(h_scr[...], w_ref[...]).astype(o_ref.dtype)


def _pad_and_tile(n, max_tile=1024, min_tile=512):
    n_pad = -(-n // LANES) * LANES
    while True:
        if n_pad <= max_tile:
            return n_pad, n_pad
        for tn in range(max_tile, min_tile - 1, -LANES):
            if n_pad % tn == 0:
                return n_pad, tn
        n_pad += LANES


def _modnorm_matmul(x, gain, sc, sh, w, seq, out_dtype=F32, tm=1024):
    t, d = x.shape
    tm = min(tm, seq)
    n = w.shape[1]
    n_pad, tn = _pad_and_tile(n)
    w = w.astype(BF16)
    if n_pad != n:
        w = jnp.pad(w, ((0, 0), (0, n_pad - n)))
    tpb = seq // tm
    b = sc.shape[0]
    return pl.pallas_call(
        _modnorm_matmul_kernel,
        out_shape=jax.ShapeDtypeStruct((t, n_pad), out_dtype),
        grid=(t // tm, n_pad // tn),
        in_specs=[
            pl.BlockSpec((tm, d), lambda i, j: (i, 0)),
            pl.BlockSpec((1, d), lambda i, j: (0, 0)),
            pl.BlockSpec((None, 1, d), lambda i, j: (i // tpb, 0, 0)),
            pl.BlockSpec((None, 1, d), lambda i, j: (i // tpb, 0, 0)),
            pl.BlockSpec((d, tn), lambda i, j: (0, j)),
        ],
        out_specs=pl.BlockSpec((tm, tn), lambda i, j: (i, j)),
        scratch_shapes=[pltpu.VMEM((tm, d), BF16)],
        compiler_params=_params(("parallel", "arbitrary")),
        name="modnorm_matmul",
    )(x, gain.reshape(1, d), sc.reshape(b, 1, d), sh.reshape(b, 1, d), w)


def _norm_matmul(x, gain, w, out_dtype=F32):
    t, d = x.shape
    zeros = jnp.zeros((1, d), F32)
    return _modnorm_matmul(x, gain, zeros, zeros, w, seq=t, out_dtype=out_dtype)


def _out_proj_kernel(y_ref, w_ref, x_ref, gate_ref, o_ref):
    o_ref[...] = x_ref[...] + gate_ref[...] * _dot(y_ref[...], w_ref[...])


def _out_proj(y, w, x, gate, seq, tm=512):
    t, k = y.shape
    d = w.shape[1]
    b = gate.shape[0]
    tpb = seq // tm
    return pl.pallas_call(
        _out_proj_kernel,
        out_shape=jax.ShapeDtypeStruct((t, d), F32),
        grid=(t // tm,),
        in_specs=[
            pl.BlockSpec((tm, k), lambda i: (i, 0)),
            pl.BlockSpec((k, d), lambda i: (0, 0)),
            pl.BlockSpec((tm, d), lambda i: (i, 0)),
            pl.BlockSpec((None, 1, d), lambda i: (i // tpb, 0, 0)),
        ],
        out_specs=pl.BlockSpec((tm, d), lambda i: (i, 0)),
        compiler_params=_params(("parallel",)),
        name="out_proj",
    )(y, w.astype(BF16), x, gate.reshape(b, 1, d))


def _ffn_kernel(x_ref, g_ref, sc_ref, sh_ref, wa_ref, wb_ref, wo_ref, gate_ref, o_ref, h_scr, acc_scr):
    j = pl.program_id(1)

    @pl.when(j == 0)
    def _():
        y = _rms(x_ref[...], g_ref[...])
        h_scr[...] = (y * (1.0 + sc_ref[...]) + sh_ref[...]).astype(BF16)
        acc_scr[...] = jnp.zeros_like(acc_scr)

    h = h_scr[...]
    a = _dot(h, wa_ref[...])
    b = _dot(h, wb_ref[...])
    u = (_silu(a) * b).astype(BF16)
    acc_scr[...] += _dot(u, wo_ref[...])

    @pl.when(j == pl.num_programs(1) - 1)
    def _():
        o_ref[...] = x_ref[...] + gate_ref[...] * acc_scr[...]


def _ffn(x, gain, sc, sh, gate, w_in, w_out, seq, tm=512):
    t, d = x.shape
    hid = w_out.shape[0]
    n_h = 2
    th = hid // n_h
    assert th % LANES == 0
    b = sc.shape[0]
    tpb = seq // tm
    w_in = w_in.astype(BF16)
    w_out = w_out.astype(BF16)
    return pl.pallas_call(
        _ffn_kernel,
        out_shape=jax.ShapeDtypeStruct((t, d), F32),
        grid=(t // tm, n_h),
        in_specs=[
            pl.BlockSpec((tm, d), lambda i, j: (i, 0)),
            pl.BlockSpec((1, d), lambda i, j: (0, 0)),
            pl.BlockSpec((None, 1, d), lambda i, j: (i // tpb, 0, 0)),
            pl.BlockSpec((None, 1, d), lambda i, j: (i // tpb, 0, 0)),
            pl.BlockSpec((d, th), lambda i, j: (0, j)),
            pl.BlockSpec((d, th), lambda i, j: (0, n_h + j)),
            pl.BlockSpec((th, d), lambda i, j: (j, 0)),
            pl.BlockSpec((None, 1, d), lambda i, j: (i // tpb, 0, 0)),
        ],
        out_specs=pl.BlockSpec((tm, d), lambda i, j: (i, 0)),
        scratch_shapes=[pltpu.VMEM((tm, d), BF16), pltpu.VMEM((tm, d), F32)],
        compiler_params=_params(("parallel", "arbitrary")),
        name="ffn",
    )(x, gain.reshape(1, d), sc.reshape(b, 1, d), sh.reshape(b, 1, d), w_in, w_in, w_out,
      gate.reshape(b, 1, d))


def _norm_rope_kernel(*refs, scale, rope):
    if rope:
        x_ref, g_ref, c_ref, s_ref, p_ref, o_ref = refs
    else:
        x_ref, g_ref, o_ref = refs
    y = _rms(x_ref[...].astype(F32), g_ref[...])
    if rope:
        hi = y.astype(BF16)
        lo = (y - hi.astype(F32)).astype(BF16)
        rot = _dot(hi, p_ref[...]) + _dot(lo, p_ref[...])
        y = y * c_ref[...] + rot * s_ref[...]
    if scale != 1.0:
        y = y * scale
    o_ref[...] = y.astype(o_ref.dtype)


def _norm_rope(x, gain, cos_t=None, sin_t=None, perm=None, scale=1.0, out_dtype=BF16, ts=2048):
    b, h, s, d = x.shape
    ts = min(ts, s)
    rope = cos_t is not None
    in_specs = [
        pl.BlockSpec((None, None, ts, d), lambda bi, hi, si: (bi, hi, si, 0)),
        pl.BlockSpec((1, d), lambda bi, hi, si: (0, 0)),
    ]
    args = [x, gain.reshape(1, d)]
    if rope:
        in_specs += [
            pl.BlockSpec((None, ts, d), lambda bi, hi, si: (bi, si, 0)),
            pl.BlockSpec((None, ts, d), lambda bi, hi, si: (bi, si, 0)),
            pl.BlockSpec((d, d), lambda bi, hi, si: (0, 0)),
        ]
        args += [cos_t, sin_t, perm]
    return pl.pallas_call(
        functools.partial(_norm_rope_kernel, scale=scale, rope=rope),
        out_shape=jax.ShapeDtypeStruct((b, h, s, d), out_dtype),
        grid=(b, h, s // ts),
        in_specs=in_specs,
        out_specs=pl.BlockSpec((None, None, ts, d), lambda bi, hi, si: (bi, hi, si, 0)),
        compiler_params=_params(("parallel", "parallel", "parallel")),
        name="head_norm_rope" if rope else "head_norm",
    )(*args)


def _rope_tables(pos, d):
    inv = ROPE_THETA ** (-jnp.arange(0, d, 2, dtype=F32) / d)
    ang = pos.astype(F32)[..., None] * inv
    return jnp.cos(ang), jnp.sin(ang)


def _rotate_half_perm(d_total, start, half):
    p = np.zeros((d_total, d_total), np.float32)
    for i in range(half):
        p[start + half + i, start + i] = -1.0
        p[start + i, start + half + i] = 1.0
    return jnp.asarray(p, BF16)


V_ONES_ROWS = 16
FLASH_STRIP = 256
FLASH_UNROLL = 4


def _flash_scratch(tk, nq, dv):
    return [pltpu.VMEM((2, tk, nq), F32), pltpu.VMEM((1, nq), F32), pltpu.VMEM((1, nq), F32),
            pltpu.VMEM((dv + V_ONES_ROWS, nq), F32)]


def _flash_causal_t(get_q_t, k_ref, vt_ref, scratch, n_full, tk, last_mask):
    s_scr, m_scr, cm_scr, acc_scr = scratch
    dv = acc_scr.shape[0] - V_ONES_ROWS
    strips = [slice(r * FLASH_STRIP, (r + 1) * FLASH_STRIP) for r in range(tk // FLASH_STRIP)]

    def scores(j, slot):
        cm = None
        for r, rows in enumerate(strips):
            start = pl.multiple_of(j * tk + r * FLASH_STRIP, FLASH_STRIP)
            s = _dot(k_ref[pl.ds(start, FLASH_STRIP), :], get_q_t(j))
            s_scr[slot, rows, :] = s
            c = jnp.max(s, axis=0, keepdims=True)
            cm = c if cm is None else jnp.maximum(cm, c)
        return cm

    def stage(j, cur, nxt):
        cm_next = scores(j + 1, nxt)
        m = m_scr[...]
        m_new = jnp.maximum(m, cm_scr[...])
        pv = None
        for rows in strips:
            p = jnp.exp(s_scr[cur, rows, :] - m_new).astype(BF16)
            d = _dot(vt_ref[j, :, rows], p)
            pv = d if pv is None else pv + d
        acc_scr[...] = jnp.exp(m - m_new) * acc_scr[...] + pv
        m_scr[...] = m_new
        cm_scr[...] = cm_next

    m_scr[...] = jnp.full(m_scr.shape, -jnp.inf, F32)
    acc_scr[...] = jnp.zeros(acc_scr.shape, F32)
    cm_scr[...] = scores(0, 0)

    def group(i, carry):
        for u in range(FLASH_UNROLL):
            stage(FLASH_UNROLL * i + u, u % 2, (u + 1) % 2)
        return carry

    n_groups = n_full // FLASH_UNROLL
    lax.fori_loop(0, n_groups, group, 0)
    for u in range(FLASH_UNROLL - 1):

        @pl.when(n_full - n_groups * FLASH_UNROLL > u)
        def _():
            stage(n_groups * FLASH_UNROLL + u, u % 2, (u + 1) % 2)

    s = jnp.where(last_mask, s_scr[n_full % 2], -jnp.inf)
    m = m_scr[...]
    m_new = jnp.maximum(m, jnp.max(s, axis=0, keepdims=True))
    p = jnp.exp(s - m_new).astype(BF16)
    acc = jnp.exp(m - m_new) * acc_scr[...] + _dot(vt_ref[n_full], p)
    return acc[0:dv, :] / acc[dv:dv + 1, :]


def _with_ones_rows_t(v, tk):
    *lead, s, dv = v.shape
    extra = jnp.zeros((*lead, s, V_ONES_ROWS), v.dtype).at[..., 0].set(1.0)
    va = jnp.concatenate([v, extra], axis=-1).astype(BF16)
    va = va.reshape(*lead, s // tk, tk, dv + V_ONES_ROWS)
    return jnp.swapaxes(va, -1, -2)


def _mla_attn_kernel(qt_ref, k_ref, vt_ref, o_ref, *scratch, tk):
    qi = pl.program_id(2)
    tq = qt_ref.shape[1]
    kpos = lax.broadcasted_iota(jnp.int32, (tk, tq), 0)
    qpos = lax.broadcasted_iota(jnp.int32, (tk, tq), 1)
    o = _flash_causal_t(lambda j: qt_ref[...], k_ref, vt_ref, scratch, qi, tk, kpos <= qpos)
    o_ref[...] = o.astype(o_ref.dtype)


def _mla_attention(q_t, k, v_t, tq, tk):
    b, h, dq, s = q_t.shape
    nk, dva = v_t.shape[2], v_t.shape[3]
    dv = dva - V_ONES_ROWS
    assert tq == tk
    return pl.pallas_call(
        functools.partial(_mla_attn_kernel, tk=tk),
        out_shape=jax.ShapeDtypeStruct((b, h, dv, s), BF16),
        grid=(b, h, s // tq),
        in_specs=[
            pl.BlockSpec((None, None, dq, tq), lambda bi, hi, qi: (bi, hi, 0, qi)),
            pl.BlockSpec((None, None, s, dq), lambda bi, hi, qi: (bi, hi, 0, 0)),
            pl.BlockSpec((None, None, nk, dva, tk), lambda bi, hi, qi: (bi, hi, 0, 0, 0)),
        ],
        out_specs=pl.BlockSpec((None, None, dv, tq), lambda bi, hi, qi: (bi, hi, 0, qi)),
        scratch_shapes=_flash_scratch(tk, tq, dv),
        compiler_params=_params(("parallel", "parallel", "arbitrary")),
        name="mla_flash",
    )(q_t, k, v_t)


def _hi_lo_dot(y, m):
    hi = y.astype(BF16)
    lo = (y - hi.astype(F32)).astype(BF16)
    return _dot(hi, m) + _dot(lo, m)


def _mla_q_prep_kernel(x_ref, g_ref, c_ref, s_ref, p_ref, o_ref, *, n_real, scale):
    x = x_ref[...].astype(F32)
    ms = jnp.sum(x * x, axis=-1, keepdims=True) * (1.0 / n_real)
    y = x * lax.rsqrt(ms + NORM_EPS) * g_ref[...]
    y = (y * c_ref[...] + _hi_lo_dot(y, p_ref[...]) * s_ref[...]) * scale
    o_ref[...] = y.T.astype(o_ref.dtype)


def _mla_q_prep(q_raw, gain, cos_t, sin_t, perm, batch, seq, heads, n_real, scale, ts=1024):
    ts = min(ts, seq)
    nsb = seq // ts
    hw = LANES
    const = lambda bi, hi, si: (0, 0)
    return pl.pallas_call(
        functools.partial(_mla_q_prep_kernel, n_real=n_real, scale=scale),
        out_shape=jax.ShapeDtypeStruct((batch, heads, hw, seq), BF16),
        grid=(batch, heads, nsb),
        in_specs=[
            pl.BlockSpec((ts, hw), lambda bi, hi, si: (bi * nsb + si, hi)),
            pl.BlockSpec((1, hw), const),
            pl.BlockSpec((None, ts, hw), lambda bi, hi, si: (bi, si, 0)),
            pl.BlockSpec((None, ts, hw), lambda bi, hi, si: (bi, si, 0)),
            pl.BlockSpec((hw, hw), const),
        ],
        out_specs=pl.BlockSpec((None, None, hw, ts), lambda bi, hi, si: (bi, hi, 0, si)),
        compiler_params=_params(("parallel", "parallel", "parallel")),
        name="mla_q_prep",
    )(q_raw, gain.reshape(1, hw), cos_t, sin_t, perm)


def _mla_kv_prep_kernel(kv_ref, kr_ref, g1_ref, g2_ref, c_ref, s_ref, p_ref, e_ref, k_ref, vt_ref, *, n_nope, n_rope):
    x = kv_ref[...].astype(F32)
    ts, hw = x.shape
    lane = lax.broadcasted_iota(jnp.int32, (ts, hw), 1)
    xk = jnp.where(lane < n_nope, x, 0.0)
    kn = xk * lax.rsqrt(jnp.sum(xk * xk, axis=-1, keepdims=True) * (1.0 / n_nope) + NORM_EPS) * g1_ref[...]
    r = kr_ref[...].astype(F32)
    rn = r * lax.rsqrt(jnp.sum(r * r, axis=-1, keepdims=True) * (1.0 / n_rope) + NORM_EPS) * g2_ref[...]
    rr = rn * c_ref[...] + _hi_lo_dot(rn, p_ref[...]) * s_ref[...]
    k_ref[...] = (kn + _hi_lo_dot(rr, e_ref[...])).astype(k_ref.dtype)
    v_t = x.T[n_nope:, :]
    ones = jnp.where(lax.broadcasted_iota(jnp.int32, (V_ONES_ROWS, ts), 0) == 0, 1.0, 0.0)
    vt_ref[...] = jnp.concatenate([v_t, ones], axis=0).astype(vt_ref.dtype)


def _mla_kv_prep(kv_raw, proj, rot_block, g_nope, g_rope, cos_t, sin_t, perm, place, batch, seq, heads, tk):
    hw = LANES
    nk = seq // tk
    n_nope, n_rope = MLA_NOPE, MLA_ROPE
    dva = hw - n_nope + V_ONES_ROWS
    const = lambda bi, hi, si: (0, 0)
    return pl.pallas_call(
        functools.partial(_mla_kv_prep_kernel, n_nope=n_nope, n_rope=n_rope),
        out_shape=(jax.ShapeDtypeStruct((batch, heads, seq, hw), BF16),
                   jax.ShapeDtypeStruct((batch, heads, nk, dva, tk), BF16)),
        grid=(batch, heads, nk),
        in_specs=[
            pl.BlockSpec((tk, hw), lambda bi, hi, si: (bi * nk + si, hi)),
            pl.BlockSpec((tk, hw), lambda bi, hi, si: (bi * nk + si, rot_block)),
            pl.BlockSpec((1, hw), const),
            pl.BlockSpec((1, hw), const),
            pl.BlockSpec((None, tk, hw), lambda bi, hi, si: (bi, si, 0)),
            pl.BlockSpec((None, tk, hw), lambda bi, hi, si: (bi, si, 0)),
            pl.BlockSpec((hw, hw), const),
            pl.BlockSpec((hw, hw), const),
        ],
        out_specs=(pl.BlockSpec((None, None, tk, hw), lambda bi, hi, si: (bi, hi, si, 0)),
                   pl.BlockSpec((None, None, None, dva, tk), lambda bi, hi, si: (bi, hi, si, 0, 0))),
        compiler_params=_params(("parallel", "parallel", "parallel")),
        name="mla_kv_prep",
    )(kv_raw, proj, g_nope.reshape(1, hw), g_rope.reshape(1, hw), cos_t, sin_t, perm, place)


def _out_proj_t_kernel(ot_ref, w_ref, x_ref, gate_ref, o_ref):
    h, dv, tm = ot_ref.shape
    y = ot_ref[...].reshape(h * dv, tm).astype(F32).T.astype(BF16)
    o_ref[...] = x_ref[...] + gate_ref[...] * _dot(y, w_ref[...])


def _out_proj_t(o_t, w, x, gate, seq, tm=512):
    b, h, dv, _ = o_t.shape
    t, d = x.shape
    tm = min(tm, seq)
    tpb = seq // tm
    return pl.pallas_call(
        _out_proj_t_kernel,
        out_shape=jax.ShapeDtypeStruct((t, d), F32),
        grid=(b, tpb),
        in_specs=[
            pl.BlockSpec((None, h, dv, tm), lambda bi, i: (bi, 0, 0, i)),
            pl.BlockSpec((h * dv, d), lambda bi, i: (0, 0)),
            pl.BlockSpec((tm, d), lambda bi, i: (bi * tpb + i, 0)),
            pl.BlockSpec((None, 1, d), lambda bi, i: (bi, 0, 0)),
        ],
        out_specs=pl.BlockSpec((tm, d), lambda bi, i: (bi * tpb + i, 0)),
        compiler_params=_params(("parallel", "parallel")),
        name="out_proj_t",
    )(o_t, w.astype(BF16), x, gate.reshape(b, 1, d))


def _compress_kernel(x_ref, pelo_ref, pehi_ref, w1lo_ref, w1hi_ref, w2_ref, g_ref, o_ref, *, norm):
    x = x_ref[...]
    nc = x.shape[0]
    a = _dot((x + pelo_ref[...]).astype(BF16), w1lo_ref[...])
    b = _dot((x + pehi_ref[...]).astype(BF16), w1hi_ref[...])
    hid = _silu(a + pltpu.roll(b, shift=nc - 1, axis=0))
    y = _dot(hid.astype(BF16), w2_ref[...])
    if norm:
        y = _rms(y, g_ref[...])
    o_ref[...] = y.astype(o_ref.dtype)


def _compress(x, pe, w1, w2, gain, norm):
    b, g, nc, wdt = x.shape
    dh = w2.shape[1]
    hidden = w1.shape[1]
    half = NSA_CMP_STRIDE
    pelo = pe[:half].reshape(1, wdt)
    pehi = pe[half:].reshape(1, wdt)
    w1 = w1.astype(BF16)
    const = lambda bi, gi: (0, 0)
    return pl.pallas_call(
        functools.partial(_compress_kernel, norm=norm),
        out_shape=jax.ShapeDtypeStruct((b, g, nc, dh), BF16),
        grid=(b, g),
        in_specs=[
            pl.BlockSpec((None, None, nc, wdt), lambda bi, gi: (bi, gi, 0, 0)),
            pl.BlockSpec((1, wdt), const),
            pl.BlockSpec((1, wdt), const),
            pl.BlockSpec((wdt, hidden), const),
            pl.BlockSpec((wdt, hidden), const),
            pl.BlockSpec((hidden, dh), const),
            pl.BlockSpec((1, dh), const),
        ],
        out_specs=pl.BlockSpec((None, None, nc, dh), lambda bi, gi: (bi, gi, 0, 0)),
        compiler_params=_params(("parallel", "parallel")),
        name="nsa_compress",
    )(x, pelo, pehi, w1[:wdt], w1[wdt:], w2.astype(BF16), gain.reshape(1, dh))


def _to_token_major(o_t, n_rep, tq):
    stacked = jnp.concatenate([o_t[:, r * tq:(r + 1) * tq] for r in range(n_rep)], axis=0)
    return stacked.T


def _nsa_cmp_kernel(qt_ref, kc_ref, vct_ref, band_ref, o_ref, sb_ref, s_scr, *, tq, n_rep):
    qi = pl.program_id(2)
    q0 = qi * tq
    nq = n_rep * tq
    n_ch, ch = s_scr.shape[0], s_scr.shape[1]
    dh = qt_ref.shape[0]
    nb = band_ref.shape[1]
    q_t = qt_ref[...]
    n_vis = (q0 + tq - NSA_CMP_STRIDE) // NSA_CMP_STRIDE
    n_used = (n_vis + ch - 1) // ch
    tpos = q0 + (lax.broadcasted_iota(jnp.int32, (ch, nq), 1) & (tq - 1))
    end0 = lax.broadcasted_iota(jnp.int32, (ch, nq), 0) * NSA_CMP_STRIDE + (NSA_CMP_LEN - 1)

    def scores(c, cm):
        s = _dot(kc_ref[pl.ds(pl.multiple_of(c * ch, ch), ch), :], q_t)
        s = jnp.where(end0 + c * (ch * NSA_CMP_STRIDE) <= tpos, s, -jnp.inf)
        s_scr[c] = s
        return jnp.maximum(cm, jnp.max(s, axis=0, keepdims=True))

    m = lax.fori_loop(0, n_used, scores, jnp.full((1, nq), -jnp.inf, F32))
    m = jnp.where(m > -jnp.inf, m, 0.0)

    def exps(c, l):
        p = jnp.exp(s_scr[c] - m)
        s_scr[c] = p
        return l + jnp.sum(p, axis=0, keepdims=True)

    l = lax.fori_loop(0, n_used, exps, jnp.zeros((1, nq), F32))
    inv = 1.0 / jnp.maximum(l, 1e-30)

    def outputs(c, carry):
        o_t, imp = carry
        p = s_scr[c] * inv
        o_t = o_t + _dot(vct_ref[c], p.astype(BF16))
        psum = p[:, 0:tq]
        for r in range(1, n_rep):
            psum = psum + p[:, r * tq:(r + 1) * tq]
        return o_t, imp + _dot3_left(band_ref[c], psum)

    o_t, imp = lax.fori_loop(0, n_used, outputs, (jnp.zeros((dh, nq), F32), jnp.zeros((nb, tq), F32)))
    o_ref[...] = _to_token_major(o_t, n_rep, tq).astype(o_ref.dtype)

    blk = lax.broadcasted_iota(jnp.int32, (nb, tq), 0)
    cur = (q0 + lax.broadcasted_iota(jnp.int32, (nb, tq), 1)) // NSA_SEL_LEN
    causal = blk <= cur
    forced = causal & ((blk == 0) | (blk > cur - NSA_N_LOCAL))
    score = jnp.where(forced, 1e30, jnp.where(causal, imp, -1.0))
    blk_f = blk.astype(F32)
    bias = jnp.full((nb, tq), SEL_MASK_BIAS, F32)
    for _ in range(NSA_N_SEL):
        best = jnp.max(score, axis=0, keepdims=True)
        idx = jnp.min(jnp.where(score == best, blk_f, float(nb)), axis=0, keepdims=True)
        pick = (blk_f == idx) & (best >= 0.0)
        bias = jnp.where(pick, 0.0, bias)
        score = jnp.where(pick, -1.0, score)
    sb_ref[...] = bias.astype(sb_ref.dtype)


NSA_CMP_CHUNK = 256


def _nsa_cmp_select(q_t, k_cmp, v_cmp, band, tq):
    b, g, n_q, dh, nq = q_t.shape
    r = nq // tq
    nc = k_cmp.shape[2]
    nb = band.shape[0]
    ch = min(NSA_CMP_CHUNK, nc)
    n_ch = nc // ch
    vc_t = jnp.swapaxes(v_cmp.reshape(b, g, n_ch, ch, dh), -1, -2)
    band_c = band.reshape(nb, n_ch, ch).transpose(1, 0, 2)
    return pl.pallas_call(
        functools.partial(_nsa_cmp_kernel, tq=tq, n_rep=r),
        out_shape=(jax.ShapeDtypeStruct((b * n_q * tq, g * r * dh), BF16),
                   jax.ShapeDtypeStruct((b, g, nb, n_q * tq), BF16)),
        grid=(b, g, n_q),
        in_specs=[
            pl.BlockSpec((None, None, None, dh, nq), lambda bi, gi, qi: (bi, gi, qi, 0, 0)),
            pl.BlockSpec((None, None, nc, dh), lambda bi, gi, qi: (bi, gi, 0, 0)),
            pl.BlockSpec((None, None, n_ch, dh, ch), lambda bi, gi, qi: (bi, gi, 0, 0, 0)),
            pl.BlockSpec((n_ch, nb, ch), lambda bi, gi, qi: (0, 0, 0)),
        ],
        out_specs=(pl.BlockSpec((tq, r * dh), lambda bi, gi, qi: (bi * n_q + qi, gi)),
                   pl.BlockSpec((None, None, nb, tq), lambda bi, gi, qi: (bi, gi, 0, qi))),
        scratch_shapes=[pltpu.VMEM((n_ch, ch, nq), F32)],
        compiler_params=_params(("parallel", "parallel", "parallel")),
        name="nsa_cmp_select",
    )(q_t, k_cmp, vc_t, band_c)


def _nsa_sel_kernel(qt_ref, sb_ref, k_ref, vt_ref, o_ref, qa_scr, *scratch, tq, tk, n_rep, tiles_per_half):
    qi = pl.program_id(2)
    q0 = qi * tq
    nq = n_rep * tq
    n_half = qa_scr.shape[0]
    q_t = qt_ref[...]
    for hh in range(n_half):
        sb = sb_ref[hh * SEL_ONEHOT:(hh + 1) * SEL_ONEHOT, :]
        qa_scr[hh] = jnp.concatenate([jnp.concatenate([sb] * n_rep, axis=1), q_t], axis=0)
    jd = q0 // tk
    kpos = jd * tk + lax.broadcasted_iota(jnp.int32, (tk, nq), 0)
    tpos = q0 + (lax.broadcasted_iota(jnp.int32, (tk, nq), 1) & (tq - 1))
    o = _flash_causal_t(lambda j: qa_scr[j // tiles_per_half], k_ref, vt_ref, scratch, jd, tk, kpos <= tpos)
    o_ref[...] = _to_token_major(o, n_rep, tq).astype(o_ref.dtype)


def _nsa_sel_attention(q_t, sel_bias, k_aug, v_t, tq, tk):
    b, g, n_q, dh, nq = q_t.shape
    nb, s = sel_bias.shape[2], sel_bias.shape[3]
    n_half = nb // SEL_ONEHOT
    ka = k_aug.shape[-1]
    nk, dva = v_t.shape[2], v_t.shape[3]
    tiles_per_half = SEL_ONEHOT * NSA_SEL_LEN // tk
    return pl.pallas_call(
        functools.partial(_nsa_sel_kernel, tq=tq, tk=tk, n_rep=nq // tq, tiles_per_half=tiles_per_half),
        out_shape=jax.ShapeDtypeStruct((b * s, g * (nq // tq) * dh), BF16),
        grid=(b, g, n_q),
        in_specs=[
            pl.BlockSpec((None, None, None, dh, nq), lambda bi, gi, qi: (bi, gi, qi, 0, 0)),
            pl.BlockSpec((None, None, nb, tq), lambda bi, gi, qi: (bi, gi, 0, qi)),
            pl.BlockSpec((None, None, s, ka), lambda bi, gi, qi: (bi, gi, 0, 0)),
            pl.BlockSpec((None, None, nk, dva, tk), lambda bi, gi, qi: (bi, gi, 0, 0, 0)),
        ],
        out_specs=pl.BlockSpec((tq, (nq // tq) * dh), lambda bi, gi, qi: (bi * n_q + qi, gi)),
        scratch_shapes=[pltpu.VMEM((n_half, ka, nq), BF16)] + _flash_scratch(tk, nq, dva - V_ONES_ROWS),
        compiler_params=_params(("parallel", "parallel", "arbitrary")),
        name="nsa_sel_flash",
    )(q_t, sel_bias, k_aug, v_t)


def _nsa_win_kernel(qt_ref, k_ref, vt_ref, o_ref, *, tq, win, n_rep):
    qi = pl.program_id(2)
    q0 = qi * tq
    nq = n_rep * tq
    dv = vt_ref.shape[-2] - V_ONES_ROWS
    n_chunk = (tq + win) // tq
    q_t = qt_ref[...]
    tpos = q0 + (lax.broadcasted_iota(jnp.int32, (tq, nq), 1) & (tq - 1))
    krow = lax.broadcasted_iota(jnp.int32, (tq, nq), 0)
    chunk_ids = [jnp.maximum(qi + c - (n_chunk - 1), 0) for c in range(n_chunk)]
    ss = []
    m = None
    for c in range(n_chunk):
        s = _dot(k_ref[pl.ds(pl.multiple_of(chunk_ids[c] * tq, tq), tq), :], q_t)
        kpos = q0 - win + c * tq + krow
        if c < n_chunk - 1:
            mask = (kpos > tpos - win) & (kpos >= 0)
        else:
            mask = kpos <= tpos
        s = jnp.where(mask, s, -jnp.inf)
        ss.append(s)
        cm = jnp.max(s, axis=0, keepdims=True)
        m = cm if m is None else jnp.maximum(m, cm)
    acc = None
    for c in range(n_chunk):
        d = _dot(vt_ref[chunk_ids[c]], jnp.exp(ss[c] - m).astype(BF16))
        acc = d if acc is None else acc + d
    o_t = acc[0:dv, :] / acc[dv:dv + 1, :]
    o_ref[...] = _to_token_major(o_t, n_rep, tq).astype(o_ref.dtype)


def _nsa_win_attention(q_t, k_pad, vt_pad, tq):
    b, g, n_q, dh, nq = q_t.shape
    sp = k_pad.shape[2]
    nkc, dva = vt_pad.shape[2], vt_pad.shape[3]
    r = nq // tq
    return pl.pallas_call(
        functools.partial(_nsa_win_kernel, tq=tq, win=NSA_WINDOW, n_rep=r),
        out_shape=jax.ShapeDtypeStruct((b * n_q * tq, g * r * dh), BF16),
        grid=(b, g, n_q),
        in_specs=[
            pl.BlockSpec((None, None, None, dh, nq), lambda bi, gi, qi: (bi, gi, qi, 0, 0)),
            pl.BlockSpec((None, None, sp, dh), lambda bi, gi, qi: (bi, gi, 0, 0)),
            pl.BlockSpec((None, None, nkc, dva, tq), lambda bi, gi, qi: (bi, gi, 0, 0, 0)),
        ],
        out_specs=pl.BlockSpec((tq, r * dh), lambda bi, gi, qi: (bi * n_q + qi, gi)),
        compiler_params=_params(("parallel", "parallel", "arbitrary")),
        name="nsa_window",
    )(q_t, k_pad, vt_pad)


def _nsa_out_kernel(oc_ref, os_ref, ow_ref, gl_ref, e_ref, w_ref, x_ref, gate_ref, o_ref):
    sg = _sigmoid(gl_ref[...].astype(F32))
    hi = sg.astype(BF16)
    lo = (sg - hi.astype(F32)).astype(BF16)

    def expand(i):
        return _dot(hi, e_ref[i]) + _dot(lo, e_ref[i])

    o = expand(0) * oc_ref[...] + expand(1) * os_ref[...] + expand(2) * ow_ref[...]
    o_ref[...] = x_ref[...] + gate_ref[...] * _dot(o.astype(BF16), w_ref[...])


def _nsa_out(o_cmp, o_sel, o_win, gate_logits, expand, w, x, gate, seq, tm=512):
    t, d = x.shape
    k = o_cmp.shape[1]
    gw = gate_logits.shape[1]
    b = gate.shape[0]
    tpb = seq // tm
    row = lambda i: (i, 0)
    return pl.pallas_call(
        _nsa_out_kernel,
        out_shape=jax.ShapeDtypeStruct((t, d), F32),
        grid=(t // tm,),
        in_specs=[
            pl.BlockSpec((tm, k), row),
            pl.BlockSpec((tm, k), row),
            pl.BlockSpec((tm, k), row),
            pl.BlockSpec((tm, gw), row),
            pl.BlockSpec((3, gw, k), lambda i: (0, 0, 0)),
            pl.BlockSpec((k, d), lambda i: (0, 0)),
            pl.BlockSpec((tm, d), row),
            pl.BlockSpec((None, 1, d), lambda i: (i // tpb, 0, 0)),
        ],
        out_specs=pl.BlockSpec((tm, d), row),
        compiler_params=_params(("parallel",)),
        name="nsa_out",
    )(o_cmp, o_sel, o_win, gate_logits, expand, w.astype(BF16), x, gate.reshape(b, 1, d))


def _ret_kernel(lg_ref, q_ref, k_ref, v_ref, g_ref, cos_ref, sin_ref, gn_ref, o_ref, st_scr):
    h = pl.program_id(1)
    c = pl.program_id(2)

    @pl.when(c == 0)
    def _():
        st_scr[...] = jnp.zeros_like(st_scr)

    cn = RET_CHUNK
    dk = q_ref.shape[1]
    half = dk // 2
    lg = jnp.full((1, 1), lg_ref[h], F32)
    n = lax.broadcasted_iota(jnp.int32, (cn, 1), 0).astype(F32)
    rel = (lax.broadcasted_iota(jnp.int32, (cn, cn), 0) - lax.broadcasted_iota(jnp.int32, (cn, cn), 1)).astype(F32)
    decay = jnp.where(rel >= 0, jnp.exp(jnp.maximum(rel, 0.0) * lg), 0.0)
    xi = jnp.exp((n + 1.0) * lg)
    zeta = jnp.exp((cn - 1.0 - n) * lg)
    gamma_c = jnp.exp(cn * lg)

    for sub in range(q_ref.shape[0] // cn):
        rows = slice(sub * cn, (sub + 1) * cn)
        cos = cos_ref[rows, :]
        sin = sin_ref[rows, :]

        def rope(x):
            x1, x2 = x[:, :half], x[:, half:]
            return jnp.concatenate([x1 * cos - x2 * sin, x1 * sin + x2 * cos], axis=1)

        q = rope(q_ref[rows, :].astype(F32))
        k = rope(k_ref[rows, :].astype(F32)) * (dk ** -0.5)
        v = v_ref[rows, :].astype(BF16)
        qb = q.astype(BF16)
        inner = _dot((_dot_nt(qb, k.astype(BF16)) * decay).astype(BF16), v)
        st = st_scr[...]
        cross = _dot(qb, st.astype(BF16)) * xi
        kz_t = (k * zeta).T.astype(BF16)
        st_scr[...] = gamma_c * st + _dot(kz_t, v)
        y = inner + cross
        mu = jnp.mean(y, axis=-1, keepdims=True)
        yc = y - mu
        var = jnp.mean(yc * yc, axis=-1, keepdims=True)
        yn = yc * lax.rsqrt(var + NORM_EPS) * gn_ref[...]
        o_ref[rows, :] = (_silu(g_ref[rows, :].astype(F32)) * yn).astype(o_ref.dtype)


def _retention(proj, cos_t, sin_t, log_gamma, gn_w, batch, seq, heads):
    t = proj.shape[0]
    dk, dv = RET_QK, RET_V
    cn = min(RET_STEP_CHUNKS * RET_CHUNK, seq)
    n_ch = seq // cn
    kb = heads * dk // dk
    vb = 2 * heads * dk // dv
    row = lambda bi, hi, ci, *_: bi * n_ch + ci
    grid_spec = pltpu.PrefetchScalarGridSpec(
        num_scalar_prefetch=1,
        grid=(batch, heads, n_ch),
        in_specs=[
            pl.BlockSpec((cn, dk), lambda bi, hi, ci, lg: (row(bi, hi, ci), hi)),
            pl.BlockSpec((cn, dk), lambda bi, hi, ci, lg: (row(bi, hi, ci), kb + hi)),
            pl.BlockSpec((cn, dv), lambda bi, hi, ci, lg: (row(bi, hi, ci), vb + hi)),
            pl.BlockSpec((cn, dv), lambda bi, hi, ci, lg: (row(bi, hi, ci), vb + heads + hi)),
            pl.BlockSpec((cn, dk // 2), lambda bi, hi, ci, lg: (row(bi, hi, ci), 0)),
            pl.BlockSpec((cn, dk // 2), lambda bi, hi, ci, lg: (row(bi, hi, ci), 0)),
            pl.BlockSpec((1, dv), lambda bi, hi, ci, lg: (0, hi)),
        ],
        out_specs=pl.BlockSpec((cn, dv), lambda bi, hi, ci, lg: (row(bi, hi, ci), hi)),
        scratch_shapes=[pltpu.VMEM((dk, dv), F32)],
    )
    return pl.pallas_call(
        _ret_kernel,
        out_shape=jax.ShapeDtypeStruct((t, heads * dv), BF16),
        grid_spec=grid_spec,
        compiler_params=_params(("parallel", "parallel", "arbitrary")),
        name="retention",
    )(log_gamma, proj, proj, proj, proj, cos_t, sin_t, gn_w.reshape(1, heads * dv))


def _softplus(x):
    return jnp.maximum(x, 0.0) + jnp.log1p(jnp.exp(-jnp.abs(x)))


def _ssd_kernel(row_ref, dt_ref, dtt_ref, cw_ref, cb_ref, dtb_ref, dtbt_ref, al_ref, alt_ref, dsk_ref, nw_ref,
                ex_ref, o_ref, buf_scr, st_scr, *, inner, groups, state, heads):
    c = pl.program_id(1)
    cn = row_ref.shape[0]
    conv_ch = inner + 2 * groups * state
    hp = inner // heads
    rep = heads // groups
    gw = rep * hp
    halo = 8

    @pl.when(c == 0)
    def _():
        buf_scr[0:halo, :] = jnp.zeros((halo, conv_ch), F32)
        st_scr[...] = jnp.zeros_like(st_scr)

    z = row_ref[:, 0:inner].astype(F32)
    xbc = row_ref[:, inner:inner + conv_ch].astype(F32)
    dt_raw = dt_ref[...]

    buf_scr[halo:halo + cn, :] = xbc
    conv = cb_ref[...] + cw_ref[0:1, :] * buf_scr[halo - 3:halo - 3 + cn, :]
    for kk in range(1, SSD_CONV):
        conv = conv + cw_ref[kk:kk + 1, :] * buf_scr[halo - 3 + kk:halo - 3 + kk + cn, :]
    buf_scr[0:halo, :] = xbc[cn - halo:cn, :]
    act = _silu(conv)
    xs = act[:, 0:inner]
    bm = act[:, inner:inner + groups * state]
    cm = act[:, inner + groups * state:conv_ch]

    dt = _softplus(dt_raw + dtb_ref[...])
    dt_t = _softplus(dtt_ref[...] + dtbt_ref[...])
    a = -jnp.exp(al_ref[...])
    a_t = -jnp.exp(alt_ref[...])
    ri = lax.broadcasted_iota(jnp.int32, (cn, cn), 0)
    ci = lax.broadcasted_iota(jnp.int32, (cn, cn), 1)
    tril = ci <= ri
    lower = jnp.where(tril, 1.0, 0.0).astype(BF16)
    upper = jnp.where(ci >= ri, 1.0, 0.0).astype(BF16)
    cum = _dot3_left(lower, dt * a)
    cum_t = _dot3(dt_t * a_t, upper)
    cum_last = cum[cn - 1:cn, :]

    ex = ex_ref[...]
    e_cum = _dot3(jnp.exp(cum), ex)
    w_end = _dot3(jnp.exp(cum_last - cum) * dt, ex)
    e_last = _dot3(jnp.exp(cum_last), ex)

    lane = lax.broadcasted_iota(jnp.int32, (cn, 2 * hp), 1)
    ys = []
    for g in range(groups):
        cc = cm[:, g * state:(g + 1) * state].astype(BF16)
        bc = bm[:, g * state:(g + 1) * state].astype(BF16)
        cb = _dot_nt(cc, bc)
        xg = xs[:, g * gw:(g + 1) * gw]
        st = st_scr[g]
        cross = _dot(cc, st.astype(BF16)) * e_cum[:, g * gw:(g + 1) * gw]
        xw_t = bc.astype(F32).T.astype(BF16)
        st_scr[g] = e_last[:, g * gw:(g + 1) * gw] * st + _dot(
            xw_t, (xg * w_end[:, g * gw:(g + 1) * gw]).astype(BF16))
        intra = []
        for pr in range(rep // 2):
            ws = []
            for hh in (2 * pr, 2 * pr + 1):
                hd = g * rep + hh
                seg = jnp.where(tril, cum[:, hd:hd + 1] - cum_t[hd:hd + 1, :], -jnp.inf)
                ws.append((jnp.exp(seg) * cb * dt_t[hd:hd + 1, :]).astype(BF16))
            w2 = jnp.concatenate(ws, axis=1)
            xp = xg[:, 2 * pr * hp:(2 * pr + 2) * hp]
            x2 = jnp.concatenate([jnp.where(lane < hp, xp, 0.0), jnp.where(lane >= hp, xp, 0.0)], axis=0)
            intra.append(_dot(w2, x2.astype(BF16)))
        ys.append(jnp.concatenate(intra, axis=1) + cross)
    y = jnp.concatenate(ys, axis=1) + dsk_ref[...] * xs
    y = y * _silu(z)
    outs = []
    for g in range(groups):
        outs.append(_rms(y[:, g * gw:(g + 1) * gw], nw_ref[:, g * gw:(g + 1) * gw]))
    o_ref[...] = jnp.concatenate(outs, axis=1).astype(o_ref.dtype)


def _ssd(proj, dt_raw, conv_w, conv_b, dt_bias, a_log, d_skip, norm_w, batch, seq, inner, heads):
    t, n_pad = proj.shape
    groups, state, cn = SSD_GROUPS, SSD_STATE, SSD_CHUNK
    conv_ch = inner + 2 * groups * state
    n_ch = seq // cn
    hp = inner // heads
    gw = inner // groups
    hpad = dt_raw.shape[1]
    assert heads <= hpad and n_pad >= inner + conv_ch
    dt_t = dt_raw.reshape(batch, seq, hpad).transpose(0, 2, 1)
    ex_np = np.zeros((hpad, inner), np.float32)
    ex_np[:heads] = np.kron(np.eye(heads, dtype=np.float32), np.ones((1, hp), np.float32))
    ex = jnp.asarray(ex_np, BF16)
    dt_bias = jnp.pad(dt_bias, (0, hpad - heads))
    a_log = jnp.pad(a_log, (0, hpad - heads))
    heads_k = heads
    heads = hpad
    const = lambda bi, ci: (0, 0)
    return pl.pallas_call(
        functools.partial(_ssd_kernel, inner=inner, groups=groups, state=state, heads=heads_k),
        out_shape=jax.ShapeDtypeStruct((t, inner), BF16),
        grid=(batch, n_ch),
        in_specs=[
            pl.BlockSpec((cn, n_pad), lambda bi, ci: (bi * n_ch + ci, 0)),
            pl.BlockSpec((cn, heads), lambda bi, ci: (bi * n_ch + ci, 0)),
            pl.BlockSpec((None, heads, cn), lambda bi, ci: (bi, 0, ci)),
            pl.BlockSpec((SSD_CONV, conv_ch), const),
            pl.BlockSpec((1, conv_ch), const),
            pl.BlockSpec((1, heads), const),
            pl.BlockSpec((heads, 1), const),
            pl.BlockSpec((1, heads), const),
            pl.BlockSpec((heads, 1), const),
            pl.BlockSpec((1, inner), const),
            pl.BlockSpec((1, inner), const),
            pl.BlockSpec((heads, inner), const),
        ],
        out_specs=pl.BlockSpec((cn, inner), lambda bi, ci: (bi * n_ch + ci, 0)),
        scratch_shapes=[pltpu.VMEM((8 + cn, conv_ch), F32), pltpu.VMEM((groups, state, gw), F32)],
        compiler_params=_params(("parallel", "arbitrary")),
        name="ssd_scan",
    )(proj, dt_raw, dt_t, conv_w, conv_b.reshape(1, conv_ch), dt_bias.reshape(1, heads), dt_bias.reshape(heads, 1),
      a_log.reshape(1, heads), a_log.reshape(heads, 1), jnp.repeat(d_skip, hp).reshape(1, inner),
      norm_w.reshape(1, inner), ex)


def _nsa_layer(x, mods, norm_g, batch, seq, w_in, q_norm, k_norm, cmp_pe, cmp_w1, cmp_w2, w_out):
    sh, sc, gate = mods
    d = x.shape[1]
    dh, g = HEAD_DIM, NSA_GROUPS
    heads = d // dh
    r = heads // g
    kvw = g * dh
    proj = _modnorm_matmul(x, norm_g, sc, sh, w_in, seq, out_dtype=BF16)
    off = heads * dh

    def heads_major(cols, n_heads):
        return cols.reshape(batch, seq, n_heads, dh).transpose(0, 2, 1, 3)

    q = heads_major(proj[:, :off], heads)
    parts = [proj[:, off + i * kvw: off + (i + 1) * kvw] for i in range(6)]
    k_c, v_c, k_s, v_s, k_w, v_w = parts
    gate_logits = proj[:, off + 6 * kvw: off + 6 * kvw + LANES]

    tq, tk = NSA_TQ, min(NSA_TK, seq)
    n_q = seq // tq
    qn = _norm_rope(q, q_norm, scale=dh ** -0.5)
    q_t = qn.reshape(batch, g, r, n_q, tq, dh).transpose(0, 1, 3, 5, 2, 4).reshape(batch, g, n_q, dh, r * tq)

    def chunk_rows(cols):
        st = NSA_CMP_STRIDE
        return cols.reshape(batch, seq // st, st, g, dh).transpose(0, 3, 1, 2, 4).reshape(batch, g, seq // st, st * dh)

    k_cmp = _compress(chunk_rows(k_c), cmp_pe[0], cmp_w1[0], cmp_w2[0], k_norm[0], norm=True)
    v_cmp = _compress(chunk_rows(v_c), cmp_pe[1], cmp_w1[1], cmp_w2[1], k_norm[0], norm=False)

    n_blk = seq // NSA_SEL_LEN
    n_cmp = seq // NSA_CMP_STRIDE
    assert n_blk % SEL_ONEHOT == 0
    ratio = NSA_SEL_LEN // NSA_CMP_STRIDE
    jj = np.arange(n_blk)[:, None]
    cc = np.arange(n_cmp)[None, :]
    band = jnp.asarray(((cc >= ratio * jj - 1) & (cc <= ratio * jj + ratio - 1)).astype(np.float32), BF16)
    o_cmp, sel_bias = _nsa_cmp_select(q_t, k_cmp, v_cmp, band, tq)

    ks_n = _norm_rope(heads_major(k_s, g), k_norm[1])
    onehot = jnp.asarray(np.eye(SEL_ONEHOT, dtype=np.float32)[(np.arange(seq) // NSA_SEL_LEN) % SEL_ONEHOT], BF16)
    k_aug = jnp.concatenate([jnp.broadcast_to(onehot, (batch, g, seq, SEL_ONEHOT)), ks_n], axis=-1)
    o_sel = _nsa_sel_attention(q_t, sel_bias, k_aug, _with_ones_rows_t(heads_major(v_s, g), tk), tq, tk)

    kw_n = _norm_rope(heads_major(k_w, g), k_norm[2])
    o_win = _nsa_win_attention(q_t, kw_n, _with_ones_rows_t(heads_major(v_w, g), tq), tq)

    e = np.zeros((3, LANES, heads * dh), np.float32)
    for hd in range(heads):
        for i in range(3):
            e[i, hd * 3 + i, hd * dh:(hd + 1) * dh] = 1.0
    return _nsa_out(o_cmp, o_sel, o_win, gate_logits, jnp.asarray(e, BF16), w_out, x, gate, seq)


def _mla_layer(x, mods, norm_g, batch, seq, pos, w_in, q_a_norm, kv_a_norm, w_q_b, w_kv_b, q_norm, k_norm, w_out):
    sh, sc, gate = mods
    d = x.shape[1]
    heads = w_out.shape[0] // MLA_V
    dq = MLA_NOPE + MLA_ROPE
    proj = _modnorm_matmul(x, norm_g, sc, sh, w_in, seq, out_dtype=BF16)
    cq = proj[:, :MLA_Q_LORA]
    ckv = proj[:, MLA_Q_LORA:MLA_Q_LORA + MLA_KV_LORA]
    k_rot = proj[:, MLA_Q_LORA + MLA_KV_LORA:MLA_Q_LORA + MLA_KV_LORA + MLA_ROPE]
    assert MLA_NOPE + MLA_V == LANES and dq <= LANES
    w_q_pad = jnp.pad(w_q_b.reshape(MLA_Q_LORA, heads, dq), ((0, 0), (0, 0), (0, LANES - dq)))
    q_raw = _norm_matmul(cq, q_a_norm, w_q_pad.reshape(MLA_Q_LORA, heads * LANES), out_dtype=BF16)
    kv_raw = _norm_matmul(ckv, kv_a_norm, w_kv_b, out_dtype=BF16)

    half = MLA_ROPE // 2
    cos, sin = _rope_tables(pos, MLA_ROPE)
    ones = jnp.ones((batch, seq, MLA_NOPE), F32)
    zq = jnp.zeros((batch, seq, LANES - dq), F32)
    zk = jnp.zeros((batch, seq, LANES - MLA_ROPE), F32)
    cos_q = jnp.concatenate([ones, cos, cos, zq], axis=-1)
    sin_q = jnp.concatenate([0.0 * ones, sin, sin, zq], axis=-1)
    cos_k = jnp.concatenate([cos, cos, zk], axis=-1)
    sin_k = jnp.concatenate([sin, sin, zk], axis=-1)
    pad_to = lambda v: jnp.pad(v, (0, LANES - v.shape[0]))
    tq = min(MLA_TQ, seq)
    tk = min(MLA_TK, tq)
    q_t = _mla_q_prep(q_raw, pad_to(q_norm), cos_q, sin_q, _rotate_half_perm(LANES, MLA_NOPE, half),
                      batch, seq, heads, n_real=dq, scale=dq ** -0.5)
    place = np.zeros((LANES, LANES), np.float32)
    place[np.arange(MLA_ROPE), MLA_NOPE + np.arange(MLA_ROPE)] = 1.0
    rot_block = (MLA_Q_LORA + MLA_KV_LORA) // LANES
    assert rot_block * LANES == MLA_Q_LORA + MLA_KV_LORA and proj.shape[1] >= (rot_block + 1) * LANES
    k, v_t = _mla_kv_prep(kv_raw, proj, rot_block, pad_to(k_norm[:MLA_NOPE]), pad_to(k_norm[MLA_NOPE:]),
                          cos_k, sin_k, _rotate_half_perm(LANES, 0, half), jnp.asarray(place, BF16),
                          batch, seq, heads, tk)
    o_t = _mla_attention(q_t, k, v_t, tq, tk)
    return _out_proj_t(o_t, w_out, x, gate, seq)


def _ret_layer(x, mods, norm_g, batch, seq, pos, w_in, gn_w, w_out):
    sh, sc, gate = mods
    heads = w_out.shape[0] // RET_V
    proj = _modnorm_matmul(x, norm_g, sc, sh, w_in, seq, out_dtype=BF16)
    cos, sin = _rope_tables(pos, RET_QK)
    cos = cos.reshape(batch * seq, RET_QK // 2)
    sin = sin.reshape(batch * seq, RET_QK // 2)
    log_gamma = jnp.log1p(-(2.0 ** (-5.0 - jnp.arange(heads, dtype=F32))))
    y = _retention(proj, cos, sin, log_gamma, gn_w, batch, seq, heads)
    return _out_proj(y, w_out, x, gate, seq)


def _ssd_layer(x, mods, norm_g, batch, seq, w_in, conv_w, conv_b, dt_bias, a_log, d_skip, norm_w, w_out):
    sh, sc, gate = mods
    inner = w_out.shape[0]
    heads = dt_bias.shape[0]
    conv_ch = conv_w.shape[1]
    proj = _modnorm_matmul(x, norm_g, sc, sh, w_in[:, :inner + conv_ch], seq, out_dtype=BF16)
    dt_raw = _modnorm_matmul(x, norm_g, sc, sh, w_in[:, inner + conv_ch:], seq)
    y = _ssd(proj, dt_raw, conv_w, conv_b, dt_bias, a_log, d_skip, norm_w, batch, seq, inner, heads)
    return _out_proj(y, w_out, x, gate, seq)


def kernel(x, c, positions, ada_w, ada_b, norm_mix, norm_ffn, ffn_w_in, ffn_w_out, nsa_w_in, nsa_q_norm, nsa_k_norm, nsa_cmp_pe, nsa_cmp_w1, nsa_cmp_w2, nsa_w_out, mla_w_in, mla_q_a_norm, mla_kv_a_norm, mla_w_q_b, mla_w_kv_b, mla_q_norm, mla_k_norm, mla_w_out, ret_w_in, ret_gn_w, ret_w_out, ssd_w_in, ssd_conv_w, ssd_conv_b, ssd_dt_bias, ssd_a_log, ssd_d, ssd_norm, ssd_w_out):
    batch, seq, d = x.shape
    depth = ada_w.shape[0]
    n_mixers = 4
    mod = _modulation(c, ada_w, ada_b)
    xt = x.reshape(batch * seq, d)
    for i in range(depth):
        sh_m, sc_m, g_m, sh_f, sc_f, g_f = [mod[i, :, k * d:(k + 1) * d] for k in range(6)]
        mods = (sh_m, sc_m, g_m)
        kind, j = i % n_mixers, i // n_mixers
        if kind == 0:
            xt = _nsa_layer(xt, mods, norm_mix[i], batch, seq, nsa_w_in[j], nsa_q_norm[j], nsa_k_norm[j],
                            nsa_cmp_pe[j], nsa_cmp_w1[j], nsa_cmp_w2[j], nsa_w_out[j])
        elif kind == 1:
            xt = _mla_layer(xt, mods, norm_mix[i], batch, seq, positions, mla_w_in[j], mla_q_a_norm[j],
                            mla_kv_a_norm[j], mla_w_q_b[j], mla_w_kv_b[j], mla_q_norm[j], mla_k_norm[j], mla_w_out[j])
        elif kind == 2:
            xt = _ret_layer(xt, mods, norm_mix[i], batch, seq, positions, ret_w_in[j], ret_gn_w[j], ret_w_out[j])
        else:
            xt = _ssd_layer(xt, mods, norm_mix[i], batch, seq, ssd_w_in[j], ssd_conv_w[j], ssd_conv_b[j],
                            ssd_dt_bias[j], ssd_a_log[j], ssd_d[j], ssd_norm[j], ssd_w_out[j])
        xt = _ffn(xt, norm_ffn[i], sc_f, sh_f, g_f, ffn_w_in[i], ffn_w_out[i], seq)
    return xt.reshape(batch, seq, d)
```

```python
import functools
import math

import numpy as np
import jax
import jax.numpy as jnp
from jax import lax
from jax.experimental import pallas as pl
from jax.experimental.pallas import tpu as pltpu

F32 = jnp.float32
BF16 = jnp.bfloat16

NORM_EPS = 1e-6
ROPE_THETA = 10000.0
LOG2E = math.log2(math.e)
LANES = 128

HEAD_DIM = 64
NSA_GROUPS = 4
NSA_CMP_STRIDE = 16
NSA_CMP_LEN = 32
NSA_SEL_LEN = 64
NSA_N_SEL = 16
NSA_N_LOCAL = 2
NSA_WINDOW = 512
NSA_TQ = 128
NSA_TK = 512
SEL_ONEHOT = 128
SEL_MASK_BIAS = -32768.0

MLA_Q_LORA = 384
MLA_KV_LORA = 256
MLA_NOPE = 64
MLA_ROPE = 32
MLA_V = 64
MLA_TQ = 512
MLA_TK = 512

RET_QK = 256
RET_V = 512
RET_CHUNK = 128
RET_STEP_CHUNKS = 4

SSD_HEADDIM = 64
SSD_GROUPS = 4
SSD_STATE = 128
SSD_CONV = 4
SSD_CHUNK = 128

VMEM_LIMIT = 48 * 1024 * 1024


def _params(sem, vmem=VMEM_LIMIT, flags=None):
    return pltpu.CompilerParams(dimension_semantics=sem, vmem_limit_bytes=vmem, flags=flags)


def _sigmoid(x):
    return 1.0 / (1.0 + jnp.exp(-x))


def _silu(x):
    return x * _sigmoid(x)


def _dot(a, b):
    return jnp.dot(a, b, preferred_element_type=F32)


def _dot_nt(a, b):
    return lax.dot_general(a, b, (((1,), (1,)), ((), ())), preferred_element_type=F32)


def _split3(x):
    hi = x.astype(BF16)
    r1 = x - hi.astype(F32)
    mid = r1.astype(BF16)
    lo = (r1 - mid.astype(F32)).astype(BF16)
    return hi, mid, lo


def _dot3(x, m):
    hi, mid, lo = _split3(x)
    return _dot(hi, m) + _dot(mid, m) + _dot(lo, m)


def _dot3_left(m, x):
    hi, mid, lo = _split3(x)
    return _dot(m, hi) + _dot(m, mid) + _dot(m, lo)


def _rms(x, gain):
    ms = jnp.mean(x * x, axis=-1, keepdims=True)
    return x * lax.rsqrt(ms + NORM_EPS) * gain


def _mod_kernel(c_ref, w_ref, b_ref, o_ref):
    cond = _silu(c_ref[...]).astype(BF16)
    o_ref[...] = _dot(cond, w_ref[...]) + b_ref[...]


def _modulation(c, ada_w, ada_b):
    depth, d, n = ada_w.shape
    b = c.shape[0]
    rows = 16
    c_pad = jnp.zeros((rows, d), F32).at[:b].set(c)
    tn = 1024
    out = pl.pallas_call(
        _mod_kernel,
        out_shape=jax.ShapeDtypeStruct((depth, rows, n), F32),
        grid=(depth, n // tn),
        in_specs=[
            pl.BlockSpec((rows, d), lambda l, j: (0, 0)),
            pl.BlockSpec((None, d, tn), lambda l, j: (l, 0, j)),
            pl.BlockSpec((None, 1, tn), lambda l, j: (l, 0, j)),
        ],
        out_specs=pl.BlockSpec((None, rows, tn), lambda l, j: (l, 0, j)),
        compiler_params=_params(("parallel", "parallel")),
        name="adaln_mod",
    )(c_pad, ada_w.astype(BF16), ada_b.reshape(depth, 1, n))
    return out[:, :b]


def _modnorm_matmul_kernel(x_ref, g_ref, sc_ref, sh_ref, w_ref, o_ref, h_scr):
    @pl.when(pl.program_id(1) == 0)
    def _():
        y = _rms(x_ref[...].astype(F32), g_ref[...])
        h_scr[...] = (y * (1.0 + sc_ref[...]) + sh_ref[...]).astype(BF16)

    o_ref[...] = _dot(h_scr[...], w_ref[...]).astype(o_ref.dtype)


def _pad_and_tile(n, max_tile=1024, min_tile=512):
    n_pad = -(-n // LANES) * LANES
    while True:
        if n_pad <= max_tile:
            return n_pad, n_pad
        for tn in range(max_tile, min_tile - 1, -LANES):
            if n_pad % tn == 0:
                return n_pad, tn
        n_pad += LANES


def _modnorm_matmul(x, gain, sc, sh, w, seq, out_dtype=F32, tm=1024):
    t, d = x.shape
    tm = min(tm, seq)
    n = w.shape[1]
    n_pad, tn = _pad_and_tile(n)
    w = w.astype(BF16)
    if n_pad != n:
        w = jnp.pad(w, ((0, 0), (0, n_pad - n)))
    tpb = seq // tm
    b = sc.shape[0]
    return pl.pallas_call(
        _modnorm_matmul_kernel,
        out_shape=jax.ShapeDtypeStruct((t, n_pad), out_dtype),
        grid=(t // tm, n_pad // tn),
        in_specs=[
            pl.BlockSpec((tm, d), lambda i, j: (i, 0)),
            pl.BlockSpec((1, d), lambda i, j: (0, 0)),
            pl.BlockSpec((None, 1, d), lambda i, j: (i // tpb, 0, 0)),
            pl.BlockSpec((None, 1, d), lambda i, j: (i // tpb, 0, 0)),
            pl.BlockSpec((d, tn), lambda i, j: (0, j)),
        ],
        out_specs=pl.BlockSpec((tm, tn), lambda i, j: (i, j)),
        scratch_shapes=[pltpu.VMEM((tm, d), BF16)],
        compiler_params=_params(("parallel", "arbitrary")),
        name="modnorm_matmul",
    )(x, gain.reshape(1, d), sc.reshape(b, 1, d), sh.reshape(b, 1, d), w)


def _norm_matmul(x, gain, w, out_dtype=F32):
    t, d = x.shape
    zeros = jnp.zeros((1, d), F32)
    return _modnorm_matmul(x, gain, zeros, zeros, w, seq=t, out_dtype=out_dtype)


def _out_proj_kernel(y_ref, w_ref, x_ref, gate_ref, o_ref):
    o_ref[...] = x_ref[...] + gate_ref[...] * _dot(y_ref[...], w_ref[...])


def _out_proj(y, w, x, gate, seq, tm=512):
    t, k = y.shape
    d = w.shape[1]
    b = gate.shape[0]
    tpb = seq // tm
    return pl.pallas_call(
        _out_proj_kernel,
        out_shape=jax.ShapeDtypeStruct((t, d), F32),
        grid=(t // tm,),
        in_specs=[
            pl.BlockSpec((tm, k), lambda i: (i, 0)),
            pl.BlockSpec((k, d), lambda i: (0, 0)),
            pl.BlockSpec((tm, d), lambda i: (i, 0)),
            pl.BlockSpec((None, 1, d), lambda i: (i // tpb, 0, 0)),
        ],
        out_specs=pl.BlockSpec((tm, d), lambda i: (i, 0)),
        compiler_params=_params(("parallel",)),
        name="out_proj",
    )(y, w.astype(BF16), x, gate.reshape(b, 1, d))


def _ffn_kernel(x_ref, g_ref, sc_ref, sh_ref, wa_ref, wb_ref, wo_ref, gate_ref, o_ref, h_scr, acc_scr):
    j = pl.program_id(1)

    @pl.when(j == 0)
    def _():
        y = _rms(x_ref[...], g_ref[...])
        h_scr[...] = (y * (1.0 + sc_ref[...]) + sh_ref[...]).astype(BF16)
        acc_scr[...] = jnp.zeros_like(acc_scr)

    h = h_scr[...]
    a = _dot(h, wa_ref[...])
    b = _dot(h, wb_ref[...])
    u = (_silu(a) * b).astype(BF16)
    acc_scr[...] += _dot(u, wo_ref[...])

    @pl.when(j == pl.num_programs(1) - 1)
    def _():
        o_ref[...] = x_ref[...] + gate_ref[...] * acc_scr[...]


def _ffn(x, gain, sc, sh, gate, w_in, w_out, seq, tm=512):
    t, d = x.shape
    hid = w_out.shape[0]
    n_h = 2
    th = hid // n_h
    assert th % LANES == 0
    b = sc.shape[0]
    tpb = seq // tm
    w_in = w_in.astype(BF16)
    w_out = w_out.astype(BF16)
    return pl.pallas_call(
        _ffn_kernel,
        out_shape=jax.ShapeDtypeStruct((t, d), F32),
        grid=(t // tm, n_h),
        in_specs=[
            pl.BlockSpec((tm, d), lambda i, j: (i, 0)),
            pl.BlockSpec((1, d), lambda i, j: (0, 0)),
            pl.BlockSpec((None, 1, d), lambda i, j: (i // tpb, 0, 0)),
            pl.BlockSpec((None, 1, d), lambda i, j: (i // tpb, 0, 0)),
            pl.BlockSpec((d, th), lambda i, j: (0, j)),
            pl.BlockSpec((d, th), lambda i, j: (0, n_h + j)),
            pl.BlockSpec((th, d), lambda i, j: (j, 0)),
            pl.BlockSpec((None, 1, d), lambda i, j: (i // tpb, 0, 0)),
        ],
        out_specs=pl.BlockSpec((tm, d), lambda i, j: (i, 0)),
        scratch_shapes=[pltpu.VMEM((tm, d), BF16), pltpu.VMEM((tm, d), F32)],
        compiler_params=_params(("parallel", "arbitrary")),
        name="ffn",
    )(x, gain.reshape(1, d), sc.reshape(b, 1, d), sh.reshape(b, 1, d), w_in, w_in, w_out,
      gate.reshape(b, 1, d))


def _norm_rope_kernel(*refs, scale, rope):
    if rope:
        x_ref, g_ref, c_ref, s_ref, p_ref, o_ref = refs
    else:
        x_ref, g_ref, o_ref = refs
    y = _rms(x_ref[...].astype(F32), g_ref[...])
    if rope:
        hi = y.astype(BF16)
        lo = (y - hi.astype(F32)).astype(BF16)
        rot = _dot(hi, p_ref[...]) + _dot(lo, p_ref[...])
        y = y * c_ref[...] + rot * s_ref[...]
    if scale != 1.0:
        y = y * scale
    o_ref[...] = y.astype(o_ref.dtype)


def _norm_rope(x, gain, cos_t=None, sin_t=None, perm=None, scale=1.0, out_dtype=BF16, ts=2048):
    b, h, s, d = x.shape
    ts = min(ts, s)
    rope = cos_t is not None
    in_specs = [
        pl.BlockSpec((None, None, ts, d), lambda bi, hi, si: (bi, hi, si, 0)),
        pl.BlockSpec((1, d), lambda bi, hi, si: (0, 0)),
    ]
    args = [x, gain.reshape(1, d)]
    if rope:
        in_specs += [
            pl.BlockSpec((None, ts, d), lambda bi, hi, si: (bi, si, 0)),
            pl.BlockSpec((None, ts, d), lambda bi, hi, si: (bi, si, 0)),
            pl.BlockSpec((d, d), lambda bi, hi, si: (0, 0)),
        ]
        args += [cos_t, sin_t, perm]
    return pl.pallas_call(
        functools.partial(_norm_rope_kernel, scale=scale, rope=rope),
        out_shape=jax.ShapeDtypeStruct((b, h, s, d), out_dtype),
        grid=(b, h, s // ts),
        in_specs=in_specs,
        out_specs=pl.BlockSpec((None, None, ts, d), lambda bi, hi, si: (bi, hi, si, 0)),
        compiler_params=_params(("parallel", "parallel", "parallel")),
        name="head_norm_rope" if rope else "head_norm",
    )(*args)


def _rope_tables(pos, d):
    inv = ROPE_THETA ** (-jnp.arange(0, d, 2, dtype=F32) / d)
    ang = pos.astype(F32)[..., None] * inv
    return jnp.cos(ang), jnp.sin(ang)


def _rotate_half_perm(d_total, start, half):
    p = np.zeros((d_total, d_total), np.float32)
    for i in range(half):
        p[start + half + i, start + i] = -1.0
        p[start + i, start + half + i] = 1.0
    return jnp.asarray(p, BF16)


V_ONES_ROWS = 16
FLASH_STRIP = 256
FLASH_UNROLL = 4


def _flash_scratch(tk, nq, dv):
    return [pltpu.VMEM((2, tk, nq), F32), pltpu.VMEM((1, nq), F32), pltpu.VMEM((1, nq), F32),
            pltpu.VMEM((dv + V_ONES_ROWS, nq), F32)]


def _flash_causal_t(get_q_t, k_ref, vt_ref, scratch, n_full, tk, last_mask):
    s_scr, m_scr, cm_scr, acc_scr = scratch
    dv = acc_scr.shape[0] - V_ONES_ROWS
    strips = [slice(r * FLASH_STRIP, (r + 1) * FLASH_STRIP) for r in range(tk // FLASH_STRIP)]

    def scores(j, slot):
        cm = None
        for r, rows in enumerate(strips):
            start = pl.multiple_of(j * tk + r * FLASH_STRIP, FLASH_STRIP)
            s = _dot(k_ref[pl.ds(start, FLASH_STRIP), :], get_q_t(j))
            s_scr[slot, rows, :] = s
            c = jnp.max(s, axis=0, keepdims=True)
            cm = c if cm is None else jnp.maximum(cm, c)
        return cm

    def stage(j, cur, nxt):
        cm_next = scores(j + 1, nxt)
        m = m_scr[...]
        m_new = jnp.maximum(m, cm_scr[...])
        pv = None
        for rows in strips:
            p = jnp.exp2(s_scr[cur, rows, :] - m_new).astype(BF16)
            d = _dot(vt_ref[j, :, rows], p)
            pv = d if pv is None else pv + d
        acc_scr[...] = jnp.exp2(m - m_new) * acc_scr[...] + pv
        m_scr[...] = m_new
        cm_scr[...] = cm_next

    m_scr[...] = jnp.full(m_scr.shape, -jnp.inf, F32)
    acc_scr[...] = jnp.zeros(acc_scr.shape, F32)
    cm_scr[...] = scores(0, 0)

    def group(i, carry):
        for u in range(FLASH_UNROLL):
            stage(FLASH_UNROLL * i + u, u % 2, (u + 1) % 2)
        return carry

    n_groups = n_full // FLASH_UNROLL
    lax.fori_loop(0, n_groups, group, 0)
    for u in range(FLASH_UNROLL - 1):

        @pl.when(n_full - n_groups * FLASH_UNROLL > u)
        def _():
            stage(n_groups * FLASH_UNROLL + u, u % 2, (u + 1) % 2)

    s = jnp.where(last_mask, s_scr[n_full % 2], -jnp.inf)
    m = m_scr[...]
    m_new = jnp.maximum(m, jnp.max(s, axis=0, keepdims=True))
    p = jnp.exp2(s - m_new).astype(BF16)
    acc = jnp.exp2(m - m_new) * acc_scr[...] + _dot(vt_ref[n_full], p)
    return acc[0:dv, :] / acc[dv:dv + 1, :]


def _with_ones_rows_t(v, tk):
    *lead, s, dv = v.shape
    extra = jnp.zeros((*lead, s, V_ONES_ROWS), v.dtype).at[..., 0].set(1.0)
    va = jnp.concatenate([v, extra], axis=-1).astype(BF16)
    va = va.reshape(*lead, s // tk, tk, dv + V_ONES_ROWS)
    return jnp.swapaxes(va, -1, -2)


def _mla_attn_kernel(qt_ref, k_ref, vt_ref, o_ref, *scratch, tk):
    qi = pl.program_id(2)
    tq = qt_ref.shape[1]
    kpos = lax.broadcasted_iota(jnp.int32, (tk, tq), 0)
    qpos = lax.broadcasted_iota(jnp.int32, (tk, tq), 1)
    o = _flash_causal_t(lambda j: qt_ref[...], k_ref, vt_ref, scratch, qi, tk, kpos <= qpos)
    o_ref[...] = o.astype(o_ref.dtype)


def _mla_attention(q_t, k, v_t, tq, tk):
    b, h, dq, s = q_t.shape
    nk, dva = v_t.shape[2], v_t.shape[3]
    dv = dva - V_ONES_ROWS
    assert tq == tk
    return pl.pallas_call(
        functools.partial(_mla_attn_kernel, tk=tk),
        out_shape=jax.ShapeDtypeStruct((b, h, dv, s), BF16),
        grid=(b, h, s // tq),
        in_specs=[
            pl.BlockSpec((None, None, dq, tq), lambda bi, hi, qi: (bi, hi, 0, qi)),
            pl.BlockSpec((None, None, s, dq), lambda bi, hi, qi: (bi, hi, 0, 0)),
            pl.BlockSpec((None, None, nk, dva, tk), lambda bi, hi, qi: (bi, hi, 0, 0, 0)),
        ],
        out_specs=pl.BlockSpec((None, None, dv, tq), lambda bi, hi, qi: (bi, hi, 0, qi)),
        scratch_shapes=_flash_scratch(tk, tq, dv),
        compiler_params=_params(("parallel", "parallel", "arbitrary")),
        name="mla_flash",
    )(q_t, k, v_t)


def _hi_lo_dot(y, m):
    hi = y.astype(BF16)
    lo = (y - hi.astype(F32)).astype(BF16)
    return _dot(hi, m) + _dot(lo, m)


def _mla_q_prep_kernel(x_ref, g_ref, c_ref, s_ref, p_ref, o_ref, *, n_real, scale):
    x = x_ref[...].astype(F32)
    ms = jnp.sum(x * x, axis=-1, keepdims=True) * (1.0 / n_real)
    y = x * lax.rsqrt(ms + NORM_EPS) * g_ref[...]
    y = (y * c_ref[...] + _hi_lo_dot(y, p_ref[...]) * s_ref[...]) * scale
    o_ref[...] = y.T.astype(o_ref.dtype)


def _mla_q_prep(q_raw, gain, cos_t, sin_t, perm, batch, seq, heads, n_real, scale, ts=2048):
    ts = min(ts, seq)
    nsb = seq // ts
    hw = LANES
    const = lambda bi, hi, si: (0, 0)
    return pl.pallas_call(
        functools.partial(_mla_q_prep_kernel, n_real=n_real, scale=scale),
        out_shape=jax.ShapeDtypeStruct((batch, heads, hw, seq), BF16),
        grid=(batch, heads, nsb),
        in_specs=[
            pl.BlockSpec((ts, hw), lambda bi, hi, si: (bi * nsb + si, hi)),
            pl.BlockSpec((1, hw), const),
            pl.BlockSpec((None, ts, hw), lambda bi, hi, si: (bi, si, 0)),
            pl.BlockSpec((None, ts, hw), lambda bi, hi, si: (bi, si, 0)),
            pl.BlockSpec((hw, hw), const),
        ],
        out_specs=pl.BlockSpec((None, None, hw, ts), lambda bi, hi, si: (bi, hi, 0, si)),
        compiler_params=_params(("parallel", "parallel", "parallel")),
        name="mla_q_prep",
    )(q_raw, gain.reshape(1, hw), cos_t, sin_t, perm)


def _mla_kv_prep_kernel(kv_ref, kr_ref, g1_ref, g2_ref, c_ref, s_ref, p_ref, e_ref, k_ref, vt_ref, rot_scr, *,
                        n_nope, n_rope):
    @pl.when(pl.program_id(2) == 0)
    def _():
        r = kr_ref[...].astype(F32)
        rn = r * lax.rsqrt(jnp.sum(r * r, axis=-1, keepdims=True) * (1.0 / n_rope) + NORM_EPS) * g2_ref[...]
        rr = rn * c_ref[...] + _hi_lo_dot(rn, p_ref[...]) * s_ref[...]
        rot_scr[...] = _hi_lo_dot(rr, e_ref[...])

    x = kv_ref[...].astype(F32)
    ts, hw = x.shape
    lane = lax.broadcasted_iota(jnp.int32, (ts, hw), 1)
    xk = jnp.where(lane < n_nope, x, 0.0)
    kn = xk * lax.rsqrt(jnp.sum(xk * xk, axis=-1, keepdims=True) * (1.0 / n_nope) + NORM_EPS) * g1_ref[...]
    k_ref[...] = (kn + rot_scr[...]).astype(k_ref.dtype)
    v_t = x.T[n_nope:, :]
    n_t, _, tk = vt_ref.shape
    ones = jnp.where(lax.broadcasted_iota(jnp.int32, (V_ONES_ROWS, tk), 0) == 0, 1.0, 0.0)
    for i in range(n_t):
        vt_ref[i] = jnp.concatenate([v_t[:, i * tk:(i + 1) * tk], ones], axis=0).astype(vt_ref.dtype)


def _mla_kv_prep(kv_raw, proj, rot_block, g_nope, g_rope, cos_t, sin_t, perm, place, batch, seq, heads, tk,
                 tiles_per_step=4):
    hw = LANES
    n_t = min(tiles_per_step, seq // tk)
    ts = n_t * tk
    ns = seq // ts
    n_nope, n_rope = MLA_NOPE, MLA_ROPE
    dva = hw - n_nope + V_ONES_ROWS
    const = lambda bi, si, hi: (0, 0)
    return pl.pallas_call(
        functools.partial(_mla_kv_prep_kernel, n_nope=n_nope, n_rope=n_rope),
        out_shape=(jax.ShapeDtypeStruct((batch, heads, seq, hw), BF16),
                   jax.ShapeDtypeStruct((batch, heads, seq // tk, dva, tk), BF16)),
        grid=(batch, ns, heads),
        in_specs=[
            pl.BlockSpec((ts, hw), lambda bi, si, hi: (bi * ns + si, hi)),
            pl.BlockSpec((ts, hw), lambda bi, si, hi: (bi * ns + si, rot_block)),
            pl.BlockSpec((1, hw), const),
            pl.BlockSpec((1, hw), const),
            pl.BlockSpec((None, ts, hw), lambda bi, si, hi: (bi, si, 0)),
            pl.BlockSpec((None, ts, hw), lambda bi, si, hi: (bi, si, 0)),
            pl.BlockSpec((hw, hw), const),
            pl.BlockSpec((hw, hw), const),
        ],
        out_specs=(pl.BlockSpec((None, None, ts, hw), lambda bi, si, hi: (bi, hi, si, 0)),
                   pl.BlockSpec((None, None, n_t, dva, tk), lambda bi, si, hi: (bi, hi, si, 0, 0))),
        scratch_shapes=[pltpu.VMEM((ts, hw), F32)],
        compiler_params=_params(("parallel", "parallel", "arbitrary")),
        name="mla_kv_prep",
    )(kv_raw, proj, g_nope.reshape(1, hw), g_rope.reshape(1, hw), cos_t, sin_t, perm, place)


def _out_proj_t_kernel(ot_ref, w_ref, x_ref, gate_ref, o_ref):
    h, dv, tm = ot_ref.shape
    y = ot_ref[...].reshape(h * dv, tm).astype(F32).T.astype(BF16)
    o_ref[...] = x_ref[...] + gate_ref[...] * _dot(y, w_ref[...])


def _out_proj_t(o_t, w, x, gate, seq, tm=512):
    b, h, dv, _ = o_t.shape
    t, d = x.shape
    tm = min(tm, seq)
    tpb = seq // tm
    return pl.pallas_call(
        _out_proj_t_kernel,
        out_shape=jax.ShapeDtypeStruct((t, d), F32),
        grid=(b, tpb),
        in_specs=[
            pl.BlockSpec((None, h, dv, tm), lambda bi, i: (bi, 0, 0, i)),
            pl.BlockSpec((h * dv, d), lambda bi, i: (0, 0)),
            pl.BlockSpec((tm, d), lambda bi, i: (bi * tpb + i, 0)),
            pl.BlockSpec((None, 1, d), lambda bi, i: (bi, 0, 0)),
        ],
        out_specs=pl.BlockSpec((tm, d), lambda bi, i: (bi * tpb + i, 0)),
        compiler_params=_params(("parallel", "parallel")),
        name="out_proj_t",
    )(o_t, w.astype(BF16), x, gate.reshape(b, 1, d))


def _compress_kernel(x_ref, pelo_ref, pehi_ref, w1lo_ref, w1hi_ref, w2_ref, g_ref, o_ref, *, norm):
    x = x_ref[...]
    nc = x.shape[0]
    a = _dot((x + pelo_ref[...]).astype(BF16), w1lo_ref[...])
    b = _dot((x + pehi_ref[...]).astype(BF16), w1hi_ref[...])
    hid = _silu(a + pltpu.roll(b, shift=nc - 1, axis=0))
    y = _dot(hid.astype(BF16), w2_ref[...])
    if norm:
        y = _rms(y, g_ref[...])
    o_ref[...] = y.astype(o_ref.dtype)


def _compress(x, pe, w1, w2, gain, norm):
    b, g, nc, wdt = x.shape
    dh = w2.shape[1]
    hidden = w1.shape[1]
    half = NSA_CMP_STRIDE
    pelo = pe[:half].reshape(1, wdt)
    pehi = pe[half:].reshape(1, wdt)
    w1 = w1.astype(BF16)
    const = lambda bi, gi: (0, 0)
    return pl.pallas_call(
        functools.partial(_compress_kernel, norm=norm),
        out_shape=jax.ShapeDtypeStruct((b, g, nc, dh), BF16),
        grid=(b, g),
        in_specs=[
            pl.BlockSpec((None, None, nc, wdt), lambda bi, gi: (bi, gi, 0, 0)),
            pl.BlockSpec((1, wdt), const),
            pl.BlockSpec((1, wdt), const),
            pl.BlockSpec((wdt, hidden), const),
            pl.BlockSpec((wdt, hidden), const),
            pl.BlockSpec((hidden, dh), const),
            pl.BlockSpec((1, dh), const),
        ],
        out_specs=pl.BlockSpec((None, None, nc, dh), lambda bi, gi: (bi, gi, 0, 0)),
        compiler_params=_params(("parallel", "parallel")),
        name="nsa_compress",
    )(x, pelo, pehi, w1[:wdt], w1[wdt:], w2.astype(BF16), gain.reshape(1, dh))


def _to_token_major(o_t, n_rep, tq):
    stacked = jnp.concatenate([o_t[:, r * tq:(r + 1) * tq] for r in range(n_rep)], axis=0)
    return stacked.T


def _nsa_cmp_kernel(qt_ref, kc_ref, vct_ref, band_ref, o_ref, sb_ref, s_scr, *, tq, n_rep):
    qi = pl.program_id(2)
    q0 = qi * tq
    nq = n_rep * tq
    n_ch, ch = s_scr.shape[0], s_scr.shape[1]
    dh = qt_ref.shape[0]
    nb = band_ref.shape[1]
    q_t = qt_ref[...]
    n_vis = (q0 + tq - NSA_CMP_STRIDE) // NSA_CMP_STRIDE
    n_used = (n_vis + ch - 1) // ch
    tpos = q0 + (lax.broadcasted_iota(jnp.int32, (ch, nq), 1) & (tq - 1))
    end0 = lax.broadcasted_iota(jnp.int32, (ch, nq), 0) * NSA_CMP_STRIDE + (NSA_CMP_LEN - 1)

    def scores(c, cm):
        s = _dot(kc_ref[pl.ds(pl.multiple_of(c * ch, ch), ch), :], q_t)
        s = jnp.where(end0 + c * (ch * NSA_CMP_STRIDE) <= tpos, s, -jnp.inf)
        s_scr[c] = s
        return jnp.maximum(cm, jnp.max(s, axis=0, keepdims=True))

    m = lax.fori_loop(0, n_used, scores, jnp.full((1, nq), -jnp.inf, F32))
    m = jnp.where(m > -jnp.inf, m, 0.0)

    def exps(c, l):
        p = jnp.exp2(s_scr[c] - m)
        s_scr[c] = p
        return l + jnp.sum(p, axis=0, keepdims=True)

    l = lax.fori_loop(0, n_used, exps, jnp.zeros((1, nq), F32))
    inv = 1.0 / jnp.maximum(l, 1e-30)

    def outputs(c, carry):
        o_t, imp = carry
        p = s_scr[c] * inv
        o_t = o_t + _dot(vct_ref[c], p.astype(BF16))
        psum = p[:, 0:tq]
        for r in range(1, n_rep):
            psum = psum + p[:, r * tq:(r + 1) * tq]
        return o_t, imp + _dot3_left(band_ref[c], psum)

    o_t, imp = lax.fori_loop(0, n_used, outputs, (jnp.zeros((dh, nq), F32), jnp.zeros((nb, tq), F32)))
    o_ref[...] = _to_token_major(o_t, n_rep, tq).astype(o_ref.dtype)

    blk = lax.broadcasted_iota(jnp.int32, (nb, tq), 0)
    cur = (q0 + lax.broadcasted_iota(jnp.int32, (nb, tq), 1)) // NSA_SEL_LEN
    causal = blk <= cur
    forced = causal & ((blk == 0) | (blk > cur - NSA_N_LOCAL))
    score = jnp.where(forced, 1e30, jnp.where(causal, imp, -1.0))
    blk_f = blk.astype(F32)
    bias = jnp.full((nb, tq), SEL_MASK_BIAS, F32)
    for _ in range(NSA_N_SEL):
        best = jnp.max(score, axis=0, keepdims=True)
        idx = jnp.min(jnp.where(score == best, blk_f, float(nb)), axis=0, keepdims=True)
        pick = (blk_f == idx) & (best >= 0.0)
        bias = jnp.where(pick, 0.0, bias)
        score = jnp.where(pick, -1.0, score)
    sb_ref[...] = bias.astype(sb_ref.dtype)


NSA_CMP_CHUNK = 512


def _nsa_cmp_select(q_t, k_cmp, v_cmp, band, tq):
    b, g, n_q, dh, nq = q_t.shape
    r = nq // tq
    nc = k_cmp.shape[2]
    nb = band.shape[0]
    ch = min(NSA_CMP_CHUNK, nc)
    n_ch = nc // ch
    vc_t = jnp.swapaxes(v_cmp.reshape(b, g, n_ch, ch, dh), -1, -2)
    band_c = band.reshape(nb, n_ch, ch).transpose(1, 0, 2)
    return pl.pallas_call(
        functools.partial(_nsa_cmp_kernel, tq=tq, n_rep=r),
        out_shape=(jax.ShapeDtypeStruct((b * n_q * tq, g * r * dh), BF16),
                   jax.ShapeDtypeStruct((b, g, nb, n_q * tq), BF16)),
        grid=(b, g, n_q),
        in_specs=[
            pl.BlockSpec((None, None, None, dh, nq), lambda bi, gi, qi: (bi, gi, qi, 0, 0)),
            pl.BlockSpec((None, None, nc, dh), lambda bi, gi, qi: (bi, gi, 0, 0)),
            pl.BlockSpec((None, None, n_ch, dh, ch), lambda bi, gi, qi: (bi, gi, 0, 0, 0)),
            pl.BlockSpec((n_ch, nb, ch), lambda bi, gi, qi: (0, 0, 0)),
        ],
        out_specs=(pl.BlockSpec((tq, r * dh), lambda bi, gi, qi: (bi * n_q + qi, gi)),
                   pl.BlockSpec((None, None, nb, tq), lambda bi, gi, qi: (bi, gi, 0, qi))),
        scratch_shapes=[pltpu.VMEM((n_ch, ch, nq), F32)],
        compiler_params=_params(("parallel", "parallel", "parallel")),
        name="nsa_cmp_select",
    )(q_t, k_cmp, vc_t, band_c)


def _nsa_sel_kernel(qt_ref, sb_ref, k_ref, vt_ref, o_ref, qa_scr, *scratch, tq, tk, n_rep, tiles_per_half):
    qi = pl.program_id(2)
    q0 = qi * tq
    nq = n_rep * tq
    n_half = qa_scr.shape[0]
    q_t = qt_ref[...]
    for hh in range(n_half):
        sb = sb_ref[hh * SEL_ONEHOT:(hh + 1) * SEL_ONEHOT, :]
        qa_scr[hh] = jnp.concatenate([jnp.concatenate([sb] * n_rep, axis=1), q_t], axis=0)
    jd = q0 // tk
    kpos = jd * tk + lax.broadcasted_iota(jnp.int32, (tk, nq), 0)
    tpos = q0 + (lax.broadcasted_iota(jnp.int32, (tk, nq), 1) & (tq - 1))
    o = _flash_causal_t(lambda j: qa_scr[j // tiles_per_half], k_ref, vt_ref, scratch, jd, tk, kpos <= tpos)
    o_ref[...] = _to_token_major(o, n_rep, tq).astype(o_ref.dtype)


def _nsa_sel_attention(q_t, sel_bias, k_aug, v_t, tq, tk):
    b, g, n_q, dh, nq = q_t.shape
    nb, s = sel_bias.shape[2], sel_bias.shape[3]
    n_half = nb // SEL_ONEHOT
    ka = k_aug.shape[-1]
    nk, dva = v_t.shape[2], v_t.shape[3]
    tiles_per_half = SEL_ONEHOT * NSA_SEL_LEN // tk
    return pl.pallas_call(
        functools.partial(_nsa_sel_kernel, tq=tq, tk=tk, n_rep=nq // tq, tiles_per_half=tiles_per_half),
        out_shape=jax.ShapeDtypeStruct((b * s, g * (nq // tq) * dh), BF16),
        grid=(b, g, n_q),
        in_specs=[
            pl.BlockSpec((None, None, None, dh, nq), lambda bi, gi, qi: (bi, gi, qi, 0, 0)),
            pl.BlockSpec((None, None, nb, tq), lambda bi, gi, qi: (bi, gi, 0, qi)),
            pl.BlockSpec((None, None, s, ka), lambda bi, gi, qi: (bi, gi, 0, 0)),
            pl.BlockSpec((None, None, nk, dva, tk), lambda bi, gi, qi: (bi, gi, 0, 0, 0)),
        ],
        out_specs=pl.BlockSpec((tq, (nq // tq) * dh), lambda bi, gi, qi: (bi * n_q + qi, gi)),
        scratch_shapes=[pltpu.VMEM((n_half, ka, nq), BF16)] + _flash_scratch(tk, nq, dva - V_ONES_ROWS),
        compiler_params=_params(("parallel", "parallel", "arbitrary")),
        name="nsa_sel_flash",
    )(q_t, sel_bias, k_aug, v_t)


def _nsa_win_kernel(qt_ref, k_ref, vt_ref, o_ref, *, tq, win, n_rep):
    qi = pl.program_id(2)
    q0 = qi * tq
    nq = n_rep * tq
    dv = vt_ref.shape[-2] - V_ONES_ROWS
    n_chunk = (tq + win) // tq
    q_t = qt_ref[...]
    tpos = q0 + (lax.broadcasted_iota(jnp.int32, (tq, nq), 1) & (tq - 1))
    krow = lax.broadcasted_iota(jnp.int32, (tq, nq), 0)
    chunk_ids = [jnp.maximum(qi + c - (n_chunk - 1), 0) for c in range(n_chunk)]
    ss = []
    m = None
    for c in range(n_chunk):
        s = _dot(k_ref[pl.ds(pl.multiple_of(chunk_ids[c] * tq, tq), tq), :], q_t)
        kpos = q0 - win + c * tq + krow
        if c < n_chunk - 1:
            mask = (kpos > tpos - win) & (kpos >= 0)
        else:
            mask = kpos <= tpos
        s = jnp.where(mask, s, -jnp.inf)
        ss.append(s)
        cm = jnp.max(s, axis=0, keepdims=True)
        m = cm if m is None else jnp.maximum(m, cm)
    acc = None
    for c in range(n_chunk):
        d = _dot(vt_ref[chunk_ids[c]], jnp.exp2(ss[c] - m).astype(BF16))
        acc = d if acc is None else acc + d
    o_t = acc[0:dv, :] / acc[dv:dv + 1, :]
    o_ref[...] = _to_token_major(o_t, n_rep, tq).astype(o_ref.dtype)


def _nsa_win_attention(q_t, k_pad, vt_pad, tq):
    b, g, n_q, dh, nq = q_t.shape
    sp = k_pad.shape[2]
    nkc, dva = vt_pad.shape[2], vt_pad.shape[3]
    r = nq // tq
    return pl.pallas_call(
        functools.partial(_nsa_win_kernel, tq=tq, win=NSA_WINDOW, n_rep=r),
        out_shape=jax.ShapeDtypeStruct((b * n_q * tq, g * r * dh), BF16),
        grid=(b, g, n_q),
        in_specs=[
            pl.BlockSpec((None, None, None, dh, nq), lambda bi, gi, qi: (bi, gi, qi, 0, 0)),
            pl.BlockSpec((None, None, sp, dh), lambda bi, gi, qi: (bi, gi, 0, 0)),
            pl.BlockSpec((None, None, nkc, dva, tq), lambda bi, gi, qi: (bi, gi, 0, 0, 0)),
        ],
        out_specs=pl.BlockSpec((tq, r * dh), lambda bi, gi, qi: (bi * n_q + qi, gi)),
        compiler_params=_params(("parallel", "parallel", "arbitrary")),
        name="nsa_window",
    )(q_t, k_pad, vt_pad)


def _nsa_out_kernel(oc_ref, os_ref, ow_ref, gl_ref, e_ref, w_ref, x_ref, gate_ref, o_ref):
    sg = _sigmoid(gl_ref[...].astype(F32))
    hi = sg.astype(BF16)
    lo = (sg - hi.astype(F32)).astype(BF16)

    def expand(i):
        return _dot(hi, e_ref[i]) + _dot(lo, e_ref[i])

    o = expand(0) * oc_ref[...] + expand(1) * os_ref[...] + expand(2) * ow_ref[...]
    o_ref[...] = x_ref[...] + gate_ref[...] * _dot(o.astype(BF16), w_ref[...])


def _nsa_out(o_cmp, o_sel, o_win, gate_logits, expand, w, x, gate, seq, tm=512):
    t, d = x.shape
    k = o_cmp.shape[1]
    gw = gate_logits.shape[1]
    b = gate.shape[0]
    tpb = seq // tm
    row = lambda i: (i, 0)
    return pl.pallas_call(
        _nsa_out_kernel,
        out_shape=jax.ShapeDtypeStruct((t, d), F32),
        grid=(t // tm,),
        in_specs=[
            pl.BlockSpec((tm, k), row),
            pl.BlockSpec((tm, k), row),
            pl.BlockSpec((tm, k), row),
            pl.BlockSpec((tm, gw), row),
            pl.BlockSpec((3, gw, k), lambda i: (0, 0, 0)),
            pl.BlockSpec((k, d), lambda i: (0, 0)),
            pl.BlockSpec((tm, d), row),
            pl.BlockSpec((None, 1, d), lambda i: (i // tpb, 0, 0)),
        ],
        out_specs=pl.BlockSpec((tm, d), row),
        compiler_params=_params(("parallel",)),
        name="nsa_out",
    )(o_cmp, o_sel, o_win, gate_logits, expand, w.astype(BF16), x, gate.reshape(b, 1, d))


def _ret_kernel(lg_ref, q_ref, k_ref, v_ref, g_ref, cos_ref, sin_ref, gn_ref, o_ref, st_scr):
    h = pl.program_id(1)
    c = pl.program_id(2)

    @pl.when(c == 0)
    def _():
        st_scr[...] = jnp.zeros_like(st_scr)

    cn = RET_CHUNK
    dk = q_ref.shape[1]
    half = dk // 2
    lg = jnp.full((1, 1), lg_ref[h], F32)
    n = lax.broadcasted_iota(jnp.int32, (cn, 1), 0).astype(F32)
    rel = (lax.broadcasted_iota(jnp.int32, (cn, cn), 0) - lax.broadcasted_iota(jnp.int32, (cn, cn), 1)).astype(F32)
    decay = jnp.where(rel >= 0, jnp.exp(jnp.maximum(rel, 0.0) * lg), 0.0)
    xi = jnp.exp((n + 1.0) * lg)
    zeta = jnp.exp((cn - 1.0 - n) * lg)
    gamma_c = jnp.exp(cn * lg)

    for sub in range(q_ref.shape[0] // cn):
        rows = slice(sub * cn, (sub + 1) * cn)
        cos = cos_ref[rows, :]
        sin = sin_ref[rows, :]

        def rope(x):
            x1, x2 = x[:, :half], x[:, half:]
            return jnp.concatenate([x1 * cos - x2 * sin, x1 * sin + x2 * cos], axis=1)

        q = rope(q_ref[rows, :].astype(F32))
        k = rope(k_ref[rows, :].astype(F32)) * (dk ** -0.5)
        v = v_ref[rows, :].astype(BF16)
        qb = q.astype(BF16)
        inner = _dot((_dot_nt(qb, k.astype(BF16)) * decay).astype(BF16), v)
        st = st_scr[...]
        cross = _dot(qb, st.astype(BF16)) * xi
        kz_t = (k * zeta).T.astype(BF16)
        st_scr[...] = gamma_c * st + _dot(kz_t, v)
        y = inner + cross
        mu = jnp.mean(y, axis=-1, keepdims=True)
        yc = y - mu
        var = jnp.mean(yc * yc, axis=-1, keepdims=True)
        yn = yc * lax.rsqrt(var + NORM_EPS) * gn_ref[...]
        o_ref[rows, :] = (_silu(g_ref[rows, :].astype(F32)) * yn).astype(o_ref.dtype)


def _retention(proj, cos_t, sin_t, log_gamma, gn_w, batch, seq, heads):
    t = proj.shape[0]
    dk, dv = RET_QK, RET_V
    cn = min(RET_STEP_CHUNKS * RET_CHUNK, seq)
    n_ch = seq // cn
    kb = heads * dk // dk
    vb = 2 * heads * dk // dv
    row = lambda bi, hi, ci, *_: bi * n_ch + ci
    grid_spec = pltpu.PrefetchScalarGridSpec(
        num_scalar_prefetch=1,
        grid=(batch, heads, n_ch),
        in_specs=[
            pl.BlockSpec((cn, dk), lambda bi, hi, ci, lg: (row(bi, hi, ci), hi)),
            pl.BlockSpec((cn, dk), lambda bi, hi, ci, lg: (row(bi, hi, ci), kb + hi)),
            pl.BlockSpec((cn, dv), lambda bi, hi, ci, lg: (row(bi, hi, ci), vb + hi)),
            pl.BlockSpec((cn, dv), lambda bi, hi, ci, lg: (row(bi, hi, ci), vb + heads + hi)),
            pl.BlockSpec((cn, dk // 2), lambda bi, hi, ci, lg: (row(bi, hi, ci), 0)),
            pl.BlockSpec((cn, dk // 2), lambda bi, hi, ci, lg: (row(bi, hi, ci), 0)),
            pl.BlockSpec((1, dv), lambda bi, hi, ci, lg: (0, hi)),
        ],
        out_specs=pl.BlockSpec((cn, dv), lambda bi, hi, ci, lg: (row(bi, hi, ci), hi)),
        scratch_shapes=[pltpu.VMEM((dk, dv), F32)],
    )
    return pl.pallas_call(
        _ret_kernel,
        out_shape=jax.ShapeDtypeStruct((t, heads * dv), BF16),
        grid_spec=grid_spec,
        compiler_params=_params(("parallel", "parallel", "arbitrary")),
        name="retention",
    )(log_gamma, proj, proj, proj, proj, cos_t, sin_t, gn_w.reshape(1, heads * dv))


def _softplus(x):
    return jnp.maximum(x, 0.0) + jnp.log1p(jnp.exp(-jnp.abs(x)))


def _ssd_kernel(row_ref, dt_ref, dtt_ref, cw_ref, cb_ref, dtb_ref, dtbt_ref, al_ref, alt_ref, dsk_ref, nw_ref,
                ex_ref, o_ref, buf_scr, st_scr, *, inner, groups, state, heads):
    c = pl.program_id(1)
    cn = row_ref.shape[0]
    conv_ch = inner + 2 * groups * state
    hp = inner // heads
    rep = heads // groups
    gw = rep * hp
    halo = 8

    @pl.when(c == 0)
    def _():
        buf_scr[0:halo, :] = jnp.zeros((halo, conv_ch), F32)
        st_scr[...] = jnp.zeros_like(st_scr)

    z = row_ref[:, 0:inner].astype(F32)
    xbc = row_ref[:, inner:inner + conv_ch].astype(F32)
    dt_raw = dt_ref[...]

    buf_scr[halo:halo + cn, :] = xbc
    conv = cb_ref[...] + cw_ref[0:1, :] * buf_scr[halo - 3:halo - 3 + cn, :]
    for kk in range(1, SSD_CONV):
        conv = conv + cw_ref[kk:kk + 1, :] * buf_scr[halo - 3 + kk:halo - 3 + kk + cn, :]
    buf_scr[0:halo, :] = xbc[cn - halo:cn, :]
    act = _silu(conv)
    xs = act[:, 0:inner]
    bm = act[:, inner:inner + groups * state]
    cm = act[:, inner + groups * state:conv_ch]

    dt = _softplus(dt_raw + dtb_ref[...])
    dt_t = _softplus(dtt_ref[...] + dtbt_ref[...])
    a = -jnp.exp(al_ref[...])
    a_t = -jnp.exp(alt_ref[...])
    ri = lax.broadcasted_iota(jnp.int32, (cn, cn), 0)
    ci = lax.broadcasted_iota(jnp.int32, (cn, cn), 1)
    tril = ci <= ri
    lower = jnp.where(tril, 1.0, 0.0).astype(BF16)
    upper = jnp.where(ci >= ri, 1.0, 0.0).astype(BF16)
    cum = _dot3_left(lower, dt * a)
    cum_t = _dot3(dt_t * a_t, upper)
    cum_last = cum[cn - 1:cn, :]

    ex = ex_ref[...]
    e_cum = _dot3(jnp.exp(cum), ex)
    w_end = _dot3(jnp.exp(cum_last - cum) * dt, ex)
    e_last = _dot3(jnp.exp(cum_last), ex)

    lane = lax.broadcasted_iota(jnp.int32, (cn, 2 * hp), 1)
    ys = []
    for g in range(groups):
        cc = cm[:, g * state:(g + 1) * state].astype(BF16)
        bc = bm[:, g * state:(g + 1) * state].astype(BF16)
        cb = _dot_nt(cc, bc)
        xg = xs[:, g * gw:(g + 1) * gw]
        st = st_scr[g]
        cross = _dot(cc, st.astype(BF16)) * e_cum[:, g * gw:(g + 1) * gw]
        xw_t = bc.astype(F32).T.astype(BF16)
        st_scr[g] = e_last[:, g * gw:(g + 1) * gw] * st + _dot(
            xw_t, (xg * w_end[:, g * gw:(g + 1) * gw]).astype(BF16))
        intra = []
        for pr in range(rep // 2):
            ws = []
            for hh in (2 * pr, 2 * pr + 1):
                hd = g * rep + hh
                seg = jnp.where(tril, cum[:, hd:hd + 1] - cum_t[hd:hd + 1, :], -jnp.inf)
                ws.append((jnp.exp(seg) * cb * dt_t[hd:hd + 1, :]).astype(BF16))
            w2 = jnp.concatenate(ws, axis=1)
            xp = xg[:, 2 * pr * hp:(2 * pr + 2) * hp]
            x2 = jnp.concatenate([jnp.where(lane < hp, xp, 0.0), jnp.where(lane >= hp, xp, 0.0)], axis=0)
            intra.append(_dot(w2, x2.astype(BF16)))
        ys.append(jnp.concatenate(intra, axis=1) + cross)
    y = jnp.concatenate(ys, axis=1) + dsk_ref[...] * xs
    y = y * _silu(z)
    outs = []
    for g in range(groups):
        outs.append(_rms(y[:, g * gw:(g + 1) * gw], nw_ref[:, g * gw:(g + 1) * gw]))
    o_ref[...] = jnp.concatenate(outs, axis=1).astype(o_ref.dtype)


def _ssd(proj, dt_raw, conv_w, conv_b, dt_bias, a_log, d_skip, norm_w, batch, seq, inner, heads):
    t, n_pad = proj.shape
    groups, state, cn = SSD_GROUPS, SSD_STATE, SSD_CHUNK
    conv_ch = inner + 2 * groups * state
    n_ch = seq // cn
    hp = inner // heads
    gw = inner // groups
    hpad = dt_raw.shape[1]
    assert heads <= hpad and n_pad >= inner + conv_ch
    dt_t = dt_raw.reshape(batch, seq, hpad).transpose(0, 2, 1)
    ex_np = np.zeros((hpad, inner), np.float32)
    ex_np[:heads] = np.kron(np.eye(heads, dtype=np.float32), np.ones((1, hp), np.float32))
    ex = jnp.asarray(ex_np, BF16)
    dt_bias = jnp.pad(dt_bias, (0, hpad - heads))
    a_log = jnp.pad(a_log, (0, hpad - heads))
    heads_k = heads
    heads = hpad
    const = lambda bi, ci: (0, 0)
    return pl.pallas_call(
        functools.partial(_ssd_kernel, inner=inner, groups=groups, state=state, heads=heads_k),
        out_shape=jax.ShapeDtypeStruct((t, inner), BF16),
        grid=(batch, n_ch),
        in_specs=[
            pl.BlockSpec((cn, n_pad), lambda bi, ci: (bi * n_ch + ci, 0)),
            pl.BlockSpec((cn, heads), lambda bi, ci: (bi * n_ch + ci, 0)),
            pl.BlockSpec((None, heads, cn), lambda bi, ci: (bi, 0, ci)),
            pl.BlockSpec((SSD_CONV, conv_ch), const),
            pl.BlockSpec((1, conv_ch), const),
            pl.BlockSpec((1, heads), const),
            pl.BlockSpec((heads, 1), const),
            pl.BlockSpec((1, heads), const),
            pl.BlockSpec((heads, 1), const),
            pl.BlockSpec((1, inner), const),
            pl.BlockSpec((1, inner), const),
            pl.BlockSpec((heads, inner), const),
        ],
        out_specs=pl.BlockSpec((cn, inner), lambda bi, ci: (bi * n_ch + ci, 0)),
        scratch_shapes=[pltpu.VMEM((8 + cn, conv_ch), F32), pltpu.VMEM((groups, state, gw), F32)],
        compiler_params=_params(("parallel", "arbitrary")),
        name="ssd_scan",
    )(proj, dt_raw, dt_t, conv_w, conv_b.reshape(1, conv_ch), dt_bias.reshape(1, heads), dt_bias.reshape(heads, 1),
      a_log.reshape(1, heads), a_log.reshape(heads, 1), jnp.repeat(d_skip, hp).reshape(1, inner),
      norm_w.reshape(1, inner), ex)


def _nsa_layer(x, mods, norm_g, batch, seq, w_in, q_norm, k_norm, cmp_pe, cmp_w1, cmp_w2, w_out):
    sh, sc, gate = mods
    d = x.shape[1]
    dh, g = HEAD_DIM, NSA_GROUPS
    heads = d // dh
    r = heads // g
    kvw = g * dh
    proj = _modnorm_matmul(x, norm_g, sc, sh, w_in, seq, out_dtype=BF16)
    off = heads * dh

    def heads_major(cols, n_heads):
        return cols.reshape(batch, seq, n_heads, dh).transpose(0, 2, 1, 3)

    q = heads_major(proj[:, :off], heads)
    parts = [proj[:, off + i * kvw: off + (i + 1) * kvw] for i in range(6)]
    k_c, v_c, k_s, v_s, k_w, v_w = parts
    gate_logits = proj[:, off + 6 * kvw: off + 6 * kvw + LANES]

    tq, tk = NSA_TQ, min(NSA_TK, seq)
    n_q = seq // tq
    qn = _norm_rope(q, q_norm, scale=dh ** -0.5 * LOG2E)
    q_t = qn.reshape(batch, g, r, n_q, tq, dh).transpose(0, 1, 3, 5, 2, 4).reshape(batch, g, n_q, dh, r * tq)

    def chunk_rows(cols):
        st = NSA_CMP_STRIDE
        return cols.reshape(batch, seq // st, st, g, dh).transpose(0, 3, 1, 2, 4).reshape(batch, g, seq // st, st * dh)

    k_cmp = _compress(chunk_rows(k_c), cmp_pe[0], cmp_w1[0], cmp_w2[0], k_norm[0], norm=True)
    v_cmp = _compress(chunk_rows(v_c), cmp_pe[1], cmp_w1[1], cmp_w2[1], k_norm[0], norm=False)

    n_blk = seq // NSA_SEL_LEN
    n_cmp = seq // NSA_CMP_STRIDE
    assert n_blk % SEL_ONEHOT == 0
    ratio = NSA_SEL_LEN // NSA_CMP_STRIDE
    jj = np.arange(n_blk)[:, None]
    cc = np.arange(n_cmp)[None, :]
    band = jnp.asarray(((cc >= ratio * jj - 1) & (cc <= ratio * jj + ratio - 1)).astype(np.float32), BF16)
    o_cmp, sel_bias = _nsa_cmp_select(q_t, k_cmp, v_cmp, band, tq)

    ks_n = _norm_rope(heads_major(k_s, g), k_norm[1])
    onehot = jnp.asarray(np.eye(SEL_ONEHOT, dtype=np.float32)[(np.arange(seq) // NSA_SEL_LEN) % SEL_ONEHOT], BF16)
    k_aug = jnp.concatenate([jnp.broadcast_to(onehot, (batch, g, seq, SEL_ONEHOT)), ks_n], axis=-1)
    o_sel = _nsa_sel_attention(q_t, sel_bias, k_aug, _with_ones_rows_t(heads_major(v_s, g), tk), tq, tk)

    kw_n = _norm_rope(heads_major(k_w, g), k_norm[2])
    o_win = _nsa_win_attention(q_t, kw_n, _with_ones_rows_t(heads_major(v_w, g), tq), tq)

    e = np.zeros((3, LANES, heads * dh), np.float32)
    for hd in range(heads):
        for i in range(3):
            e[i, hd * 3 + i, hd * dh:(hd + 1) * dh] = 1.0
    return _nsa_out(o_cmp, o_sel, o_win, gate_logits, jnp.asarray(e, BF16), w_out, x, gate, seq)


def _mla_layer(x, mods, norm_g, batch, seq, pos, w_in, q_a_norm, kv_a_norm, w_q_b, w_kv_b, q_norm, k_norm, w_out):
    sh, sc, gate = mods
    d = x.shape[1]
    heads = w_out.shape[0] // MLA_V
    dq = MLA_NOPE + MLA_ROPE
    proj = _modnorm_matmul(x, norm_g, sc, sh, w_in, seq, out_dtype=BF16)
    cq = proj[:, :MLA_Q_LORA]
    ckv = proj[:, MLA_Q_LORA:MLA_Q_LORA + MLA_KV_LORA]
    k_rot = proj[:, MLA_Q_LORA + MLA_KV_LORA:MLA_Q_LORA + MLA_KV_LORA + MLA_ROPE]
    assert MLA_NOPE + MLA_V == LANES and dq <= LANES
    w_q_pad = jnp.pad(w_q_b.reshape(MLA_Q_LORA, heads, dq), ((0, 0), (0, 0), (0, LANES - dq)))
    q_raw = _norm_matmul(cq, q_a_norm, w_q_pad.reshape(MLA_Q_LORA, heads * LANES), out_dtype=BF16)
    kv_raw = _norm_matmul(ckv, kv_a_norm, w_kv_b, out_dtype=BF16)

    half = MLA_ROPE // 2
    cos, sin = _rope_tables(pos, MLA_ROPE)
    ones = jnp.ones((batch, seq, MLA_NOPE), F32)
    zq = jnp.zeros((batch, seq, LANES - dq), F32)
    zk = jnp.zeros((batch, seq, LANES - MLA_ROPE), F32)
    cos_q = jnp.concatenate([ones, cos, cos, zq], axis=-1)
    sin_q = jnp.concatenate([0.0 * ones, sin, sin, zq], axis=-1)
    cos_k = jnp.concatenate([cos, cos, zk], axis=-1)
    sin_k = jnp.concatenate([sin, sin, zk], axis=-1)
    pad_to = lambda v: jnp.pad(v, (0, LANES - v.shape[0]))
    tq = min(MLA_TQ, seq)
    tk = min(MLA_TK, tq)
    q_t = _mla_q_prep(q_raw, pad_to(q_norm), cos_q, sin_q, _rotate_half_perm(LANES, MLA_NOPE, half),
                      batch, seq, heads, n_real=dq, scale=dq ** -0.5 * LOG2E)
    place = np.zeros((LANES, LANES), np.float32)
    place[np.arange(MLA_ROPE), MLA_NOPE + np.arange(MLA_ROPE)] = 1.0
    rot_block = (MLA_Q_LORA + MLA_KV_LORA) // LANES
    assert rot_block * LANES == MLA_Q_LORA + MLA_KV_LORA and proj.shape[1] >= (rot_block + 1) * LANES
    k, v_t = _mla_kv_prep(kv_raw, proj, rot_block, pad_to(k_norm[:MLA_NOPE]), pad_to(k_norm[MLA_NOPE:]),
                          cos_k, sin_k, _rotate_half_perm(LANES, 0, half), jnp.asarray(place, BF16),
                          batch, seq, heads, tk)
    o_t = _mla_attention(q_t, k, v_t, tq, tk)
    return _out_proj_t(o_t, w_out, x, gate, seq)


def _ret_layer(x, mods, norm_g, batch, seq, pos, w_in, gn_w, w_out):
    sh, sc, gate = mods
    heads = w_out.shape[0] // RET_V
    proj = _modnorm_matmul(x, norm_g, sc, sh, w_in, seq, out_dtype=BF16)
    cos, sin = _rope_tables(pos, RET_QK)
    cos = cos.reshape(batch * seq, RET_QK // 2)
    sin = sin.reshape(batch * seq, RET_QK // 2)
    log_gamma = jnp.log1p(-(2.0 ** (-5.0 - jnp.arange(heads, dtype=F32))))
    y = _retention(proj, cos, sin, log_gamma, gn_w, batch, seq, heads)
    return _out_proj(y, w_out, x, gate, seq)


def _ssd_layer(x, mods, norm_g, batch, seq, w_in, conv_w, conv_b, dt_bias, a_log, d_skip, norm_w, w_out):
    sh, sc, gate = mods
    inner = w_out.shape[0]
    heads = dt_bias.shape[0]
    conv_ch = conv_w.shape[1]
    proj = _modnorm_matmul(x, norm_g, sc, sh, w_in[:, :inner + conv_ch], seq, out_dtype=BF16)
    dt_raw = _modnorm_matmul(x, norm_g, sc, sh, w_in[:, inner + conv_ch:], seq)
    y = _ssd(proj, dt_raw, conv_w, conv_b, dt_bias, a_log, d_skip, norm_w, batch, seq, inner, heads)
    return _out_proj(y, w_out, x, gate, seq)


def kernel(x, c, positions, ada_w, ada_b, norm_mix, norm_ffn, ffn_w_in, ffn_w_out, nsa_w_in, nsa_q_norm, nsa_k_norm, nsa_cmp_pe, nsa_cmp_w1, nsa_cmp_w2, nsa_w_out, mla_w_in, mla_q_a_norm, mla_kv_a_norm, mla_w_q_b, mla_w_kv_b, mla_q_norm, mla_k_norm, mla_w_out, ret_w_in, ret_gn_w, ret_w_out, ssd_w_in, ssd_conv_w, ssd_conv_b, ssd_dt_bias, ssd_a_log, ssd_d, ssd_norm, ssd_w_out):
    batch, seq, d = x.shape
    depth = ada_w.shape[0]
    n_mixers = 4
    mod = _modulation(c, ada_w, ada_b)
    xt = x.reshape(batch * seq, d)
    for i in range(depth):
        sh_m, sc_m, g_m, sh_f, sc_f, g_f = [mod[i, :, k * d:(k + 1) * d] for k in range(6)]
        mods = (sh_m, sc_m, g_m)
        kind, j = i % n_mixers, i // n_mixers
        if kind == 0:
            xt = _nsa_layer(xt, mods, norm_mix[i], batch, seq, nsa_w_in[j], nsa_q_norm[j], nsa_k_norm[j],
                            nsa_cmp_pe[j], nsa_cmp_w1[j], nsa_cmp_w2[j], nsa_w_out[j])
        elif kind == 1:
            xt = _mla_layer(xt, mods, norm_mix[i], batch, seq, positions, mla_w_in[j], mla_q_a_norm[j],
                            mla_kv_a_norm[j], mla_w_q_b[j], mla_w_kv_b[j], mla_q_norm[j], mla_k_norm[j], mla_w_out[j])
        elif kind == 2:
            xt = _ret_layer(xt, mods, norm_mix[i], batch, seq, positions, ret_w_in[j], ret_gn_w[j], ret_w_out[j])
        else:
            xt = _ssd_layer(xt, mods, norm_mix[i], batch, seq, ssd_w_in[j], ssd_conv_w[j], ssd_conv_b[j],
                            ssd_dt_bias[j], ssd_a_log[j], ssd_d[j], ssd_norm[j], ssd_w_out[j])
        xt = _ffn(xt, norm_ffn[i], sc_f, sh_f, g_f, ffn_w_in[i], ffn_w_out[i], seq)
    return xt.reshape(batch, seq, d)
```

```python
import functools
import math

import numpy as np
import jax
import jax.numpy as jnp
from jax import lax
from jax.experimental import pallas as pl
from jax.experimental.pallas import tpu as pltpu

F32 = jnp.float32
BF16 = jnp.bfloat16

NORM_EPS = 1e-6
ROPE_THETA = 10000.0
LOG2E = math.log2(math.e)
LANES = 128

HEAD_DIM = 64
NSA_GROUPS = 4
NSA_CMP_STRIDE = 16
NSA_CMP_LEN = 32
NSA_SEL_LEN = 64
NSA_N_SEL = 16
NSA_N_LOCAL = 2
NSA_WINDOW = 512
NSA_TQ = 128
NSA_TK = 512
SEL_ONEHOT = 128
SEL_MASK_BIAS = -32768.0

MLA_Q_LORA = 384
MLA_KV_LORA = 256
MLA_NOPE = 64
MLA_ROPE = 32
MLA_V = 64
MLA_TQ = 512
MLA_TK = 512

RET_QK = 256
RET_V = 512
RET_CHUNK = 128
RET_STEP_CHUNKS = 4

SSD_HEADDIM = 64
SSD_GROUPS = 4
SSD_STATE = 128
SSD_CONV = 4
SSD_CHUNK = 128

VMEM_LIMIT = 48 * 1024 * 1024


def _params(sem, vmem=VMEM_LIMIT, flags=None):
    return pltpu.CompilerParams(dimension_semantics=sem, vmem_limit_bytes=vmem, flags=flags)


def _sigmoid(x):
    return 1.0 / (1.0 + jnp.exp(-x))


def _silu(x):
    return x * _sigmoid(x)


def _dot(a, b):
    return jnp.dot(a, b, preferred_element_type=F32)


def _dot_nt(a, b):
    return lax.dot_general(a, b, (((1,), (1,)), ((), ())), preferred_element_type=F32)


def _split3(x):
    hi = x.astype(BF16)
    r1 = x - hi.astype(F32)
    mid = r1.astype(BF16)
    lo = (r1 - mid.astype(F32)).astype(BF16)
    return hi, mid, lo


def _dot3(x, m):
    hi, mid, lo = _split3(x)
    return _dot(hi, m) + _dot(mid, m) + _dot(lo, m)


def _dot3_left(m, x):
    hi, mid, lo = _split3(x)
    return _dot(m, hi) + _dot(m, mid) + _dot(m, lo)


def _rms(x, gain):
    ms = jnp.mean(x * x, axis=-1, keepdims=True)
    return x * lax.rsqrt(ms + NORM_EPS) * gain


def _mod_kernel(c_ref, w_ref, b_ref, o_ref):
    cond = _silu(c_ref[...]).astype(BF16)
    o_ref[...] = _dot(cond, w_ref[...]) + b_ref[...]


def _modulation(c, ada_w, ada_b):
    depth, d, n = ada_w.shape
    b = c.shape[0]
    rows = 16
    c_pad = jnp.zeros((rows, d), F32).at[:b].set(c)
    tn = 1024
    out = pl.pallas_call(
        _mod_kernel,
        out_shape=jax.ShapeDtypeStruct((depth, rows, n), F32),
        grid=(depth, n // tn),
        in_specs=[
            pl.BlockSpec((rows, d), lambda l, j: (0, 0)),
            pl.BlockSpec((None, d, tn), lambda l, j: (l, 0, j)),
            pl.BlockSpec((None, 1, tn), lambda l, j: (l, 0, j)),
        ],
        out_specs=pl.BlockSpec((None, rows, tn), lambda l, j: (l, 0, j)),
        compiler_params=_params(("parallel", "parallel")),
        name="adaln_mod",
    )(c_pad, ada_w.astype(BF16), ada_b.reshape(depth, 1, n))
    return out[:, :b]


def _modnorm_matmul_kernel(x_ref, g_ref, sc_ref, sh_ref, w_ref, o_ref, h_scr):
    @pl.when(pl.program_id(1) == 0)
    def _():
        y = _rms(x_ref[...].astype(F32), g_ref[...])
        h_scr[...] = (y * (1.0 + sc_ref[...]) + sh_ref[...]).astype(BF16)

    o_ref[...] = _dot(h_scr[...], w_ref[...]).astype(o_ref.dtype)


def _pad_and_tile(n, max_tile=1024, min_tile=512):
    n_pad = -(-n // LANES) * LANES
    while True:
        if n_pad <= max_tile:
            return n_pad, n_pad
        for tn in range(max_tile, min_tile - 1, -LANES):
            if n_pad % tn == 0:
                return n_pad, tn
        n_pad += LANES


def _modnorm_matmul(x, gain, sc, sh, w, seq, out_dtype=F32, tm=1024):
    t, d = x.shape
    tm = min(tm, seq)
    n = w.shape[1]
    n_pad, tn = _pad_and_tile(n)
    w = w.astype(BF16)
    if n_pad != n:
        w = jnp.pad(w, ((0, 0), (0, n_pad - n)))
    tpb = seq // tm
    b = sc.shape[0]
    return pl.pallas_call(
        _modnorm_matmul_kernel,
        out_shape=jax.ShapeDtypeStruct((t, n_pad), out_dtype),
        grid=(t // tm, n_pad // tn),
        in_specs=[
            pl.BlockSpec((tm, d), lambda i, j: (i, 0)),
            pl.BlockSpec((1, d), lambda i, j: (0, 0)),
            pl.BlockSpec((None, 1, d), lambda i, j: (i // tpb, 0, 0)),
            pl.BlockSpec((None, 1, d), lambda i, j: (i // tpb, 0, 0)),
            pl.BlockSpec((d, tn), lambda i, j: (0, j)),
        ],
        out_specs=pl.BlockSpec((tm, tn), lambda i, j: (i, j)),
        scratch_shapes=[pltpu.VMEM((tm, d), BF16)],
        compiler_params=_params(("parallel", "arbitrary")),
        name="modnorm_matmul",
    )(x, gain.reshape(1, d), sc.reshape(b, 1, d), sh.reshape(b, 1, d), w)


def _norm_matmul(x, gain, w, out_dtype=F32):
    t, d = x.shape
    zeros = jnp.zeros((1, d), F32)
    return _modnorm_matmul(x, gain, zeros, zeros, w, seq=t, out_dtype=out_dtype)


def _out_proj_kernel(y_ref, w_ref, x_ref, gate_ref, o_ref):
    o_ref[...] = x_ref[...] + gate_ref[...] * _dot(y_ref[...], w_ref[...])


def _out_proj(y, w, x, gate, seq, tm=512):
    t, k = y.shape
    d = w.shape[1]
    b = gate.shape[0]
    tpb = seq // tm
    return pl.pallas_call(
        _out_proj_kernel,
        out_shape=jax.ShapeDtypeStruct((t, d), F32),
        grid=(t // tm,),
        in_specs=[
            pl.BlockSpec((tm, k), lambda i: (i, 0)),
            pl.BlockSpec((k, d), lambda i: (0, 0)),
            pl.BlockSpec((tm, d), lambda i: (i, 0)),
            pl.BlockSpec((None, 1, d), lambda i: (i // tpb, 0, 0)),
        ],
        out_specs=pl.BlockSpec((tm, d), lambda i: (i, 0)),
        compiler_params=_params(("parallel",)),
        name="out_proj",
    )(y, w.astype(BF16), x, gate.reshape(b, 1, d))


def _ffn_kernel(x_ref, g_ref, sc_ref, sh_ref, wa_ref, wb_ref, wo_ref, gate_ref, o_ref, h_scr, acc_scr):
    j = pl.program_id(1)

    @pl.when(j == 0)
    def _():
        y = _rms(x_ref[...], g_ref[...])
        h_scr[...] = (y * (1.0 + sc_ref[...]) + sh_ref[...]).astype(BF16)
        acc_scr[...] = jnp.zeros_like(acc_scr)

    h = h_scr[...]
    a = _dot(h, wa_ref[...])
    b = _dot(h, wb_ref[...])
    u = (_silu(a) * b).astype(BF16)
    acc_scr[...] += _dot(u, wo_ref[...])

    @pl.when(j == pl.num_programs(1) - 1)
    def _():
        o_ref[...] = x_ref[...] + gate_ref[...] * acc_scr[...]


def _ffn(x, gain, sc, sh, gate, w_in, w_out, seq, tm=512):
    t, d = x.shape
    hid = w_out.shape[0]
    n_h = 2
    th = hid // n_h
    assert th % LANES == 0
    b = sc.shape[0]
    tpb = seq // tm
    w_in = w_in.astype(BF16)
    w_out = w_out.astype(BF16)
    return pl.pallas_call(
        _ffn_kernel,
        out_shape=jax.ShapeDtypeStruct((t, d), F32),
        grid=(t // tm, n_h),
        in_specs=[
            pl.BlockSpec((tm, d), lambda i, j: (i, 0)),
            pl.BlockSpec((1, d), lambda i, j: (0, 0)),
            pl.BlockSpec((None, 1, d), lambda i, j: (i // tpb, 0, 0)),
            pl.BlockSpec((None, 1, d), lambda i, j: (i // tpb, 0, 0)),
            pl.BlockSpec((d, th), lambda i, j: (0, j)),
            pl.BlockSpec((d, th), lambda i, j: (0, n_h + j)),
            pl.BlockSpec((th, d), lambda i, j: (j, 0)),
            pl.BlockSpec((None, 1, d), lambda i, j: (i // tpb, 0, 0)),
        ],
        out_specs=pl.BlockSpec((tm, d), lambda i, j: (i, 0)),
        scratch_shapes=[pltpu.VMEM((tm, d), BF16), pltpu.VMEM((tm, d), F32)],
        compiler_params=_params(("parallel", "arbitrary")),
        name="ffn",
    )(x, gain.reshape(1, d), sc.reshape(b, 1, d), sh.reshape(b, 1, d), w_in, w_in, w_out,
      gate.reshape(b, 1, d))


def _norm_rope_kernel(*refs, scale, rope):
    if rope:
        x_ref, g_ref, c_ref, s_ref, p_ref, o_ref = refs
    else:
        x_ref, g_ref, o_ref = refs
    y = _rms(x_ref[...].astype(F32), g_ref[...])
    if rope:
        hi = y.astype(BF16)
        lo = (y - hi.astype(F32)).astype(BF16)
        rot = _dot(hi, p_ref[...]) + _dot(lo, p_ref[...])
        y = y * c_ref[...] + rot * s_ref[...]
    if scale != 1.0:
        y = y * scale
    o_ref[...] = y.astype(o_ref.dtype)


def _norm_rope(x, gain, cos_t=None, sin_t=None, perm=None, scale=1.0, out_dtype=BF16, ts=2048):
    b, h, s, d = x.shape
    ts = min(ts, s)
    rope = cos_t is not None
    in_specs = [
        pl.BlockSpec((None, None, ts, d), lambda bi, hi, si: (bi, hi, si, 0)),
        pl.BlockSpec((1, d), lambda bi, hi, si: (0, 0)),
    ]
    args = [x, gain.reshape(1, d)]
    if rope:
        in_specs += [
            pl.BlockSpec((None, ts, d), lambda bi, hi, si: (bi, si, 0)),
            pl.BlockSpec((None, ts, d), lambda bi, hi, si: (bi, si, 0)),
            pl.BlockSpec((d, d), lambda bi, hi, si: (0, 0)),
        ]
        args += [cos_t, sin_t, perm]
    return pl.pallas_call(
        functools.partial(_norm_rope_kernel, scale=scale, rope=rope),
        out_shape=jax.ShapeDtypeStruct((b, h, s, d), out_dtype),
        grid=(b, h, s // ts),
        in_specs=in_specs,
        out_specs=pl.BlockSpec((None, None, ts, d), lambda bi, hi, si: (bi, hi, si, 0)),
        compiler_params=_params(("parallel", "parallel", "parallel")),
        name="head_norm_rope" if rope else "head_norm",
    )(*args)


def _rope_tables(pos, d):
    inv = ROPE_THETA ** (-jnp.arange(0, d, 2, dtype=F32) / d)
    ang = pos.astype(F32)[..., None] * inv
    return jnp.cos(ang), jnp.sin(ang)


def _rotate_half_perm(d_total, start, half):
    p = np.zeros((d_total, d_total), np.float32)
    for i in range(half):
        p[start + half + i, start + i] = -1.0
        p[start + i, start + half + i] = 1.0
    return jnp.asarray(p, BF16)


V_ONES_ROWS = 16
FLASH_STRIP = 256
FLASH_UNROLL = 4


def _flash_scratch(tk, nq, dv):
    return [pltpu.VMEM((2, tk, nq), F32), pltpu.VMEM((1, nq), F32), pltpu.VMEM((1, nq), F32),
            pltpu.VMEM((dv + V_ONES_ROWS, nq), F32)]


def _flash_causal_t(get_q_t, k_ref, vt_ref, scratch, n_full, tk, last_mask):
    s_scr, m_scr, cm_scr, acc_scr = scratch
    dv = acc_scr.shape[0] - V_ONES_ROWS
    strips = [slice(r * FLASH_STRIP, (r + 1) * FLASH_STRIP) for r in range(tk // FLASH_STRIP)]

    def scores(j, slot):
        cm = None
        for r, rows in enumerate(strips):
            start = pl.multiple_of(j * tk + r * FLASH_STRIP, FLASH_STRIP)
            s = _dot(k_ref[pl.ds(start, FLASH_STRIP), :], get_q_t(j))
            s_scr[slot, rows, :] = s
            c = jnp.max(s, axis=0, keepdims=True)
            cm = c if cm is None else jnp.maximum(cm, c)
        return cm

    def stage(j, cur, nxt):
        cm_next = scores(j + 1, nxt)
        m = m_scr[...]
        m_new = jnp.maximum(m, cm_scr[...])
        pv = None
        for rows in strips:
            p = jnp.exp2(s_scr[cur, rows, :] - m_new).astype(BF16)
            d = _dot(vt_ref[j, :, rows], p)
            pv = d if pv is None else pv + d
        acc_scr[...] = jnp.exp2(m - m_new) * acc_scr[...] + pv
        m_scr[...] = m_new
        cm_scr[...] = cm_next

    m_scr[...] = jnp.full(m_scr.shape, -jnp.inf, F32)
    acc_scr[...] = jnp.zeros(acc_scr.shape, F32)
    cm_scr[...] = scores(0, 0)

    def group(i, carry):
        for u in range(FLASH_UNROLL):
            stage(FLASH_UNROLL * i + u, u % 2, (u + 1) % 2)
        return carry

    n_groups = n_full // FLASH_UNROLL
    lax.fori_loop(0, n_groups, group, 0)
    for u in range(FLASH_UNROLL - 1):

        @pl.when(n_full - n_groups * FLASH_UNROLL > u)
        def _():
            stage(n_groups * FLASH_UNROLL + u, u % 2, (u + 1) % 2)

    s = jnp.where(last_mask, s_scr[n_full % 2], -jnp.inf)
    m = m_scr[...]
    m_new = jnp.maximum(m, jnp.max(s, axis=0, keepdims=True))
    p = jnp.exp2(s - m_new).astype(BF16)
    acc = jnp.exp2(m - m_new) * acc_scr[...] + _dot(vt_ref[n_full], p)
    return acc[0:dv, :] / acc[dv:dv + 1, :]


def _with_ones_rows_t(v, tk):
    *lead, s, dv = v.shape
    extra = jnp.zeros((*lead, s, V_ONES_ROWS), v.dtype).at[..., 0].set(1.0)
    va = jnp.concatenate([v, extra], axis=-1).astype(BF16)
    va = va.reshape(*lead, s // tk, tk, dv + V_ONES_ROWS)
    return jnp.swapaxes(va, -1, -2)


def _mla_attn_kernel(qt_ref, k_ref, vt_ref, o_ref, *scratch, tk):
    qi = pl.program_id(2)
    tq = qt_ref.shape[1]
    kpos = lax.broadcasted_iota(jnp.int32, (tk, tq), 0)
    qpos = lax.broadcasted_iota(jnp.int32, (tk, tq), 1)
    o = _flash_causal_t(lambda j: qt_ref[...], k_ref, vt_ref, scratch, qi, tk, kpos <= qpos)
    o_ref[...] = o.astype(o_ref.dtype)


def _mla_attention(q_t, k, v_t, tq, tk):
    b, h, dq, s = q_t.shape
    nk, dva = v_t.shape[2], v_t.shape[3]
    dv = dva - V_ONES_ROWS
    assert tq == tk
    return pl.pallas_call(
        functools.partial(_mla_attn_kernel, tk=tk),
        out_shape=jax.ShapeDtypeStruct((b, h, dv, s), BF16),
        grid=(b, h, s // tq),
        in_specs=[
            pl.BlockSpec((None, None, dq, tq), lambda bi, hi, qi: (bi, hi, 0, qi)),
            pl.BlockSpec((None, None, s, dq), lambda bi, hi, qi: (bi, hi, 0, 0)),
            pl.BlockSpec((None, None, nk, dva, tk), lambda bi, hi, qi: (bi, hi, 0, 0, 0)),
        ],
        out_specs=pl.BlockSpec((None, None, dv, tq), lambda bi, hi, qi: (bi, hi, 0, qi)),
        scratch_shapes=_flash_scratch(tk, tq, dv),
        compiler_params=_params(("parallel", "parallel", "arbitrary")),
        name="mla_flash",
    )(q_t, k, v_t)


def _hi_lo_dot(y, m):
    hi = y.astype(BF16)
    lo = (y - hi.astype(F32)).astype(BF16)
    return _dot(hi, m) + _dot(lo, m)


def _mla_q_prep_kernel(x_ref, g_ref, c_ref, s_ref, p_ref, o_ref, *, n_real, scale):
    x = x_ref[...].astype(F32)
    ms = jnp.sum(x * x, axis=-1, keepdims=True) * (1.0 / n_real)
    y = x * lax.rsqrt(ms + NORM_EPS) * g_ref[...]
    y = (y * c_ref[...] + _hi_lo_dot(y, p_ref[...]) * s_ref[...]) * scale
    o_ref[...] = y.T.astype(o_ref.dtype)


def _mla_q_prep(q_raw, gain, cos_t, sin_t, perm, batch, seq, heads, n_real, scale, ts=2048):
    ts = min(ts, seq)
    nsb = seq // ts
    hw = LANES
    const = lambda bi, hi, si: (0, 0)
    return pl.pallas_call(
        functools.partial(_mla_q_prep_kernel, n_real=n_real, scale=scale),
        out_shape=jax.ShapeDtypeStruct((batch, heads, hw, seq), BF16),
        grid=(batch, heads, nsb),
        in_specs=[
            pl.BlockSpec((ts, hw), lambda bi, hi, si: (bi * nsb + si, hi)),
            pl.BlockSpec((1, hw), const),
            pl.BlockSpec((None, ts, hw), lambda bi, hi, si: (bi, si, 0)),
            pl.BlockSpec((None, ts, hw), lambda bi, hi, si: (bi, si, 0)),
            pl.BlockSpec((hw, hw), const),
        ],
        out_specs=pl.BlockSpec((None, None, hw, ts), lambda bi, hi, si: (bi, hi, 0, si)),
        compiler_params=_params(("parallel", "parallel", "parallel")),
        name="mla_q_prep",
    )(q_raw, gain.reshape(1, hw), cos_t, sin_t, perm)


def _mla_kv_prep_kernel(kv_ref, kr_ref, g1_ref, g2_ref, c_ref, s_ref, p_ref, e_ref, k_ref, vt_ref, rot_scr, *,
                        n_nope, n_rope):
    @pl.when(pl.program_id(2) == 0)
    def _():
        r = kr_ref[...].astype(F32)
        rn = r * lax.rsqrt(jnp.sum(r * r, axis=-1, keepdims=True) * (1.0 / n_rope) + NORM_EPS) * g2_ref[...]
        rr = rn * c_ref[...] + _hi_lo_dot(rn, p_ref[...]) * s_ref[...]
        rot_scr[...] = _hi_lo_dot(rr, e_ref[...])

    x = kv_ref[...].astype(F32)
    ts, hw = x.shape
    lane = lax.broadcasted_iota(jnp.int32, (ts, hw), 1)
    xk = jnp.where(lane < n_nope, x, 0.0)
    kn = xk * lax.rsqrt(jnp.sum(xk * xk, axis=-1, keepdims=True) * (1.0 / n_nope) + NORM_EPS) * g1_ref[...]
    k_ref[...] = (kn + rot_scr[...]).astype(k_ref.dtype)
    v_t = x.T[n_nope:, :]
    n_t, _, tk = vt_ref.shape
    ones = jnp.where(lax.broadcasted_iota(jnp.int32, (V_ONES_ROWS, tk), 0) == 0, 1.0, 0.0)
    for i in range(n_t):
        vt_ref[i] = jnp.concatenate([v_t[:, i * tk:(i + 1) * tk], ones], axis=0).astype(vt_ref.dtype)


def _mla_kv_prep(kv_raw, proj, rot_block, g_nope, g_rope, cos_t, sin_t, perm, place, batch, seq, heads, tk,
                 tiles_per_step=4):
    hw = LANES
    n_t = min(tiles_per_step, seq // tk)
    ts = n_t * tk
    ns = seq // ts
    n_nope, n_rope = MLA_NOPE, MLA_ROPE
    dva = hw - n_nope + V_ONES_ROWS
    const = lambda bi, si, hi: (0, 0)
    return pl.pallas_call(
        functools.partial(_mla_kv_prep_kernel, n_nope=n_nope, n_rope=n_rope),
        out_shape=(jax.ShapeDtypeStruct((batch, heads, seq, hw), BF16),
                   jax.ShapeDtypeStruct((batch, heads, seq // tk, dva, tk), BF16)),
        grid=(batch, ns, heads),
        in_specs=[
            pl.BlockSpec((ts, hw), lambda bi, si, hi: (bi * ns + si, hi)),
            pl.BlockSpec((ts, hw), lambda bi, si, hi: (bi * ns + si, rot_block)),
            pl.BlockSpec((1, hw), const),
            pl.BlockSpec((1, hw), const),
            pl.BlockSpec((None, ts, hw), lambda bi, si, hi: (bi, si, 0)),
            pl.BlockSpec((None, ts, hw), lambda bi, si, hi: (bi, si, 0)),
            pl.BlockSpec((hw, hw), const),
            pl.BlockSpec((hw, hw), const),
        ],
        out_specs=(pl.BlockSpec((None, None, ts, hw), lambda bi, si, hi: (bi, hi, si, 0)),
                   pl.BlockSpec((None, None, n_t, dva, tk), lambda bi, si, hi: (bi, hi, si, 0, 0))),
        scratch_shapes=[pltpu.VMEM((ts, hw), F32)],
        compiler_params=_params(("parallel", "parallel", "arbitrary")),
        name="mla_kv_prep",
    )(kv_raw, proj, g_nope.reshape(1, hw), g_rope.reshape(1, hw), cos_t, sin_t, perm, place)


def _out_proj_t_kernel(ot_ref, w_ref, x_ref, gate_ref, o_ref):
    h, dv, tm = ot_ref.shape
    y = ot_ref[...].reshape(h * dv, tm).astype(F32).T.astype(BF16)
    o_ref[...] = x_ref[...] + gate_ref[...] * _dot(y, w_ref[...])


def _out_proj_t(o_t, w, x, gate, seq, tm=512):
    b, h, dv, _ = o_t.shape
    t, d = x.shape
    tm = min(tm, seq)
    tpb = seq // tm
    return pl.pallas_call(
        _out_proj_t_kernel,
        out_shape=jax.ShapeDtypeStruct((t, d), F32),
        grid=(b, tpb),
        in_specs=[
            pl.BlockSpec((None, h, dv, tm), lambda bi, i: (bi, 0, 0, i)),
            pl.BlockSpec((h * dv, d), lambda bi, i: (0, 0)),
            pl.BlockSpec((tm, d), lambda bi, i: (bi * tpb + i, 0)),
            pl.BlockSpec((None, 1, d), lambda bi, i: (bi, 0, 0)),
        ],
        out_specs=pl.BlockSpec((tm, d), lambda bi, i: (bi * tpb + i, 0)),
        compiler_params=_params(("parallel", "parallel")),
        name="out_proj_t",
    )(o_t, w.astype(BF16), x, gate.reshape(b, 1, d))


def _compress_kernel(x_ref, pelo_ref, pehi_ref, w1lo_ref, w1hi_ref, w2_ref, g_ref, o_ref, *, norm):
    x = x_ref[...]
    nc = x.shape[0]
    a = _dot((x + pelo_ref[...]).astype(BF16), w1lo_ref[...])
    b = _dot((x + pehi_ref[...]).astype(BF16), w1hi_ref[...])
    hid = _silu(a + pltpu.roll(b, shift=nc - 1, axis=0))
    y = _dot(hid.astype(BF16), w2_ref[...])
    if norm:
        y = _rms(y, g_ref[...])
    o_ref[...] = y.astype(o_ref.dtype)


def _compress(x, pe, w1, w2, gain, norm):
    b, g, nc, wdt = x.shape
    dh = w2.shape[1]
    hidden = w1.shape[1]
    half = NSA_CMP_STRIDE
    pelo = pe[:half].reshape(1, wdt)
    pehi = pe[half:].reshape(1, wdt)
    w1 = w1.astype(BF16)
    const = lambda bi, gi: (0, 0)
    return pl.pallas_call(
        functools.partial(_compress_kernel, norm=norm),
        out_shape=jax.ShapeDtypeStruct((b, g, nc, dh), BF16),
        grid=(b, g),
        in_specs=[
            pl.BlockSpec((None, None, nc, wdt), lambda bi, gi: (bi, gi, 0, 0)),
            pl.BlockSpec((1, wdt), const),
            pl.BlockSpec((1, wdt), const),
            pl.BlockSpec((wdt, hidden), const),
            pl.BlockSpec((wdt, hidden), const),
            pl.BlockSpec((hidden, dh), const),
            pl.BlockSpec((1, dh), const),
        ],
        out_specs=pl.BlockSpec((None, None, nc, dh), lambda bi, gi: (bi, gi, 0, 0)),
        compiler_params=_params(("parallel", "parallel")),
        name="nsa_compress",
    )(x, pelo, pehi, w1[:wdt], w1[wdt:], w2.astype(BF16), gain.reshape(1, dh))


def _to_token_major(o_t, n_rep, tq):
    stacked = jnp.concatenate([o_t[:, r * tq:(r + 1) * tq] for r in range(n_rep)], axis=0)
    return stacked.T


def _nsa_cmp_kernel(qt_ref, kc_ref, vct_ref, band_ref, o_ref, sb_ref, s_scr, *, tq, n_rep):
    qi = pl.program_id(2)
    q0 = qi * tq
    nq = n_rep * tq
    n_ch, ch = s_scr.shape[0], s_scr.shape[1]
    dh = qt_ref.shape[0]
    nb = band_ref.shape[1]
    q_t = qt_ref[...]
    n_vis = (q0 + tq - NSA_CMP_STRIDE) // NSA_CMP_STRIDE
    n_used = (n_vis + ch - 1) // ch
    tpos = q0 + (lax.broadcasted_iota(jnp.int32, (ch, nq), 1) & (tq - 1))
    end0 = lax.broadcasted_iota(jnp.int32, (ch, nq), 0) * NSA_CMP_STRIDE + (NSA_CMP_LEN - 1)

    def scores(c, cm):
        s = _dot(kc_ref[pl.ds(pl.multiple_of(c * ch, ch), ch), :], q_t)
        s = jnp.where(end0 + c * (ch * NSA_CMP_STRIDE) <= tpos, s, -jnp.inf)
        s_scr[c] = s
        return jnp.maximum(cm, jnp.max(s, axis=0, keepdims=True))

    m = lax.fori_loop(0, n_used, scores, jnp.full((1, nq), -jnp.inf, F32))
    m = jnp.where(m > -jnp.inf, m, 0.0)

    def exps(c, l):
        p = jnp.exp2(s_scr[c] - m)
        s_scr[c] = p
        return l + jnp.sum(p, axis=0, keepdims=True)

    l = lax.fori_loop(0, n_used, exps, jnp.zeros((1, nq), F32))
    inv = 1.0 / jnp.maximum(l, 1e-30)

    def outputs(c, carry):
        o_t, imp = carry
        p = s_scr[c] * inv
        o_t = o_t + _dot(vct_ref[c], p.astype(BF16))
        psum = p[:, 0:tq]
        for r in range(1, n_rep):
            psum = psum + p[:, r * tq:(r + 1) * tq]
        return o_t, imp + _dot3_left(band_ref[c], psum)

    o_t, imp = lax.fori_loop(0, n_used, outputs, (jnp.zeros((dh, nq), F32), jnp.zeros((nb, tq), F32)))
    o_ref[...] = _to_token_major(o_t, n_rep, tq).astype(o_ref.dtype)

    def select(rows):
        blk = lax.broadcasted_iota(jnp.int32, (rows, tq), 0)
        cur = (q0 + lax.broadcasted_iota(jnp.int32, (rows, tq), 1)) // NSA_SEL_LEN
        causal = blk <= cur
        forced = causal & ((blk == 0) | (blk > cur - NSA_N_LOCAL))
        score = jnp.where(forced, 1e30, jnp.where(causal, imp[0:rows, :], -1.0))
        blk_f = blk.astype(F32)
        bias = jnp.full((rows, tq), SEL_MASK_BIAS, F32)
        for _ in range(NSA_N_SEL):
            best = jnp.max(score, axis=0, keepdims=True)
            idx = jnp.min(jnp.where(score == best, blk_f, float(nb)), axis=0, keepdims=True)
            pick = (blk_f == idx) & (best >= 0.0)
            bias = jnp.where(pick, 0.0, bias)
            score = jnp.where(pick, -1.0, score)
        sb_ref[0:rows, :] = bias.astype(sb_ref.dtype)
        if rows < nb:
            sb_ref[rows:nb, :] = jnp.full((nb - rows, tq), SEL_MASK_BIAS, sb_ref.dtype)

    n_causal = (q0 + tq - 1) // NSA_SEL_LEN + 1
    step = nb // NSA_SEL_LEVELS
    for lv in range(NSA_SEL_LEVELS):

        @pl.when((n_causal > lv * step) & (n_causal <= (lv + 1) * step))
        def _(lv=lv):
            select((lv + 1) * step)


NSA_CMP_CHUNK = 512
NSA_SEL_LEVELS = 4


def _nsa_cmp_select(q_t, k_cmp, v_cmp, band, tq):
    b, g, n_q, dh, nq = q_t.shape
    r = nq // tq
    nc = k_cmp.shape[2]
    nb = band.shape[0]
    ch = min(NSA_CMP_CHUNK, nc)
    n_ch = nc // ch
    vc_t = jnp.swapaxes(v_cmp.reshape(b, g, n_ch, ch, dh), -1, -2)
    band_c = band.reshape(nb, n_ch, ch).transpose(1, 0, 2)
    return pl.pallas_call(
        functools.partial(_nsa_cmp_kernel, tq=tq, n_rep=r),
        out_shape=(jax.ShapeDtypeStruct((b * n_q * tq, g * r * dh), BF16),
                   jax.ShapeDtypeStruct((b, g, nb, n_q * tq), BF16)),
        grid=(b, g, n_q),
        in_specs=[
            pl.BlockSpec((None, None, None, dh, nq), lambda bi, gi, qi: (bi, gi, qi, 0, 0)),
            pl.BlockSpec((None, None, nc, dh), lambda bi, gi, qi: (bi, gi, 0, 0)),
            pl.BlockSpec((None, None, n_ch, dh, ch), lambda bi, gi, qi: (bi, gi, 0, 0, 0)),
            pl.BlockSpec((n_ch, nb, ch), lambda bi, gi, qi: (0, 0, 0)),
        ],
        out_specs=(pl.BlockSpec((tq, r * dh), lambda bi, gi, qi: (bi * n_q + qi, gi)),
                   pl.BlockSpec((None, None, nb, tq), lambda bi, gi, qi: (bi, gi, 0, qi))),
        scratch_shapes=[pltpu.VMEM((n_ch, ch, nq), F32)],
        compiler_params=_params(("parallel", "parallel", "parallel")),
        name="nsa_cmp_select",
    )(q_t, k_cmp, vc_t, band_c)


def _nsa_sel_kernel(qt_ref, sb_ref, k_ref, vt_ref, o_ref, qa_scr, *scratch, tq, tk, n_rep, tiles_per_half):
    qi = pl.program_id(2)
    q0 = qi * tq
    nq = n_rep * tq
    n_half = qa_scr.shape[0]
    q_t = qt_ref[...]
    for hh in range(n_half):
        sb = sb_ref[hh * SEL_ONEHOT:(hh + 1) * SEL_ONEHOT, :]
        qa_scr[hh] = jnp.concatenate([jnp.concatenate([sb] * n_rep, axis=1), q_t], axis=0)
    jd = q0 // tk
    kpos = jd * tk + lax.broadcasted_iota(jnp.int32, (tk, nq), 0)
    tpos = q0 + (lax.broadcasted_iota(jnp.int32, (tk, nq), 1) & (tq - 1))
    o = _flash_causal_t(lambda j: qa_scr[j // tiles_per_half], k_ref, vt_ref, scratch, jd, tk, kpos <= tpos)
    o_ref[...] = _to_token_major(o, n_rep, tq).astype(o_ref.dtype)


def _nsa_sel_attention(q_t, sel_bias, k_aug, v_t, tq, tk):
    b, g, n_q, dh, nq = q_t.shape
    nb, s = sel_bias.shape[2], sel_bias.shape[3]
    n_half = nb // SEL_ONEHOT
    ka = k_aug.shape[-1]
    nk, dva = v_t.shape[2], v_t.shape[3]
    tiles_per_half = SEL_ONEHOT * NSA_SEL_LEN // tk
    return pl.pallas_call(
        functools.partial(_nsa_sel_kernel, tq=tq, tk=tk, n_rep=nq // tq, tiles_per_half=tiles_per_half),
        out_shape=jax.ShapeDtypeStruct((b * s, g * (nq // tq) * dh), BF16),
        grid=(b, g, n_q),
        in_specs=[
            pl.BlockSpec((None, None, None, dh, nq), lambda bi, gi, qi: (bi, gi, qi, 0, 0)),
            pl.BlockSpec((None, None, nb, tq), lambda bi, gi, qi: (bi, gi, 0, qi)),
            pl.BlockSpec((None, None, s, ka), lambda bi, gi, qi: (bi, gi, 0, 0)),
            pl.BlockSpec((None, None, nk, dva, tk), lambda bi, gi, qi: (bi, gi, 0, 0, 0)),
        ],
        out_specs=pl.BlockSpec((tq, (nq // tq) * dh), lambda bi, gi, qi: (bi * n_q + qi, gi)),
        scratch_shapes=[pltpu.VMEM((n_half, ka, nq), BF16)] + _flash_scratch(tk, nq, dva - V_ONES_ROWS),
        compiler_params=_params(("parallel", "parallel", "arbitrary")),
        name="nsa_sel_flash",
    )(q_t, sel_bias, k_aug, v_t)


def _nsa_win_kernel(qt_ref, k_ref, vt_ref, o_ref, *, tq, win, n_rep):
    qi = pl.program_id(2)
    q0 = qi * tq
    nq = n_rep * tq
    dv = vt_ref.shape[-2] - V_ONES_ROWS
    n_chunk = (tq + win) // tq
    q_t = qt_ref[...]
    tpos = q0 + (lax.broadcasted_iota(jnp.int32, (tq, nq), 1) & (tq - 1))
    krow = lax.broadcasted_iota(jnp.int32, (tq, nq), 0)
    chunk_ids = [jnp.maximum(qi + c - (n_chunk - 1), 0) for c in range(n_chunk)]
    ss = []
    m = None
    for c in range(n_chunk):
        s = _dot(k_ref[pl.ds(pl.multiple_of(chunk_ids[c] * tq, tq), tq), :], q_t)
        kpos = q0 - win + c * tq + krow
        if c < n_chunk - 1:
            mask = (kpos > tpos - win) & (kpos >= 0)
        else:
            mask = kpos <= tpos
        s = jnp.where(mask, s, -jnp.inf)
        ss.append(s)
        cm = jnp.max(s, axis=0, keepdims=True)
        m = cm if m is None else jnp.maximum(m, cm)
    acc = None
    for c in range(n_chunk):
        d = _dot(vt_ref[chunk_ids[c]], jnp.exp2(ss[c] - m).astype(BF16))
        acc = d if acc is None else acc + d
    o_t = acc[0:dv, :] / acc[dv:dv + 1, :]
    o_ref[...] = _to_token_major(o_t, n_rep, tq).astype(o_ref.dtype)


def _nsa_win_attention(q_t, k_pad, vt_pad, tq):
    b, g, n_q, dh, nq = q_t.shape
    sp = k_pad.shape[2]
    nkc, dva = vt_pad.shape[2], vt_pad.shape[3]
    r = nq // tq
    return pl.pallas_call(
        functools.partial(_nsa_win_kernel, tq=tq, win=NSA_WINDOW, n_rep=r),
        out_shape=jax.ShapeDtypeStruct((b * n_q * tq, g * r * dh), BF16),
        grid=(b, g, n_q),
        in_specs=[
            pl.BlockSpec((None, None, None, dh, nq), lambda bi, gi, qi: (bi, gi, qi, 0, 0)),
            pl.BlockSpec((None, None, sp, dh), lambda bi, gi, qi: (bi, gi, 0, 0)),
            pl.BlockSpec((None, None, nkc, dva, tq), lambda bi, gi, qi: (bi, gi, 0, 0, 0)),
        ],
        out_specs=pl.BlockSpec((tq, r * dh), lambda bi, gi, qi: (bi * n_q + qi, gi)),
        compiler_params=_params(("parallel", "parallel", "arbitrary")),
        name="nsa_window",
    )(q_t, k_pad, vt_pad)


def _nsa_out_kernel(oc_ref, os_ref, ow_ref, gl_ref, e_ref, w_ref, x_ref, gate_ref, o_ref):
    sg = _sigmoid(gl_ref[...].astype(F32))
    hi = sg.astype(BF16)
    lo = (sg - hi.astype(F32)).astype(BF16)

    def expand(i):
        return _dot(hi, e_ref[i]) + _dot(lo, e_ref[i])

    o = expand(0) * oc_ref[...] + expand(1) * os_ref[...] + expand(2) * ow_ref[...]
    o_ref[...] = x_ref[...] + gate_ref[...] * _dot(o.astype(BF16), w_ref[...])


def _nsa_out(o_cmp, o_sel, o_win, gate_logits, expand, w, x, gate, seq, tm=512):
    t, d = x.shape
    k = o_cmp.shape[1]
    gw = gate_logits.shape[1]
    b = gate.shape[0]
    tpb = seq // tm
    row = lambda i: (i, 0)
    return pl.pallas_call(
        _nsa_out_kernel,
        out_shape=jax.ShapeDtypeStruct((t, d), F32),
        grid=(t // tm,),
        in_specs=[
            pl.BlockSpec((tm, k), row),
            pl.BlockSpec((tm, k), row),
            pl.BlockSpec((tm, k), row),
            pl.BlockSpec((tm, gw), row),
            pl.BlockSpec((3, gw, k), lambda i: (0, 0, 0)),
            pl.BlockSpec((k, d), lambda i: (0, 0)),
            pl.BlockSpec((tm, d), row),
            pl.BlockSpec((None, 1, d), lambda i: (i // tpb, 0, 0)),
        ],
        out_specs=pl.BlockSpec((tm, d), row),
        compiler_params=_params(("parallel",)),
        name="nsa_out",
    )(o_cmp, o_sel, o_win, gate_logits, expand, w.astype(BF16), x, gate.reshape(b, 1, d))


def _ret_kernel(lg_ref, q_ref, k_ref, v_ref, g_ref, cos_ref, sin_ref, gn_ref, o_ref, st_scr):
    h = pl.program_id(1)
    c = pl.program_id(2)

    @pl.when(c == 0)
    def _():
        st_scr[...] = jnp.zeros_like(st_scr)

    cn = RET_CHUNK
    dk = q_ref.shape[1]
    half = dk // 2
    lg = jnp.full((1, 1), lg_ref[h], F32)
    n = lax.broadcasted_iota(jnp.int32, (cn, 1), 0).astype(F32)
    rel = (lax.broadcasted_iota(jnp.int32, (cn, cn), 0) - lax.broadcasted_iota(jnp.int32, (cn, cn), 1)).astype(F32)
    decay = jnp.where(rel >= 0, jnp.exp(jnp.maximum(rel, 0.0) * lg), 0.0)
    xi = jnp.exp((n + 1.0) * lg)
    zeta = jnp.exp((cn - 1.0 - n) * lg)
    gamma_c = jnp.exp(cn * lg)

    for sub in range(q_ref.shape[0] // cn):
        rows = slice(sub * cn, (sub + 1) * cn)
        cos = cos_ref[rows, :]
        sin = sin_ref[rows, :]

        def rope(x):
            x1, x2 = x[:, :half], x[:, half:]
            return jnp.concatenate([x1 * cos - x2 * sin, x1 * sin + x2 * cos], axis=1)

        q = rope(q_ref[rows, :].astype(F32))
        k = rope(k_ref[rows, :].astype(F32)) * (dk ** -0.5)
        v = v_ref[rows, :].astype(BF16)
        qb = q.astype(BF16)
        inner = _dot((_dot_nt(qb, k.astype(BF16)) * decay).astype(BF16), v)
        st = st_scr[...]
        cross = _dot(qb, st.astype(BF16)) * xi
        kz_t = (k * zeta).T.astype(BF16)
        st_scr[...] = gamma_c * st + _dot(kz_t, v)
        y = inner + cross
        mu = jnp.mean(y, axis=-1, keepdims=True)
        yc = y - mu
        var = jnp.mean(yc * yc, axis=-1, keepdims=True)
        yn = yc * lax.rsqrt(var + NORM_EPS) * gn_ref[...]
        o_ref[rows, :] = (_silu(g_ref[rows, :].astype(F32)) * yn).astype(o_ref.dtype)


def _retention(proj, cos_t, sin_t, log_gamma, gn_w, batch, seq, heads):
    t = proj.shape[0]
    dk, dv = RET_QK, RET_V
    cn = min(RET_STEP_CHUNKS * RET_CHUNK, seq)
    n_ch = seq // cn
    kb = heads * dk // dk
    vb = 2 * heads * dk // dv
    row = lambda bi, hi, ci, *_: bi * n_ch + ci
    grid_spec = pltpu.PrefetchScalarGridSpec(
        num_scalar_prefetch=1,
        grid=(batch, heads, n_ch),
        in_specs=[
            pl.BlockSpec((cn, dk), lambda bi, hi, ci, lg: (row(bi, hi, ci), hi)),
            pl.BlockSpec((cn, dk), lambda bi, hi, ci, lg: (row(bi, hi, ci), kb + hi)),
            pl.BlockSpec((cn, dv), lambda bi, hi, ci, lg: (row(bi, hi, ci), vb + hi)),
            pl.BlockSpec((cn, dv), lambda bi, hi, ci, lg: (row(bi, hi, ci), vb + heads + hi)),
            pl.BlockSpec((cn, dk // 2), lambda bi, hi, ci, lg: (row(bi, hi, ci), 0)),
            pl.BlockSpec((cn, dk // 2), lambda bi, hi, ci, lg: (row(bi, hi, ci), 0)),
            pl.BlockSpec((1, dv), lambda bi, hi, ci, lg: (0, hi)),
        ],
        out_specs=pl.BlockSpec((cn, dv), lambda bi, hi, ci, lg: (row(bi, hi, ci), hi)),
        scratch_shapes=[pltpu.VMEM((dk, dv), F32)],
    )
    return pl.pallas_call(
        _ret_kernel,
        out_shape=jax.ShapeDtypeStruct((t, heads * dv), BF16),
        grid_spec=grid_spec,
        compiler_params=_params(("parallel", "parallel", "arbitrary")),
        name="retention",
    )(log_gamma, proj, proj, proj, proj, cos_t, sin_t, gn_w.reshape(1, heads * dv))


def _softplus(x):
    return jnp.maximum(x, 0.0) + jnp.log1p(jnp.exp(-jnp.abs(x)))


def _ssd_kernel(row_ref, dt_ref, dtt_ref, cw_ref, cb_ref, dtb_ref, dtbt_ref, al_ref, alt_ref, dsk_ref, nw_ref,
                ex_ref, o_ref, buf_scr, st_scr, *, inner, groups, state, heads):
    c = pl.program_id(1)
    cn = row_ref.shape[0]
    conv_ch = inner + 2 * groups * state
    hp = inner // heads
    rep = heads // groups
    gw = rep * hp
    halo = 8

    @pl.when(c == 0)
    def _():
        buf_scr[0:halo, :] = jnp.zeros((halo, conv_ch), F32)
        st_scr[...] = jnp.zeros_like(st_scr)

    z = row_ref[:, 0:inner].astype(F32)
    xbc = row_ref[:, inner:inner + conv_ch].astype(F32)
    dt_raw = dt_ref[...]

    buf_scr[halo:halo + cn, :] = xbc
    conv = cb_ref[...] + cw_ref[0:1, :] * buf_scr[halo - 3:halo - 3 + cn, :]
    for kk in range(1, SSD_CONV):
        conv = conv + cw_ref[kk:kk + 1, :] * buf_scr[halo - 3 + kk:halo - 3 + kk + cn, :]
    buf_scr[0:halo, :] = xbc[cn - halo:cn, :]
    act = _silu(conv)
    xs = act[:, 0:inner]
    bm = act[:, inner:inner + groups * state]
    cm = act[:, inner + groups * state:conv_ch]

    dt = _softplus(dt_raw + dtb_ref[...])
    dt_t = _softplus(dtt_ref[...] + dtbt_ref[...])
    a = -jnp.exp(al_ref[...])
    a_t = -jnp.exp(alt_ref[...])
    ri = lax.broadcasted_iota(jnp.int32, (cn, cn), 0)
    ci = lax.broadcasted_iota(jnp.int32, (cn, cn), 1)
    tril = ci <= ri
    lower = jnp.where(tril, 1.0, 0.0).astype(BF16)
    upper = jnp.where(ci >= ri, 1.0, 0.0).astype(BF16)
    cum = _dot3_left(lower, dt * a)
    cum_t = _dot3(dt_t * a_t, upper)
    cum_last = cum[cn - 1:cn, :]

    ex = ex_ref[...]
    e_cum = _dot3(jnp.exp(cum), ex)
    w_end = _dot3(jnp.exp(cum_last - cum) * dt, ex)
    e_last = _dot3(jnp.exp(cum_last), ex)

    lane = lax.broadcasted_iota(jnp.int32, (cn, 2 * hp), 1)
    ys = []
    for g in range(groups):
        cc = cm[:, g * state:(g + 1) * state].astype(BF16)
        bc = bm[:, g * state:(g + 1) * state].astype(BF16)
        cb = _dot_nt(cc, bc)
        xg = xs[:, g * gw:(g + 1) * gw]
        st = st_scr[g]
        cross = _dot(cc, st.astype(BF16)) * e_cum[:, g * gw:(g + 1) * gw]
        xw_t = bc.astype(F32).T.astype(BF16)
        st_scr[g] = e_last[:, g * gw:(g + 1) * gw] * st + _dot(
            xw_t, (xg * w_end[:, g * gw:(g + 1) * gw]).astype(BF16))
        intra = []
        for pr in range(rep // 2):
            ws = []
            for hh in (2 * pr, 2 * pr + 1):
                hd = g * rep + hh
                seg = jnp.where(tril, cum[:, hd:hd + 1] - cum_t[hd:hd + 1, :], -jnp.inf)
                ws.append((jnp.exp(seg) * cb * dt_t[hd:hd + 1, :]).astype(BF16))
            w2 = jnp.concatenate(ws, axis=1)
            xp = xg[:, 2 * pr * hp:(2 * pr + 2) * hp]
            x2 = jnp.concatenate([jnp.where(lane < hp, xp, 0.0), jnp.where(lane >= hp, xp, 0.0)], axis=0)
            intra.append(_dot(w2, x2.astype(BF16)))
        ys.append(jnp.concatenate(intra, axis=1) + cross)
    y = jnp.concatenate(ys, axis=1) + dsk_ref[...] * xs
    y = y * _silu(z)
    outs = []
    for g in range(groups):
        outs.append(_rms(y[:, g * gw:(g + 1) * gw], nw_ref[:, g * gw:(g + 1) * gw]))
    o_ref[...] = jnp.concatenate(outs, axis=1).astype(o_ref.dtype)


def _ssd(proj, dt_raw, conv_w, conv_b, dt_bias, a_log, d_skip, norm_w, batch, seq, inner, heads):
    t, n_pad = proj.shape
    groups, state, cn = SSD_GROUPS, SSD_STATE, SSD_CHUNK
    conv_ch = inner + 2 * groups * state
    n_ch = seq // cn
    hp = inner // heads
    gw = inner // groups
    hpad = dt_raw.shape[1]
    assert heads <= hpad and n_pad >= inner + conv_ch
    dt_t = dt_raw.reshape(batch, seq, hpad).transpose(0, 2, 1)
    ex_np = np.zeros((hpad, inner), np.float32)
    ex_np[:heads] = np.kron(np.eye(heads, dtype=np.float32), np.ones((1, hp), np.float32))
    ex = jnp.asarray(ex_np, BF16)
    dt_bias = jnp.pad(dt_bias, (0, hpad - heads))
    a_log = jnp.pad(a_log, (0, hpad - heads))
    heads_k = heads
    heads = hpad
    const = lambda bi, ci: (0, 0)
    return pl.pallas_call(
        functools.partial(_ssd_kernel, inner=inner, groups=groups, state=state, heads=heads_k),
        out_shape=jax.ShapeDtypeStruct((t, inner), BF16),
        grid=(batch, n_ch),
        in_specs=[
            pl.BlockSpec((cn, n_pad), lambda bi, ci: (bi * n_ch + ci, 0)),
            pl.BlockSpec((cn, heads), lambda bi, ci: (bi * n_ch + ci, 0)),
            pl.BlockSpec((None, heads, cn), lambda bi, ci: (bi, 0, ci)),
            pl.BlockSpec((SSD_CONV, conv_ch), const),
            pl.BlockSpec((1, conv_ch), const),
            pl.BlockSpec((1, heads), const),
            pl.BlockSpec((heads, 1), const),
            pl.BlockSpec((1, heads), const),
            pl.BlockSpec((heads, 1), const),
            pl.BlockSpec((1, inner), const),
            pl.BlockSpec((1, inner), const),
            pl.BlockSpec((heads, inner), const),
        ],
        out_specs=pl.BlockSpec((cn, inner), lambda bi, ci: (bi * n_ch + ci, 0)),
        scratch_shapes=[pltpu.VMEM((8 + cn, conv_ch), F32), pltpu.VMEM((groups, state, gw), F32)],
        compiler_params=_params(("parallel", "arbitrary")),
        name="ssd_scan",
    )(proj, dt_raw, dt_t, conv_w, conv_b.reshape(1, conv_ch), dt_bias.reshape(1, heads), dt_bias.reshape(heads, 1),
      a_log.reshape(1, heads), a_log.reshape(heads, 1), jnp.repeat(d_skip, hp).reshape(1, inner),
      norm_w.reshape(1, inner), ex)


def _nsa_layer(x, mods, norm_g, batch, seq, w_in, q_norm, k_norm, cmp_pe, cmp_w1, cmp_w2, w_out):
    sh, sc, gate = mods
    d = x.shape[1]
    dh, g = HEAD_DIM, NSA_GROUPS
    heads = d // dh
    r = heads // g
    kvw = g * dh
    proj = _modnorm_matmul(x, norm_g, sc, sh, w_in, seq, out_dtype=BF16)
    off = heads * dh

    def heads_major(cols, n_heads):
        return cols.reshape(batch, seq, n_heads, dh).transpose(0, 2, 1, 3)

    q = heads_major(proj[:, :off], heads)
    parts = [proj[:, off + i * kvw: off + (i + 1) * kvw] for i in range(6)]
    k_c, v_c, k_s, v_s, k_w, v_w = parts
    gate_logits = proj[:, off + 6 * kvw: off + 6 * kvw + LANES]

    tq, tk = NSA_TQ, min(NSA_TK, seq)
    n_q = seq // tq
    qn = _norm_rope(q, q_norm, scale=dh ** -0.5 * LOG2E)
    q_t = qn.reshape(batch, g, r, n_q, tq, dh).transpose(0, 1, 3, 5, 2, 4).reshape(batch, g, n_q, dh, r * tq)

    def chunk_rows(cols):
        st = NSA_CMP_STRIDE
        return cols.reshape(batch, seq // st, st, g, dh).transpose(0, 3, 1, 2, 4).reshape(batch, g, seq // st, st * dh)

    k_cmp = _compress(chunk_rows(k_c), cmp_pe[0], cmp_w1[0], cmp_w2[0], k_norm[0], norm=True)
    v_cmp = _compress(chunk_rows(v_c), cmp_pe[1], cmp_w1[1], cmp_w2[1], k_norm[0], norm=False)

    n_blk = seq // NSA_SEL_LEN
    n_cmp = seq // NSA_CMP_STRIDE
    assert n_blk % SEL_ONEHOT == 0
    ratio = NSA_SEL_LEN // NSA_CMP_STRIDE
    jj = np.arange(n_blk)[:, None]
    cc = np.arange(n_cmp)[None, :]
    band = jnp.asarray(((cc >= ratio * jj - 1) & (cc <= ratio * jj + ratio - 1)).astype(np.float32), BF16)
    o_cmp, sel_bias = _nsa_cmp_select(q_t, k_cmp, v_cmp, band, tq)

    ks_n = _norm_rope(heads_major(k_s, g), k_norm[1])
    onehot = jnp.asarray(np.eye(SEL_ONEHOT, dtype=np.float32)[(np.arange(seq) // NSA_SEL_LEN) % SEL_ONEHOT], BF16)
    k_aug = jnp.concatenate([jnp.broadcast_to(onehot, (batch, g, seq, SEL_ONEHOT)), ks_n], axis=-1)
    o_sel = _nsa_sel_attention(q_t, sel_bias, k_aug, _with_ones_rows_t(heads_major(v_s, g), tk), tq, tk)

    kw_n = _norm_rope(heads_major(k_w, g), k_norm[2])
    o_win = _nsa_win_attention(q_t, kw_n, _with_ones_rows_t(heads_major(v_w, g), tq), tq)

    e = np.zeros((3, LANES, heads * dh), np.float32)
    for hd in range(heads):
        for i in range(3):
            e[i, hd * 3 + i, hd * dh:(hd + 1) * dh] = 1.0
    return _nsa_out(o_cmp, o_sel, o_win, gate_logits, jnp.asarray(e, BF16), w_out, x, gate, seq)


def _mla_layer(x, mods, norm_g, batch, seq, pos, w_in, q_a_norm, kv_a_norm, w_q_b, w_kv_b, q_norm, k_norm, w_out):
    sh, sc, gate = mods
    d = x.shape[1]
    heads = w_out.shape[0] // MLA_V
    dq = MLA_NOPE + MLA_ROPE
    proj = _modnorm_matmul(x, norm_g, sc, sh, w_in, seq, out_dtype=BF16)
    cq = proj[:, :MLA_Q_LORA]
    ckv = proj[:, MLA_Q_LORA:MLA_Q_LORA + MLA_KV_LORA]
    k_rot = proj[:, MLA_Q_LORA + MLA_KV_LORA:MLA_Q_LORA + MLA_KV_LORA + MLA_ROPE]
    assert MLA_NOPE + MLA_V == LANES and dq <= LANES
    w_q_pad = jnp.pad(w_q_b.reshape(MLA_Q_LORA, heads, dq), ((0, 0), (0, 0), (0, LANES - dq)))
    q_raw = _norm_matmul(cq, q_a_norm, w_q_pad.reshape(MLA_Q_LORA, heads * LANES), out_dtype=BF16)
    kv_raw = _norm_matmul(ckv, kv_a_norm, w_kv_b, out_dtype=BF16)

    half = MLA_ROPE // 2
    cos, sin = _rope_tables(pos, MLA_ROPE)
    ones = jnp.ones((batch, seq, MLA_NOPE), F32)
    zq = jnp.zeros((batch, seq, LANES - dq), F32)
    zk = jnp.zeros((batch, seq, LANES - MLA_ROPE), F32)
    cos_q = jnp.concatenate([ones, cos, cos, zq], axis=-1)
    sin_q = jnp.concatenate([0.0 * ones, sin, sin, zq], axis=-1)
    cos_k = jnp.concatenate([cos, cos, zk], axis=-1)
    sin_k = jnp.concatenate([sin, sin, zk], axis=-1)
    pad_to = lambda v: jnp.pad(v, (0, LANES - v.shape[0]))
    tq = min(MLA_TQ, seq)
    tk = min(MLA_TK, tq)
    q_t = _mla_q_prep(q_raw, pad_to(q_norm), cos_q, sin_q, _rotate_half_perm(LANES, MLA_NOPE, half),
                      batch, seq, heads, n_real=dq, scale=dq ** -0.5 * LOG2E)
    place = np.zeros((LANES, LANES), np.float32)
    place[np.arange(MLA_ROPE), MLA_NOPE + np.arange(MLA_ROPE)] = 1.0
    rot_block = (MLA_Q_LORA + MLA_KV_LORA) // LANES
    assert rot_block * LANES == MLA_Q_LORA + MLA_KV_LORA and proj.shape[1] >= (rot_block + 1) * LANES
    k, v_t = _mla_kv_prep(kv_raw, proj, rot_block, pad_to(k_norm[:MLA_NOPE]), pad_to(k_norm[MLA_NOPE:]),
                          cos_k, sin_k, _rotate_half_perm(LANES, 0, half), jnp.asarray(place, BF16),
                          batch, seq, heads, tk)
    o_t = _mla_attention(q_t, k, v_t, tq, tk)
    return _out_proj_t(o_t, w_out, x, gate, seq)


def _ret_layer(x, mods, norm_g, batch, seq, pos, w_in, gn_w, w_out):
    sh, sc, gate = mods
    heads = w_out.shape[0] // RET_V
    proj = _modnorm_matmul(x, norm_g, sc, sh, w_in, seq, out_dtype=BF16)
    cos, sin = _rope_tables(pos, RET_QK)
    cos = cos.reshape(batch * seq, RET_QK // 2)
    sin = sin.reshape(batch * seq, RET_QK // 2)
    log_gamma = jnp.log1p(-(2.0 ** (-5.0 - jnp.arange(heads, dtype=F32))))
    y = _retention(proj, cos, sin, log_gamma, gn_w, batch, seq, heads)
    return _out_proj(y, w_out, x, gate, seq)


def _ssd_layer(x, mods, norm_g, batch, seq, w_in, conv_w, conv_b, dt_bias, a_log, d_skip, norm_w, w_out):
    sh, sc, gate = mods
    inner = w_out.shape[0]
    heads = dt_bias.shape[0]
    conv_ch = conv_w.shape[1]
    proj = _modnorm_matmul(x, norm_g, sc, sh, w_in[:, :inner + conv_ch], seq, out_dtype=BF16)
    dt_raw = _modnorm_matmul(x, norm_g, sc, sh, w_in[:, inner + conv_ch:], seq)
    y = _ssd(proj, dt_raw, conv_w, conv_b, dt_bias, a_log, d_skip, norm_w, batch, seq, inner, heads)
    return _out_proj(y, w_out, x, gate, seq)


def kernel(x, c, positions, ada_w, ada_b, norm_mix, norm_ffn, ffn_w_in, ffn_w_out, nsa_w_in, nsa_q_norm, nsa_k_norm, nsa_cmp_pe, nsa_cmp_w1, nsa_cmp_w2, nsa_w_out, mla_w_in, mla_q_a_norm, mla_kv_a_norm, mla_w_q_b, mla_w_kv_b, mla_q_norm, mla_k_norm, mla_w_out, ret_w_in, ret_gn_w, ret_w_out, ssd_w_in, ssd_conv_w, ssd_conv_b, ssd_dt_bias, ssd_a_log, ssd_d, ssd_norm, ssd_w_out):
    batch, seq, d = x.shape
    depth = ada_w.shape[0]
    n_mixers = 4
    mod = _modulation(c, ada_w, ada_b)
    xt = x.reshape(batch * seq, d)
    for i in range(depth):
        sh_m, sc_m, g_m, sh_f, sc_f, g_f = [mod[i, :, k * d:(k + 1) * d] for k in range(6)]
        mods = (sh_m, sc_m, g_m)
        kind, j = i % n_mixers, i // n_mixers
        if kind == 0:
            xt = _nsa_layer(xt, mods, norm_mix[i], batch, seq, nsa_w_in[j], nsa_q_norm[j], nsa_k_norm[j],
                            nsa_cmp_pe[j], nsa_cmp_w1[j], nsa_cmp_w2[j], nsa_w_out[j])
        elif kind == 1:
            xt = _mla_layer(xt, mods, norm_mix[i], batch, seq, positions, mla_w_in[j], mla_q_a_norm[j],
                            mla_kv_a_norm[j], mla_w_q_b[j], mla_w_kv_b[j], mla_q_norm[j], mla_k_norm[j], mla_w_out[j])
        elif kind == 2:
            xt = _ret_layer(xt, mods, norm_mix[i], batch, seq, positions, ret_w_in[j], ret_gn_w[j], ret_w_out[j])
        else:
            xt = _ssd_layer(xt, mods, norm_mix[i], batch, seq, ssd_w_in[j], ssd_conv_w[j], ssd_conv_b[j],
                            ssd_dt_bias[j], ssd_a_log[j], ssd_d[j], ssd_norm[j], ssd_w_out[j])
        xt = _ffn(xt, norm_ffn[i], sc_f, sh_f, g_f, ffn_w_in[i], ffn_w_out[i], seq)
    return xt.reshape(batch, seq, d)
```
